```python
import math
import jax, jax.numpy as jnp
from jax import lax
import numpy as np

D_MODEL = 1024
BATCH = 32
SEQ = 256
DEPTH = 4
DEC_BATCH = 4
DEC_SEQ = 2048
PAST_LEN = 256

GRID_W = 64
N_MIXERS = 2
N_FOURIER_LAYERS = (DEPTH + 1) // 2
N_ATTN_LAYERS = DEPTH // 2
N_HEADS = 8
HEAD_DIM = 64
V_DIM = 2 * HEAD_DIM
N_FOURIER_GROUPS = 4
FOURIER_GROUP = D_MODEL // N_FOURIER_GROUPS
D_FF = 2816
CONV_WIDTH = 3
ROPE_THETA = 10000.0
AXIS_DIM = HEAD_DIM // 2
N_FREQ = AXIS_DIM // 2
Q_BLOCK = 128
DN_ALPHA = (2 * DEPTH) ** 0.25
DN_BETA = (8 * DEPTH) ** -0.25
LN_EPS = 1e-6
SUBLN_EPS = 1e-5

kernel_name = "hybrid_fourier_diffattn_dit_step"


def _ln_f32(x):
    x32 = x.astype(jnp.float32)
    mu = jnp.mean(x32, axis=-1, keepdims=True)
    var = jnp.mean(jnp.square(x32 - mu), axis=-1, keepdims=True)
    return (x32 - mu) * lax.rsqrt(var + LN_EPS)


def modulate(x, shift, scale):
    return (_ln_f32(x) * (1.0 + scale.astype(jnp.float32)) + shift.astype(jnp.float32)).astype(x.dtype)


def post_norm(x, update, g, b):
    z = DN_ALPHA * x.astype(jnp.float32) + update.astype(jnp.float32)
    y = _ln_f32(z) * g.astype(jnp.float32) + b.astype(jnp.float32)
    return y.astype(x.dtype)


def ada_params(cvec, w, b):
    a = jax.nn.silu(cvec) @ w + b
    return jnp.split(a[:, None, :], 6, axis=-1)


def fourier_mix(h, w_f):
    bn, n, d = h.shape
    hg = h.astype(jnp.float32).reshape(bn, n, N_FOURIER_GROUPS, FOURIER_GROUP)
    f = jnp.fft.fft2(hg, axes=(1, 3), norm="ortho").real
    return f.reshape(bn, n, d).astype(h.dtype) @ w_f


def conv_ffn(h, w_up, conv_w, conv_b, w_down):
    n = h.shape[1]
    u = h @ w_up
    up = jnp.pad(u, ((0, 0), (1, 1), (0, 0)))
    u = up[:, 0:n] * conv_w[0] + up[:, 1:n + 1] * conv_w[1] + up[:, 2:n + 2] * conv_w[2] + conv_b
    a, g = jnp.split(u, 2, axis=-1)
    return (jax.nn.silu(a) * g) @ w_down


def axial_rope_tables(n, dtype):
    rows = n // GRID_W
    row = jnp.repeat(jnp.arange(rows), GRID_W).astype(jnp.float32)
    col = jnp.tile(jnp.arange(GRID_W), rows).astype(jnp.float32)
    inv = 1.0 / (ROPE_THETA ** (jnp.arange(N_FREQ, dtype=jnp.float32) / N_FREQ))
    ar = row[:, None] * inv
    ac = col[:, None] * inv
    ang = jnp.concatenate([ar, ar, ac, ac], axis=-1)
    return jnp.cos(ang).astype(dtype), jnp.sin(ang).astype(dtype)


def rotate_half_axial(x):
    xs = x.reshape(x.shape[:-1] + (2, 2, N_FREQ))
    xs = jnp.concatenate([-xs[..., 1:, :], xs[..., :1, :]], axis=-2)
    return xs.reshape(x.shape)


def apply_rope(x, cos, sin):
    c = cos[None, :, None, None, :]
    s = sin[None, :, None, None, :]
    return x * c + rotate_half_axial(x) * s


def qkv_proj(h, w):
    bn, n, _ = h.shape
    q, k, v = jnp.split(h @ w, 3, axis=-1)
    return (q.reshape(bn, n, N_HEADS, 2, HEAD_DIM),
            k.reshape(bn, n, N_HEADS, 2, HEAD_DIM),
            v.reshape(bn, n, N_HEADS, V_DIM))


def diff_lambda(lq1, lk1, lq2, lk2, lam_init):
    f = jnp.float32
    return (jnp.exp(jnp.sum(lq1.astype(f) * lk1.astype(f)))
            - jnp.exp(jnp.sum(lq2.astype(f) * lk2.astype(f))) + lam_init)


def diff_attention(q, k, v, lam):
    bn, nq = q.shape[0], q.shape[1]
    nblk = nq // Q_BLOCK
    qb = jnp.moveaxis(q.reshape(bn, nblk, Q_BLOCK, N_HEADS, 2, HEAD_DIM), 1, 0)
    scale = HEAD_DIM ** -0.5

    def one_block(qblk):
        s = jnp.einsum('bqhtd,bkhtd->bhtqk', qblk, k).astype(jnp.float32) * scale
        p = jax.nn.softmax(s, axis=-1)
        w = p[:, :, 0] - lam * p[:, :, 1]
        return jnp.einsum('bhqk,bkhe->bqhe', w.astype(v.dtype), v)

    out = lax.map(one_block, qb)
    return jnp.moveaxis(out, 0, 1).reshape(bn, nq, N_HEADS, V_DIM)


def diff_attn_output(o, g, lam_init, w_o):
    bn, n = o.shape[0], o.shape[1]
    o32 = o.astype(jnp.float32)
    o32 = o32 * lax.rsqrt(jnp.mean(jnp.square(o32), axis=-1, keepdims=True) + SUBLN_EPS)
    o32 = o32 * g.astype(jnp.float32) * (1.0 - lam_init)
    return o32.reshape(bn, n, D_MODEL).astype(o.dtype) @ w_o


def setup_inputs(seed: int = 0) -> dict:
    key = jax.random.key(seed)
    ks = jax.random.split(key, 24)
    nrm = jax.random.normal
    f = jnp.float32
    d = D_MODEL
    v_col_scale = jnp.concatenate([jnp.ones((2 * d,), f), jnp.full((d,), DN_BETA, f)])
    return {
        "x_prompt": nrm(ks[0], (BATCH, SEQ, d), f),
        "x_sample": nrm(ks[1], (DEC_BATCH, DEC_SEQ, d), f),
        "cache_k": nrm(ks[2], (DEC_BATCH, N_ATTN_LAYERS, PAST_LEN, N_HEADS, 2, HEAD_DIM), f),
        "cache_v": nrm(ks[3], (DEC_BATCH, N_ATTN_LAYERS, PAST_LEN, N_HEADS, V_DIM), f) * DN_BETA,
        "c": nrm(ks[4], (DEC_BATCH, d), f),
        "c_ctx": nrm(ks[5], (d,), f),
        "w_ada": nrm(ks[6], (DEPTH, d, 6 * d), f) * d ** -0.5,
        "b_ada": nrm(ks[7], (DEPTH, 6 * d), f) * 0.02,
        "w_fourier": nrm(ks[8], (N_FOURIER_LAYERS, d, d), f) * (d ** -0.5 * DN_BETA),
        "w_qkv": nrm(ks[9], (N_ATTN_LAYERS, d, 3 * d), f) * d ** -0.5 * v_col_scale,
        "lambda_q1": nrm(ks[10], (N_ATTN_LAYERS, HEAD_DIM), f) * 0.1,
        "lambda_k1": nrm(ks[11], (N_ATTN_LAYERS, HEAD_DIM), f) * 0.1,
        "lambda_q2": nrm(ks[12], (N_ATTN_LAYERS, HEAD_DIM), f) * 0.1,
        "lambda_k2": nrm(ks[13], (N_ATTN_LAYERS, HEAD_DIM), f) * 0.1,
        "subln_g": 1.0 + 0.02 * nrm(ks[14], (N_ATTN_LAYERS, V_DIM), f),
        "w_o": nrm(ks[15], (N_ATTN_LAYERS, d, d), f) * (d ** -0.5 * DN_BETA),
        "w_up": nrm(ks[16], (DEPTH, d, 2 * D_FF), f) * (d ** -0.5 * DN_BETA),
        "conv_w": nrm(ks[17], (DEPTH, CONV_WIDTH, 2 * D_FF), f) * CONV_WIDTH ** -0.5,
        "conv_b": nrm(ks[18], (DEPTH, 2 * D_FF), f) * 0.02,
        "w_down": nrm(ks[19], (DEPTH, D_FF, d), f) * (D_FF ** -0.5 * DN_BETA),
        "ln1_g": 1.0 + 0.02 * nrm(ks[20], (DEPTH, d), f),
        "ln1_b": 0.02 * nrm(ks[21], (DEPTH, d), f),
        "ln2_g": 1.0 + 0.02 * nrm(ks[22], (DEPTH, d), f),
        "ln2_b": 0.02 * nrm(ks[23], (DEPTH, d), f),
    }


def reference(x_prompt, x_sample, cache_k, cache_v, c, c_ctx, w_ada, b_ada, w_fourier, w_qkv,
              lambda_q1, lambda_k1, lambda_q2, lambda_k2, subln_g, w_o, w_up, conv_w, conv_b,
              w_down, ln1_g, ln1_b, ln2_g, ln2_b):
    n_lat = x_sample.shape[1]
    cos, sin = axial_rope_tables(n_lat, x_sample.dtype)
    c_ctx_row = c_ctx[None, :]
    xc, xl = x_prompt, x_sample
    new_k, new_v = [], []
    for i in range(DEPTH):
        j = i // N_MIXERS
        sh1c, sc1c, g1c, sh2c, sc2c, g2c = ada_params(c_ctx_row, w_ada[i], b_ada[i])
        sh1l, sc1l, g1l, sh2l, sc2l, g2l = ada_params(c, w_ada[i], b_ada[i])
        hc = modulate(xc, sh1c, sc1c)
        hl = modulate(xl, sh1l, sc1l)
        if i % N_MIXERS == 0:
            mc = fourier_mix(hc, w_fourier[j])
            ml = fourier_mix(hl, w_fourier[j])
        else:
            lam_init = 0.8 - 0.6 * math.exp(-0.3 * i)
            lam = diff_lambda(lambda_q1[j], lambda_k1[j], lambda_q2[j], lambda_k2[j], lam_init)
            qc, kc, vc = qkv_proj(hc, w_qkv[j])
            oc = diff_attention(qc, kc, vc, lam)
            mc = diff_attn_output(oc, subln_g[j], lam_init, w_o[j])
            new_k.append(kc)
            new_v.append(vc)
            ql, kl, vl = qkv_proj(hl, w_qkv[j])
            ql = apply_rope(ql, cos, sin)
            kl = apply_rope(kl, cos, sin)
            k_all = jnp.concatenate([kl, cache_k[:, j]], axis=1)
            v_all = jnp.concatenate([vl, cache_v[:, j]], axis=1)
            ol = diff_attention(ql, k_all, v_all, lam)
            ml = diff_attn_output(ol, subln_g[j], lam_init, w_o[j])
        xc = post_norm(xc, g1c * mc, ln1_g[i], ln1_b[i])
        xl = post_norm(xl, g1l * ml, ln1_g[i], ln1_b[i])
        fc = conv_ffn(modulate(xc, sh2c, sc2c), w_up[i], conv_w[i], conv_b[i], w_down[i])
        fl = conv_ffn(modulate(xl, sh2l, sc2l), w_up[i], conv_w[i], conv_b[i], w_down[i])
        xc = post_norm(xc, g2c * fc, ln2_g[i], ln2_b[i])
        xl = post_norm(xl, g2l * fl, ln2_g[i], ln2_b[i])
    new_cache_k = jnp.stack(new_k, axis=1)
    new_cache_v = jnp.stack(new_v, axis=1)
    return (xc, xl, new_cache_k, new_cache_v)
```

```python
import functools
import math

import jax
import jax.numpy as jnp
from jax import lax
from jax.experimental import pallas as pl
from jax.experimental.pallas import tpu as pltpu

F32 = jnp.float32
BF16 = jnp.bfloat16

D_MODEL = 1024
DEPTH = 4
GRID_W = 64
N_HEADS = 8
HEAD_DIM = 64
V_DIM = 2 * HEAD_DIM
N_FOURIER_GROUPS = 4
FOURIER_GROUP = D_MODEL // N_FOURIER_GROUPS
D_FF = 2816
ROPE_THETA = 10000.0
N_FREQ = HEAD_DIM // 4
DN_ALPHA = (2 * DEPTH) ** 0.25
LN_EPS = 1e-6
SUBLN_EPS = 1e-5

MOD_ROWS = 8
FF_CHUNK = 256
HALO = 16
VMEM_LIMIT = 56 * 1024 * 1024


def _params(sem, vmem=VMEM_LIMIT):
    return pltpu.CompilerParams(dimension_semantics=sem, vmem_limit_bytes=vmem)


def _ln(x):
    mu = jnp.mean(x, axis=-1, keepdims=True)
    xc = x - mu
    var = jnp.mean(xc * xc, axis=-1, keepdims=True)
    return xc * lax.rsqrt(var + LN_EPS)


def _modulate(x, shift, scale):
    return _ln(x) * (1.0 + scale) + shift


def _post_norm(x, update, g, b):
    return _ln(DN_ALPHA * x + update) * g + b


def _silu(x):
    return x / (1.0 + jnp.exp(-x))


def _const_spec(shape):
    return pl.BlockSpec(shape, lambda *_: (0,) * len(shape))


def _ada_kernel(c_ref, w_ref, b_ref, o_ref):
    s = _silu(c_ref[...]).astype(BF16)
    a = jnp.dot(s, w_ref[0].astype(BF16), preferred_element_type=F32)
    o_ref[0] = a + b_ref[0]


def _ada_table(cvec, w_ada, b_ada):
    nc = D_MODEL
    n_col = w_ada.shape[-1] // nc
    out = pl.pallas_call(
        _ada_kernel,
        out_shape=jax.ShapeDtypeStruct((DEPTH, MOD_ROWS, w_ada.shape[-1]), F32),
        grid=(DEPTH, n_col),
        in_specs=[
            pl.BlockSpec((MOD_ROWS, D_MODEL), lambda l, j: (0, 0)),
            pl.BlockSpec((1, D_MODEL, nc), lambda l, j: (l, 0, j)),
            pl.BlockSpec((1, 1, nc), lambda l, j: (l, 0, j)),
        ],
        out_specs=pl.BlockSpec((1, MOD_ROWS, nc), lambda l, j: (l, 0, j)),
        compiler_params=_params(("arbitrary", "arbitrary")),
        name="ada_table",
    )(cvec, w_ada, b_ada.reshape(DEPTH, 1, -1))
    return out.reshape(DEPTH * MOD_ROWS, 6, D_MODEL)


def _ffn_kernel(x_ref, xp_ref, xn_ref, mod_ref, wup_ref, cw_ref, cb_ref, wdn_ref, g_ref, b_ref,
                o_ref, hext_ref, act_ref, *, tm, tiles_per_seq):
    pos = pl.program_id(0) % tiles_per_seq
    shift, scale, gate = mod_ref[0, 3:4, :], mod_ref[0, 4:5, :], mod_ref[0, 5:6, :]
    x = x_ref[...]
    hext_ref[HALO:HALO + tm, :] = _modulate(x, shift, scale).astype(BF16)
    zeros = jnp.zeros((HALO - 8, D_MODEL), F32)
    hp = jnp.where(pos > 0, _modulate(xp_ref[...], shift, scale), 0.0)
    hn = jnp.where(pos < tiles_per_seq - 1, _modulate(xn_ref[...], shift, scale), 0.0)
    hext_ref[0:HALO, :] = jnp.concatenate([zeros, hp], axis=0).astype(BF16)
    hext_ref[HALO + tm:, :] = jnp.concatenate([hn, zeros], axis=0).astype(BF16)
    hext = hext_ref[...]

    def conv(u, col):
        cw = cw_ref[:, col:col + FF_CHUNK]
        return (u[HALO - 1:HALO - 1 + tm] * cw[0:1] + u[HALO:HALO + tm] * cw[1:2]
                + u[HALO + 1:HALO + 1 + tm] * cw[2:3] + cb_ref[:, col:col + FF_CHUNK])

    for c in range(D_FF // FF_CHUNK):
        ca, cg = c * FF_CHUNK, D_FF + c * FF_CHUNK
        ua = jnp.dot(hext, wup_ref[:, ca:ca + FF_CHUNK], preferred_element_type=F32)
        ug = jnp.dot(hext, wup_ref[:, cg:cg + FF_CHUNK], preferred_element_type=F32)
        act_ref[:, ca:ca + FF_CHUNK] = (_silu(conv(ua, ca)) * conv(ug, cg)).astype(BF16)
    f = jnp.dot(act_ref[...], wdn_ref[...], preferred_element_type=F32)
    o_ref[...] = _post_norm(x, gate * f, g_ref[...], b_ref[...])


def _ffn(x, mods, layer, seq_len, mod_row, w_up, conv_w, conv_b, w_down, g, b, tm):
    rows = x.shape[0]
    tiles_per_seq = seq_len // tm
    blocks8 = tm // 8
    kern = functools.partial(_ffn_kernel, tm=tm, tiles_per_seq=tiles_per_seq)
    return pl.pallas_call(
        kern,
        out_shape=jax.ShapeDtypeStruct(x.shape, F32),
        grid=(rows // tm,),
        in_specs=[
            pl.BlockSpec((tm, D_MODEL), lambda i: (i, 0)),
            pl.BlockSpec((8, D_MODEL), lambda i: (jnp.maximum(i * blocks8 - 1, 0), 0)),
            pl.BlockSpec((8, D_MODEL), lambda i: (jnp.minimum((i + 1) * blocks8, rows // 8 - 1), 0)),
            pl.BlockSpec((1, 6, D_MODEL), lambda i: (layer * MOD_ROWS + mod_row(i), 0, 0)),
            _const_spec((D_MODEL, 2 * D_FF)),
            _const_spec((3, 2 * D_FF)),
            _const_spec((1, 2 * D_FF)),
            _const_spec((D_FF, D_MODEL)),
            _const_spec((1, D_MODEL)),
            _const_spec((1, D_MODEL)),
        ],
        out_specs=pl.BlockSpec((tm, D_MODEL), lambda i: (i, 0)),
        scratch_shapes=[
            pltpu.VMEM((tm + 2 * HALO, D_MODEL), BF16),
            pltpu.VMEM((tm, D_FF), BF16),
        ],
        compiler_params=_params(("arbitrary",)),
        name="conv_ffn",
    )(x, x, x, mods, w_up, conv_w, conv_b, w_down, g, b)


def _fourier_kernel(x_ref, mod_ref, cs_ref, dft_ref, wf_ref, g_ref, b_ref, o_ref, y_ref,
                    *, n, tq, rows_per_step):
    j = pl.program_id(1)
    shift, scale, gate = mod_ref[0, 0:1, :], mod_ref[0, 1:2, :], mod_ref[0, 2:3, :]

    @pl.when(j == 0)
    def _():
        for r in range(n // rows_per_step):
            r0 = r * rows_per_step
            h = _modulate(x_ref[r0:r0 + rows_per_step, :], shift, scale).astype(BF16)
            for gi in range(N_FOURIER_GROUPS):
                c0 = gi * FOURIER_GROUP
                y = jnp.dot(h[:, c0:c0 + FOURIER_GROUP], cs_ref[...], preferred_element_type=F32)
                y_ref[r0:r0 + rows_per_step, c0:c0 + FOURIER_GROUP] = y[:, :FOURIER_GROUP].astype(BF16)
                y_ref[n + r0:n + r0 + rows_per_step, c0:c0 + FOURIER_GROUP] = y[:, FOURIER_GROUP:].astype(BF16)

    f = jnp.dot(dft_ref[...], y_ref[...], preferred_element_type=F32)
    m = jnp.dot(f.astype(BF16), wf_ref[...], preferred_element_type=F32)
    x = x_ref[pl.ds(pl.multiple_of(j * tq, tq), tq), :]
    o_ref[...] = _post_norm(x, gate * m, g_ref[...], b_ref[...])


def _fourier(x, mods, layer, seq_len, mod_row, cs, dft, w_f, g, b, tq):
    rows = x.shape[0]
    n_seq = rows // seq_len
    steps = seq_len // tq
    kern = functools.partial(_fourier_kernel, n=seq_len, tq=tq, rows_per_step=min(seq_len, 512))
    return pl.pallas_call(
        kern,
        out_shape=jax.ShapeDtypeStruct(x.shape, F32),
        grid=(n_seq, steps),
        in_specs=[
            pl.BlockSpec((seq_len, D_MODEL), lambda s, j: (s, 0)),
            pl.BlockSpec((1, 6, D_MODEL), lambda s, j: (layer * MOD_ROWS + mod_row(s), 0, 0)),
            _const_spec((FOURIER_GROUP, 2 * FOURIER_GROUP)),
            pl.BlockSpec((tq, 2 * seq_len), lambda s, j: (j, 0)),
            _const_spec((D_MODEL, D_MODEL)),
            _const_spec((1, D_MODEL)),
            _const_spec((1, D_MODEL)),
        ],
        out_specs=pl.BlockSpec((tq, D_MODEL), lambda s, j: (s * steps + j, 0)),
        scratch_shapes=[pltpu.VMEM((2 * seq_len, D_MODEL), BF16)],
        compiler_params=_params(("arbitrary", "arbitrary")),
        name="fourier_mix",
    )(x, mods, cs, dft, w_f, g, b)


def _dft_tables(n):
    k = lax.broadcasted_iota(jnp.int32, (n, n), 0)
    t = lax.broadcasted_iota(jnp.int32, (n, n), 1)
    ang = ((k * t) % n).astype(F32) * (2.0 * math.pi / n)
    dft = jnp.concatenate([jnp.cos(ang), -jnp.sin(ang)], axis=1) * (n ** -0.5)
    return dft.astype(BF16)


def _channel_dft_table():
    n = FOURIER_GROUP
    k = lax.broadcasted_iota(jnp.int32, (n, n), 0)
    t = lax.broadcasted_iota(jnp.int32, (n, n), 1)
    ang = ((k * t) % n).astype(F32) * (2.0 * math.pi / n)
    return (jnp.concatenate([jnp.cos(ang), jnp.sin(ang)], axis=1) * (n ** -0.5)).astype(BF16)


def _qkv_kernel(*refs, tm, rope, cache):
    x_ref, mod_ref, w_ref = refs[:3]
    refs = refs[3:]
    if rope:
        cos_ref, sin_ref = refs[:2]
        refs = refs[2:]
    if cache:
        refs = refs[2:]
    q0_ref, q1_ref, k_ref, v_ref = refs[:4]
    if cache:
        kc_ref, vc_ref = refs[4:6]
    shift, scale = mod_ref[0, 0:1, :], mod_ref[0, 1:2, :]
    h = _modulate(x_ref[...], shift, scale).astype(BF16)
    lane = lax.broadcasted_iota(jnp.int32, (tm, V_DIM), 1)
    first_half = lane < HEAD_DIM
    if rope:
        cos, sin = cos_ref[...], sin_ref[...]
        swap_up = (lane % (2 * N_FREQ)) < N_FREQ

    def rotary(y):
        rot = jnp.where(swap_up, pltpu.roll(y, V_DIM - N_FREQ, 1), pltpu.roll(y, N_FREQ, 1))
        return y * cos + rot * sin

    for hh in range(N_HEADS):
        c0 = hh * V_DIM
        q = jnp.dot(h, w_ref[:, c0:c0 + V_DIM], preferred_element_type=F32)
        k = jnp.dot(h, w_ref[:, D_MODEL + c0:D_MODEL + c0 + V_DIM], preferred_element_type=F32)
        v = jnp.dot(h, w_ref[:, 2 * D_MODEL + c0:2 * D_MODEL + c0 + V_DIM], preferred_element_type=F32)
        if rope:
            q, k = rotary(q), rotary(k)
        q = q * (HEAD_DIM ** -0.5)
        q0_ref[:, c0:c0 + V_DIM] = jnp.where(first_half, q, 0.0).astype(BF16)
        q1_ref[:, c0:c0 + V_DIM] = jnp.where(first_half, 0.0, q).astype(BF16)
        k_ref[:, c0:c0 + V_DIM] = k.astype(BF16)
        v_ref[:, c0:c0 + V_DIM] = v.astype(BF16)
        if cache:
            kc_ref[:, 0, :, c0:c0 + V_DIM] = k.reshape(kc_ref.shape[0], kc_ref.shape[2], V_DIM)
            vc_ref[:, 0, :, c0:c0 + V_DIM] = v.reshape(vc_ref.shape[0], vc_ref.shape[2], V_DIM)


def _qkv(x, mods, layer, seq_len, mod_row, w_qkv, tm, rope_tables=None, cache=None, cache_slot=0):
    rows = x.shape[0]
    rope = rope_tables is not None
    with_cache = cache is not None
    tiles_per_seq = max(seq_len // tm, 1)
    in_specs = [
        pl.BlockSpec((tm, D_MODEL), lambda i: (i, 0)),
        pl.BlockSpec((1, 6, D_MODEL), lambda i: (layer * MOD_ROWS + mod_row(i), 0, 0)),
        _const_spec((D_MODEL, 3 * D_MODEL)),
    ]
    args = [x, mods, w_qkv]
    if rope:
        in_specs += [pl.BlockSpec((tm, V_DIM), lambda i: (i % tiles_per_seq, 0))] * 2
        args += list(rope_tables)
    act = jax.ShapeDtypeStruct((rows, D_MODEL), BF16)
    out_shape = [act] * 4
    out_specs = [pl.BlockSpec((tm, D_MODEL), lambda i: (i, 0))] * 4
    aliases = {}
    if with_cache:
        seqs = tm // seq_len
        ck, cv = cache
        in_specs += [pl.BlockSpec(memory_space=pl.ANY)] * 2
        aliases = {len(args): 4, len(args) + 1: 5}
        args += [ck, cv]
        out_shape += [jax.ShapeDtypeStruct(ck.shape, F32), jax.ShapeDtypeStruct(cv.shape, F32)]
        out_specs += [pl.BlockSpec((seqs, 1, seq_len, D_MODEL), lambda i: (i, cache_slot, 0, 0))] * 2
    kern = functools.partial(_qkv_kernel, tm=tm, rope=rope, cache=with_cache)
    return pl.pallas_call(
        kern,
        out_shape=out_shape,
        grid=(rows // tm,),
        in_specs=in_specs,
        out_specs=out_specs,
        input_output_aliases=aliases,
        compiler_params=_params(("arbitrary",)),
        name="qkv_proj",
    )(*args)


def _attn_kernel(*refs, hb, tq, lam_init, cache):
    lam_ref, g_ref, q0_ref, q1_ref, k_ref, v_ref = refs[:6]
    if cache:
        ck_ref, cv_ref = refs[6:8]
    o_ref = refs[-1]
    lv = lam_ref[...]
    lam = (jnp.exp(jnp.sum(lv[0:1] * lv[1:2], axis=-1, keepdims=True))
           - jnp.exp(jnp.sum(lv[2:3] * lv[3:4], axis=-1, keepdims=True)) + lam_init)
    nt = (((1,), (1,)), ((), ()))
    for hh in range(hb):
        c0 = hh * V_DIM
        qq = jnp.concatenate([q0_ref[:, c0:c0 + V_DIM], q1_ref[:, c0:c0 + V_DIM]], axis=0)
        s = lax.dot_general(qq, k_ref[:, c0:c0 + V_DIM], nt, preferred_element_type=F32)
        m = jnp.max(s, axis=-1, keepdims=True)
        if cache:
            ck = ck_ref[:, c0:c0 + V_DIM].astype(BF16)
            sc = lax.dot_general(qq, ck, nt, preferred_element_type=F32)
            m = jnp.maximum(m, jnp.max(sc, axis=-1, keepdims=True))
        p = jnp.exp(s - m)
        l = jnp.sum(p, axis=-1, keepdims=True)
        if cache:
            pc = jnp.exp(sc - m)
            l = l + jnp.sum(pc, axis=-1, keepdims=True)
        r = 1.0 / l
        r0, r1 = r[:tq], r[tq:] * lam
        w = (p[:tq] * r0 - p[tq:] * r1).astype(BF16)
        o = jnp.dot(w, v_ref[:, c0:c0 + V_DIM], preferred_element_type=F32)
        if cache:
            wc = (pc[:tq] * r0 - pc[tq:] * r1).astype(BF16)
            o = o + jnp.dot(wc, cv_ref[:, c0:c0 + V_DIM].astype(BF16), preferred_element_type=F32)
        o = o * lax.rsqrt(jnp.mean(o * o, axis=-1, keepdims=True) + SUBLN_EPS)
        o = o * g_ref[...] * (1.0 - lam_init)
        o_ref[:, c0:c0 + V_DIM] = o.astype(BF16)


def _attention(q0, q1, k, v, lam_vecs, subln_g, lam_init, seq_len, tq, hb, cache=None):
    rows = q0.shape[0]
    n_seq = rows // seq_len
    q_steps = seq_len // tq
    wb = hb * V_DIM
    with_cache = cache is not None
    q_spec = pl.BlockSpec((tq, wb), lambda s, h, j: (s * q_steps + j, h))
    kv_spec = pl.BlockSpec((seq_len, wb), lambda s, h, j: (s, h))
    in_specs = [_const_spec((4, HEAD_DIM)), _const_spec((1, V_DIM)), q_spec, q_spec, kv_spec, kv_spec]
    args = [lam_vecs, subln_g, q0, q1, k, v]
    if with_cache:
        ck, cv, slot = cache
        past = ck.shape[2]
        c_spec = pl.BlockSpec((None, None, past, wb), lambda s, h, j: (s, slot, 0, h))
        in_specs += [c_spec, c_spec]
        args += [ck, cv]
    kern = functools.partial(_attn_kernel, hb=hb, tq=tq, lam_init=lam_init, cache=with_cache)
    return pl.pallas_call(
        kern,
        out_shape=jax.ShapeDtypeStruct((rows, D_MODEL), BF16),
        grid=(n_seq, N_HEADS // hb, q_steps),
        in_specs=in_specs,
        out_specs=q_spec,
        compiler_params=_params(("arbitrary", "arbitrary", "arbitrary")),
        name="diff_attention",
    )(*args)


def _proj_kernel(x_ref, o_ref_in, mod_ref, w_ref, g_ref, b_ref, out_ref):
    gate = mod_ref[0, 2:3, :]
    m = jnp.dot(o_ref_in[...], w_ref[...], preferred_element_type=F32)
    out_ref[...] = _post_norm(x_ref[...], gate * m, g_ref[...], b_ref[...])


def _out_proj(x, o, mods, layer, mod_row, w_o, g, b, tm):
    rows = x.shape[0]
    row_spec = pl.BlockSpec((tm, D_MODEL), lambda i: (i, 0))
    return pl.pallas_call(
        _proj_kernel,
        out_shape=jax.ShapeDtypeStruct(x.shape, F32),
        grid=(rows // tm,),
        in_specs=[
            row_spec, row_spec,
            pl.BlockSpec((1, 6, D_MODEL), lambda i: (layer * MOD_ROWS + mod_row(i), 0, 0)),
            _const_spec((D_MODEL, D_MODEL)),
            _const_spec((1, D_MODEL)),
            _const_spec((1, D_MODEL)),
        ],
        out_specs=row_spec,
        compiler_params=_params(("arbitrary",)),
        name="attn_out_proj",
    )(x, o, mods, w_o, g, b)


def _rope_tables(n):
    rows = n // GRID_W
    row = jnp.repeat(jnp.arange(rows), GRID_W).astype(F32)
    col = jnp.tile(jnp.arange(GRID_W), rows).astype(F32)
    inv = 1.0 / (ROPE_THETA ** (jnp.arange(N_FREQ, dtype=F32) / N_FREQ))
    ar = row[:, None] * inv
    ac = col[:, None] * inv
    ang = jnp.concatenate([ar, ar, ac, ac], axis=-1)
    sign = jnp.tile(jnp.concatenate([-jnp.ones((N_FREQ,), F32), jnp.ones((N_FREQ,), F32)]), 2)
    cos = jnp.cos(ang)
    sin = jnp.sin(ang) * sign
    return jnp.tile(cos, (1, 2)), jnp.tile(sin, (1, 2))


def kernel(x_prompt, x_sample, cache_k, cache_v, c, c_ctx, w_ada, b_ada, w_fourier, w_qkv,
           lambda_q1, lambda_k1, lambda_q2, lambda_k2, subln_g, w_o, w_up, conv_w, conv_b,
           w_down, ln1_g, ln1_b, ln2_g, ln2_b):
    n_ctx_seq, ctx_len, d = x_prompt.shape
    n_lat_seq, lat_len, _ = x_sample.shape
    past = cache_k.shape[2]
    assert d == D_MODEL and n_lat_seq + 1 <= MOD_ROWS

    cvec = jnp.concatenate(
        [c_ctx[None, :], c, jnp.zeros((MOD_ROWS - 1 - n_lat_seq, d), F32)], axis=0)
    mods = _ada_table(cvec, w_ada, b_ada)

    xc = x_prompt.reshape(n_ctx_seq * ctx_len, d)
    xl = x_sample.reshape(n_lat_seq * lat_len, d)
    ck_in = cache_k.reshape(n_lat_seq, DEPTH // 2, past, d)
    cv_in = cache_v.reshape(n_lat_seq, DEPTH // 2, past, d)

    tm_lat = 512
    ctx_row = lambda i: 0
    cs = _channel_dft_table()
    dft_ctx, dft_lat = _dft_tables(ctx_len), _dft_tables(lat_len)
    rope = _rope_tables(lat_len)
    new_k = jnp.zeros((n_ctx_seq, DEPTH // 2, ctx_len, d), F32)
    new_v = jnp.zeros((n_ctx_seq, DEPTH // 2, ctx_len, d), F32)

    for i in range(DEPTH):
        j = i // 2
        g1, b1 = ln1_g[i][None, :], ln1_b[i][None, :]
        g2, b2 = ln2_g[i][None, :], ln2_b[i][None, :]
        if i % 2 == 0:
            wf = w_fourier[j].astype(BF16)
            xc = _fourier(xc, mods, i, ctx_len, ctx_row, cs, dft_ctx, wf, g1, b1, tq=ctx_len)
            xl = _fourier(xl, mods, i, lat_len, lambda s: 1 + s, cs, dft_lat, wf, g1, b1, tq=256)
        else:
            lam_init = 0.8 - 0.6 * math.exp(-0.3 * i)
            lam_vecs = jnp.stack([lambda_q1[j], lambda_k1[j], lambda_q2[j], lambda_k2[j]])
            sg = subln_g[j][None, :]
            wqkv = w_qkv[j].astype(BF16)
            wo = w_o[j].astype(BF16)
            q0, q1, k, v, new_k, new_v = _qkv(
                xc, mods, i, ctx_len, ctx_row, wqkv, tm=512, cache=(new_k, new_v), cache_slot=j)
            oc = _attention(q0, q1, k, v, lam_vecs, sg, lam_init, ctx_len, tq=ctx_len, hb=N_HEADS)
            xc = _out_proj(xc, oc, mods, i, ctx_row, wo, g1, b1, tm=512)
            lat_row = lambda t: 1 + t // (lat_len // tm_lat)
            q0, q1, k, v = _qkv(xl, mods, i, lat_len, lat_row, wqkv, tm=tm_lat, rope_tables=rope)
            ol = _attention(q0, q1, k, v, lam_vecs, sg, lam_init, lat_len, tq=256, hb=1,
                            cache=(ck_in, cv_in, j))
            xl = _out_proj(xl, ol, mods, i, lat_row, wo, g1, b1, tm=tm_lat)
        wup, wdn = w_up[i].astype(BF16), w_down[i].astype(BF16)
        cw, cb = conv_w[i], conv_b[i][None, :]
        xc = _ffn(xc, mods, i, ctx_len, ctx_row, wup, cw, cb, wdn, g2, b2, tm=ctx_len)
        xl = _ffn(xl, mods, i, lat_len, lambda t: 1 + t // (lat_len // tm_lat),
                  wup, cw, cb, wdn, g2, b2, tm=tm_lat)

    y_prompt = xc.reshape(x_prompt.shape)
    y_sample = xl.reshape(x_sample.shape)
    new_cache_k = new_k.reshape(n_ctx_seq, DEPTH // 2, ctx_len, N_HEADS, 2, HEAD_DIM)
    new_cache_v = new_v.reshape(n_ctx_seq, DEPTH // 2, ctx_len, N_HEADS, V_DIM)
    return (y_prompt, y_sample, new_cache_k, new_cache_v)
```

```python
import functools
import math

import jax
import jax.numpy as jnp
from jax import lax
from jax.experimental import pallas as pl
from jax.experimental.pallas import tpu as pltpu

F32 = jnp.float32
BF16 = jnp.bfloat16

D_MODEL = 1024
DEPTH = 4
GRID_W = 64
N_HEADS = 8
HEAD_DIM = 64
V_DIM = 2 * HEAD_DIM
N_FOURIER_GROUPS = 4
FOURIER_GROUP = D_MODEL // N_FOURIER_GROUPS
D_FF = 2816
ROPE_THETA = 10000.0
N_FREQ = HEAD_DIM // 4
DN_ALPHA = (2 * DEPTH) ** 0.25
LN_EPS = 1e-6
SUBLN_EPS = 1e-5

LANES = 128
SUBLANES = 8
MOD_ROWS = 8
FF_CHUNK = 256
HALO = 16
QKV_CHUNK = 256
VMEM_LIMIT = 56 * 1024 * 1024
Q_SCALE = HEAD_DIM ** -0.5 * math.log2(math.e)


def _params(sem, vmem=VMEM_LIMIT):
    return pltpu.CompilerParams(dimension_semantics=sem, vmem_limit_bytes=vmem)


def _ln(x):
    mu = jnp.mean(x, axis=-1, keepdims=True)
    xc = x - mu
    var = jnp.mean(xc * xc, axis=-1, keepdims=True)
    return xc * lax.rsqrt(var + LN_EPS)


def _modulate(x, shift, scale):
    return _ln(x) * (1.0 + scale) + shift


def _post_norm(x, update, g, b):
    return _ln(DN_ALPHA * x + update) * g + b


def _silu(x):
    return x / (1.0 + jnp.exp(-x))


def _const_spec(shape):
    return pl.BlockSpec(shape, lambda *_: (0,) * len(shape))


def _ada_kernel(c_ref, w_ref, b_ref, o_ref):
    s = _silu(c_ref[...]).astype(BF16)
    a = jnp.dot(s, w_ref[0].astype(BF16), preferred_element_type=F32)
    o_ref[0] = a + b_ref[0]


def _ada_table(cvec, w_ada, b_ada):
    nc = D_MODEL
    n_col = w_ada.shape[-1] // nc
    out = pl.pallas_call(
        _ada_kernel,
        out_shape=jax.ShapeDtypeStruct((DEPTH, MOD_ROWS, w_ada.shape[-1]), F32),
        grid=(DEPTH, n_col),
        in_specs=[
            pl.BlockSpec((MOD_ROWS, D_MODEL), lambda l, j: (0, 0)),
            pl.BlockSpec((1, D_MODEL, nc), lambda l, j: (l, 0, j)),
            pl.BlockSpec((1, 1, nc), lambda l, j: (l, 0, j)),
        ],
        out_specs=pl.BlockSpec((1, MOD_ROWS, nc), lambda l, j: (l, 0, j)),
        compiler_params=_params(("arbitrary", "arbitrary")),
        name="ada_table",
    )(cvec, w_ada, b_ada.reshape(DEPTH, 1, -1))
    return out.reshape(DEPTH * MOD_ROWS, 6, D_MODEL)


def _ffn_kernel(x_ref, xp_ref, xn_ref, mod_ref, wup_ref, cw_ref, cb_ref, wdn_ref, g_ref, b_ref,
                o_ref, slab_ref, hext_ref, act_ref, *, tm, tiles_per_seq):
    seg = tm // SUBLANES
    n_slab = D_MODEL // LANES
    pos = pl.program_id(0) % tiles_per_seq
    shift, scale, gate = mod_ref[0, 3:4, :], mod_ref[0, 4:5, :], mod_ref[0, 5:6, :]
    x = x_ref[...]
    h = _modulate(x, shift, scale)
    for cb in range(n_slab):
        for s in range(SUBLANES):
            slab_ref[cb, pl.ds(s, seg, stride=SUBLANES), :] = (
                h[s * seg:(s + 1) * seg, cb * LANES:(cb + 1) * LANES])
    for cb in range(n_slab):
        hext_ref[0:tm, cb * LANES:(cb + 1) * LANES] = slab_ref[cb].astype(BF16)
    hp = jnp.where(pos > 0, _modulate(xp_ref[...], shift, scale), 0.0)
    hn = jnp.where(pos < tiles_per_seq - 1, _modulate(xn_ref[...], shift, scale), 0.0)
    hext_ref[tm:, :] = jnp.concatenate([hp, hn], axis=0).astype(BF16)
    hext = hext_ref[...]
    row = lax.broadcasted_iota(jnp.int32, (SUBLANES, FF_CHUNK), 0)

    def conv(u, col):
        cw = cw_ref[:, col:col + FF_CHUNK]
        first = jnp.where(row == 0, u[tm + 7:tm + 8], pltpu.roll(u[tm - 8:tm], 1, 0))
        last = jnp.where(row == 7, u[tm + 8:tm + 9], pltpu.roll(u[0:8], 7, 0))
        prev = jnp.concatenate([first, u[0:tm - 8]], axis=0)
        nxt = jnp.concatenate([u[8:tm], last], axis=0)
        return prev * cw[0:1] + u[0:tm] * cw[1:2] + nxt * cw[2:3] + cb_ref[:, col:col + FF_CHUNK]

    for c in range(D_FF // FF_CHUNK):
        ca, cg = c * FF_CHUNK, D_FF + c * FF_CHUNK
        ua = jnp.dot(hext, wup_ref[:, ca:ca + FF_CHUNK], preferred_element_type=F32)
        ug = jnp.dot(hext, wup_ref[:, cg:cg + FF_CHUNK], preferred_element_type=F32)
        act_ref[:, ca:ca + FF_CHUNK] = (_silu(conv(ua, ca)) * conv(ug, cg)).astype(BF16)
    f = jnp.dot(act_ref[...], wdn_ref[...], preferred_element_type=F32)
    for cb in range(n_slab):
        slab_ref[cb] = f[:, cb * LANES:(cb + 1) * LANES]
    for cb in range(n_slab):
        for s in range(SUBLANES):
            o_ref[s * seg:(s + 1) * seg, cb * LANES:(cb + 1) * LANES] = (
                slab_ref[cb, pl.ds(s, seg, stride=SUBLANES), :])
    o_ref[...] = _post_norm(x, gate * o_ref[...], g_ref[...], b_ref[...])


def _ffn(x, mods, layer, seq_len, mod_row, w_up, conv_w, conv_b, w_down, g, b, tm):
    rows = x.shape[0]
    tiles_per_seq = seq_len // tm
    blocks8 = tm // SUBLANES
    kern = functools.partial(_ffn_kernel, tm=tm, tiles_per_seq=tiles_per_seq)
    return pl.pallas_call(
        kern,
        out_shape=jax.ShapeDtypeStruct(x.shape, F32),
        grid=(rows // tm,),
        in_specs=[
            pl.BlockSpec((tm, D_MODEL), lambda i: (i, 0)),
            pl.BlockSpec((SUBLANES, D_MODEL), lambda i: (jnp.maximum(i * blocks8 - 1, 0), 0)),
            pl.BlockSpec((SUBLANES, D_MODEL),
                         lambda i: (jnp.minimum((i + 1) * blocks8, rows // SUBLANES - 1), 0)),
            pl.BlockSpec((1, 6, D_MODEL), lambda i: (layer * MOD_ROWS + mod_row(i), 0, 0)),
            _const_spec((D_MODEL, 2 * D_FF)),
            _const_spec((3, 2 * D_FF)),
            _const_spec((1, 2 * D_FF)),
            _const_spec((D_FF, D_MODEL)),
            _const_spec((1, D_MODEL)),
            _const_spec((1, D_MODEL)),
        ],
        out_specs=pl.BlockSpec((tm, D_MODEL), lambda i: (i, 0)),
        scratch_shapes=[
            pltpu.VMEM((D_MODEL // LANES, tm, LANES), F32),
            pltpu.VMEM((tm + HALO, D_MODEL), BF16),
            pltpu.VMEM((tm, D_FF), BF16),
        ],
        compiler_params=_params(("arbitrary",)),
        name="conv_ffn",
    )(x, x, x, mods, w_up, conv_w, conv_b, w_down, g, b)


def _fourier_kernel(x_ref, mod_ref, cs_ref, dft_ref, wf_ref, g_ref, b_ref, o_ref, y_ref,
                    *, n, tq, rows_per_step):
    j = pl.program_id(1)
    shift, scale, gate = mod_ref[0, 0:1, :], mod_ref[0, 1:2, :], mod_ref[0, 2:3, :]

    @pl.when(j == 0)
    def _():
        for r in range(n // rows_per_step):
            r0 = r * rows_per_step
            h = _modulate(x_ref[r0:r0 + rows_per_step, :], shift, scale).astype(BF16)
            for gi in range(N_FOURIER_GROUPS):
                c0 = gi * FOURIER_GROUP
                y = jnp.dot(h[:, c0:c0 + FOURIER_GROUP], cs_ref[...], preferred_element_type=F32)
                y_ref[r0:r0 + rows_per_step, c0:c0 + FOURIER_GROUP] = y[:, :FOURIER_GROUP].astype(BF16)
                y_ref[n + r0:n + r0 + rows_per_step, c0:c0 + FOURIER_GROUP] = y[:, FOURIER_GROUP:].astype(BF16)

    f = jnp.dot(dft_ref[...], y_ref[...], preferred_element_type=F32)
    m = jnp.dot(f.astype(BF16), wf_ref[...], preferred_element_type=F32)
    x = x_ref[pl.ds(pl.multiple_of(j * tq, tq), tq), :]
    o_ref[...] = _post_norm(x, gate * m, g_ref[...], b_ref[...])


def _fourier(x, mods, layer, seq_len, mod_row, cs, dft, w_f, g, b, tq):
    rows = x.shape[0]
    n_seq = rows // seq_len
    steps = seq_len // tq
    kern = functools.partial(_fourier_kernel, n=seq_len, tq=tq, rows_per_step=min(seq_len, 512))
    return pl.pallas_call(
        kern,
        out_shape=jax.ShapeDtypeStruct(x.shape, F32),
        grid=(n_seq, steps),
        in_specs=[
            pl.BlockSpec((seq_len, D_MODEL), lambda s, j: (s, 0)),
            pl.BlockSpec((1, 6, D_MODEL), lambda s, j: (layer * MOD_ROWS + mod_row(s), 0, 0)),
            _const_spec((FOURIER_GROUP, 2 * FOURIER_GROUP)),
            pl.BlockSpec((tq, 2 * seq_len), lambda s, j: (j, 0)),
            _const_spec((D_MODEL, D_MODEL)),
            _const_spec((1, D_MODEL)),
            _const_spec((1, D_MODEL)),
        ],
        out_specs=pl.BlockSpec((tq, D_MODEL), lambda s, j: (s * steps + j, 0)),
        scratch_shapes=[pltpu.VMEM((2 * seq_len, D_MODEL), BF16)],
        compiler_params=_params(("arbitrary", "arbitrary")),
        name="fourier_mix",
    )(x, mods, cs, dft, w_f, g, b)


def _cos_sin(k, t, n):
    ang = ((k * t) % n).astype(F32) * (2.0 * math.pi / n)
    return jnp.cos(ang), jnp.sin(ang)


def _dft_table(n):
    split = min(n, 64)
    k = lax.broadcasted_iota(jnp.int32, (n, n // split), 0)
    t_hi = lax.broadcasted_iota(jnp.int32, (n, n // split), 1) * split
    ca, sa = _cos_sin(k, t_hi, n)
    k = lax.broadcasted_iota(jnp.int32, (n, split), 0)
    t_lo = lax.broadcasted_iota(jnp.int32, (n, split), 1)
    cb, sb = _cos_sin(k, t_lo, n)
    ca, sa, cb, sb = ca[:, :, None], sa[:, :, None], cb[:, None, :], sb[:, None, :]
    cos = (ca * cb - sa * sb).reshape(n, n)
    sin = (sa * cb + ca * sb).reshape(n, n)
    return (jnp.concatenate([cos, -sin], axis=1) * (n ** -0.5)).astype(BF16)


def _channel_dft_table():
    n = FOURIER_GROUP
    k = lax.broadcasted_iota(jnp.int32, (n, n), 0)
    t = lax.broadcasted_iota(jnp.int32, (n, n), 1)
    cos, sin = _cos_sin(k, t, n)
    return (jnp.concatenate([cos, sin], axis=1) * (n ** -0.5)).astype(BF16)


def _qkv_kernel(*refs, tm, rope, cache):
    x_ref, mod_ref, w_ref = refs[:3]
    refs = refs[3:]
    if rope:
        cos_ref, sin_ref = refs[:2]
        refs = refs[2:]
    if cache:
        refs = refs[2:]
    q0_ref, q1_ref, k_ref, v_ref = refs[:4]
    if cache:
        kc_ref, vc_ref = refs[4:6]
    shift, scale = mod_ref[0, 0:1, :], mod_ref[0, 1:2, :]
    h = _modulate(x_ref[...], shift, scale).astype(BF16)
    lane = lax.broadcasted_iota(jnp.int32, (tm, QKV_CHUNK), 1)
    first_half = (lane % V_DIM) < HEAD_DIM
    if rope:
        reps = QKV_CHUNK // V_DIM
        cos = jnp.concatenate([cos_ref[...]] * reps, axis=1)
        sin = jnp.concatenate([sin_ref[...]] * reps, axis=1)
        swap_up = (lane % (2 * N_FREQ)) < N_FREQ

    def rotary(y):
        up = jnp.concatenate([pltpu.roll(y[:, c:c + LANES], LANES - N_FREQ, 1)
                              for c in range(0, QKV_CHUNK, LANES)], axis=1)
        down = jnp.concatenate([pltpu.roll(y[:, c:c + LANES], N_FREQ, 1)
                                for c in range(0, QKV_CHUNK, LANES)], axis=1)
        return y * cos + jnp.where(swap_up, up, down) * sin

    def proj(col):
        return jnp.dot(h, w_ref[:, col:col + QKV_CHUNK], preferred_element_type=F32)

    for c0 in range(0, D_MODEL, QKV_CHUNK):
        cols = slice(c0, c0 + QKV_CHUNK)
        q, k, v = proj(c0), proj(D_MODEL + c0), proj(2 * D_MODEL + c0)
        if rope:
            q, k = rotary(q), rotary(k)
        q = q * Q_SCALE
        q0_ref[:, cols] = jnp.where(first_half, q, 0.0).astype(BF16)
        q1_ref[:, cols] = jnp.where(first_half, 0.0, q).astype(BF16)
        k_ref[:, cols] = k.astype(BF16)
        v_ref[:, cols] = v.astype(BF16)
        if cache:
            kc_ref[:, 0, :, cols] = k.reshape(kc_ref.shape[0], kc_ref.shape[2], QKV_CHUNK)
            vc_ref[:, 0, :, cols] = v.reshape(vc_ref.shape[0], vc_ref.shape[2], QKV_CHUNK)


def _qkv(x, mods, layer, seq_len, mod_row, w_qkv, tm, rope_tables=None, cache=None, cache_slot=0):
    rows = x.shape[0]
    rope = rope_tables is not None
    with_cache = cache is not None
    tiles_per_seq = max(seq_len // tm, 1)
    in_specs = [
        pl.BlockSpec((tm, D_MODEL), lambda i: (i, 0)),
        pl.BlockSpec((1, 6, D_MODEL), lambda i: (layer * MOD_ROWS + mod_row(i), 0, 0)),
        _const_spec((D_MODEL, 3 * D_MODEL)),
    ]
    args = [x, mods, w_qkv]
    if rope:
        in_specs += [pl.BlockSpec((tm, V_DIM), lambda i: (i % tiles_per_seq, 0))] * 2
        args += list(rope_tables)
    act = jax.ShapeDtypeStruct((rows, D_MODEL), BF16)
    out_shape = [act] * 4
    out_specs = [pl.BlockSpec((tm, D_MODEL), lambda i: (i, 0))] * 4
    aliases = {}
    if with_cache:
        seqs = tm // seq_len
        ck, cv = cache
        in_specs += [pl.BlockSpec(memory_space=pl.ANY)] * 2
        aliases = {len(args): 4, len(args) + 1: 5}
        args += [ck, cv]
        out_shape += [jax.ShapeDtypeStruct(ck.shape, F32), jax.ShapeDtypeStruct(cv.shape, F32)]
        out_specs += [pl.BlockSpec((seqs, 1, seq_len, D_MODEL), lambda i: (i, cache_slot, 0, 0))] * 2
    kern = functools.partial(_qkv_kernel, tm=tm, rope=rope, cache=with_cache)
    return pl.pallas_call(
        kern,
        out_shape=out_shape,
        grid=(rows // tm,),
        in_specs=in_specs,
        out_specs=out_specs,
        input_output_aliases=aliases,
        compiler_params=_params(("arbitrary",)),
        name="qkv_proj",
    )(*args)


def _attn_kernel(*refs, hb, tq, lam_init, cache):
    lam_ref, g_ref, q0_ref, q1_ref, k_ref, v_ref = refs[:6]
    if cache:
        ck_ref, cv_ref = refs[6:8]
    o_ref = refs[-1]
    lv = lam_ref[...]
    lam = (jnp.exp(jnp.sum(lv[0:1] * lv[1:2], axis=-1, keepdims=True))
           - jnp.exp(jnp.sum(lv[2:3] * lv[3:4], axis=-1, keepdims=True)) + lam_init)
    nt = (((1,), (1,)), ((), ()))

    def scores(hh):
        cols = slice(hh * V_DIM, (hh + 1) * V_DIM)
        qq = jnp.concatenate([q0_ref[:, cols], q1_ref[:, cols]], axis=0)
        s = lax.dot_general(qq, k_ref[:, cols], nt, preferred_element_type=F32)
        sc = None
        if cache:
            sc = lax.dot_general(qq, ck_ref[:, cols].astype(BF16), nt, preferred_element_type=F32)
        return s, sc

    def finish(hh, s, sc):
        cols = slice(hh * V_DIM, (hh + 1) * V_DIM)
        m = jnp.max(s, axis=-1, keepdims=True)
        if cache:
            m = jnp.maximum(m, jnp.max(sc, axis=-1, keepdims=True))
        p = jnp.exp2(s - m)
        l = jnp.sum(p, axis=-1, keepdims=True)
        if cache:
            pc = jnp.exp2(sc - m)
            l = l + jnp.sum(pc, axis=-1, keepdims=True)
        ratio = l[:tq] * lam / l[tq:]
        w = (p[:tq] - p[tq:] * ratio).astype(BF16)
        o = jnp.dot(w, v_ref[:, cols], preferred_element_type=F32)
        if cache:
            wc = (pc[:tq] - pc[tq:] * ratio).astype(BF16)
            o = o + jnp.dot(wc, cv_ref[:, cols].astype(BF16), preferred_element_type=F32)
        o = o / l[:tq]
        o = o * lax.rsqrt(jnp.mean(o * o, axis=-1, keepdims=True) + SUBLN_EPS)
        o = o * g_ref[...] * (1.0 - lam_init)
        o_ref[:, cols] = o.astype(BF16)

    pending = scores(0)
    for hh in range(hb):
        following = scores(hh + 1) if hh + 1 < hb else None
        finish(hh, *pending)
        pending = following


def _attention(q0, q1, k, v, lam_vecs, subln_g, lam_init, seq_len, tq, hb, cache=None):
    rows = q0.shape[0]
    n_seq = rows // seq_len
    q_steps = seq_len // tq
    wb = hb * V_DIM
    with_cache = cache is not None
    q_spec = pl.BlockSpec((tq, wb), lambda s, h, j: (s * q_steps + j, h))
    kv_spec = pl.BlockSpec((seq_len, wb), lambda s, h, j: (s, h))
    in_specs = [_const_spec((4, HEAD_DIM)), _const_spec((1, V_DIM)), q_spec, q_spec, kv_spec, kv_spec]
    args = [lam_vecs, subln_g, q0, q1, k, v]
    if with_cache:
        ck, cv, slot = cache
        past = ck.shape[2]
        c_spec = pl.BlockSpec((None, None, past, wb), lambda s, h, j: (s, slot, 0, h))
        in_specs += [c_spec, c_spec]
        args += [ck, cv]
    kern = functools.partial(_attn_kernel, hb=hb, tq=tq, lam_init=lam_init, cache=with_cache)
    return pl.pallas_call(
        kern,
        out_shape=jax.ShapeDtypeStruct((rows, D_MODEL), BF16),
        grid=(n_seq, N_HEADS // hb, q_steps),
        in_specs=in_specs,
        out_specs=q_spec,
        compiler_params=_params(("arbitrary", "arbitrary", "arbitrary")),
        name="diff_attention",
    )(*args)


def _proj_kernel(x_ref, o_ref_in, mod_ref, w_ref, g_ref, b_ref, out_ref):
    gate = mod_ref[0, 2:3, :]
    m = jnp.dot(o_ref_in[...], w_ref[...], preferred_element_type=F32)
    out_ref[...] = _post_norm(x_ref[...], gate * m, g_ref[...], b_ref[...])


def _out_proj(x, o, mods, layer, mod_row, w_o, g, b, tm):
    rows = x.shape[0]
    row_spec = pl.BlockSpec((tm, D_MODEL), lambda i: (i, 0))
    return pl.pallas_call(
        _proj_kernel,
        out_shape=jax.ShapeDtypeStruct(x.shape, F32),
        grid=(rows // tm,),
        in_specs=[
            row_spec, row_spec,
            pl.BlockSpec((1, 6, D_MODEL), lambda i: (layer * MOD_ROWS + mod_row(i), 0, 0)),
            _const_spec((D_MODEL, D_MODEL)),
            _const_spec((1, D_MODEL)),
            _const_spec((1, D_MODEL)),
        ],
        out_specs=row_spec,
        compiler_params=_params(("arbitrary",)),
        name="attn_out_proj",
    )(x, o, mods, w_o, g, b)


def _rope_tables(n):
    rows = n // GRID_W
    row = (lax.broadcasted_iota(jnp.int32, (n, N_FREQ), 0) // GRID_W).astype(F32)
    col = (lax.broadcasted_iota(jnp.int32, (n, N_FREQ), 0) % GRID_W).astype(F32)
    del rows
    inv = 1.0 / (ROPE_THETA ** (jnp.arange(N_FREQ, dtype=F32) / N_FREQ))
    ar = row * inv
    ac = col * inv
    ang = jnp.concatenate([ar, ar, ac, ac], axis=-1)
    sign = jnp.tile(jnp.concatenate([-jnp.ones((N_FREQ,), F32), jnp.ones((N_FREQ,), F32)]), 2)
    cos = jnp.cos(ang)
    sin = jnp.sin(ang) * sign
    return jnp.tile(cos, (1, 2)), jnp.tile(sin, (1, 2))


def _tiles(ctx_len, lat_len):
    return dict(
        ffn_ctx=ctx_len, ffn_lat=512,
        row_ctx=512, row_lat=512,
        fourier_ctx=ctx_len, fourier_lat=256,
        attn_q_ctx=ctx_len, attn_q_lat=256,
        attn_heads_ctx=N_HEADS, attn_heads_lat=4,
    )


def kernel(x_prompt, x_sample, cache_k, cache_v, c, c_ctx, w_ada, b_ada, w_fourier, w_qkv,
           lambda_q1, lambda_k1, lambda_q2, lambda_k2, subln_g, w_o, w_up, conv_w, conv_b,
           w_down, ln1_g, ln1_b, ln2_g, ln2_b):
    n_ctx_seq, ctx_len, d = x_prompt.shape
    n_lat_seq, lat_len, _ = x_sample.shape
    past = cache_k.shape[2]
    assert d == D_MODEL and n_lat_seq + 1 <= MOD_ROWS
    t = _tiles(ctx_len, lat_len)

    cvec = jnp.concatenate(
        [c_ctx[None, :], c, jnp.zeros((MOD_ROWS - 1 - n_lat_seq, d), F32)], axis=0)
    mods = _ada_table(cvec, w_ada, b_ada)

    xc = x_prompt.reshape(n_ctx_seq * ctx_len, d)
    xl = x_sample.reshape(n_lat_seq * lat_len, d)
    ck_in = cache_k.reshape(n_lat_seq, DEPTH // 2, past, d)
    cv_in = cache_v.reshape(n_lat_seq, DEPTH // 2, past, d)

    ctx_row = lambda i: 0
    lat_row_of = lambda tile: (lambda i: 1 + i // (lat_len // tile))
    cs = _channel_dft_table()
    dft_ctx, dft_lat = _dft_table(ctx_len), _dft_table(lat_len)
    rope = _rope_tables(lat_len)
    new_k = jnp.zeros((n_ctx_seq, DEPTH // 2, ctx_len, d), F32)
    new_v = jnp.zeros((n_ctx_seq, DEPTH // 2, ctx_len, d), F32)

    for i in range(DEPTH):
        j = i // 2
        g1, b1 = ln1_g[i][None, :], ln1_b[i][None, :]
        g2, b2 = ln2_g[i][None, :], ln2_b[i][None, :]
        if i % 2 == 0:
            wf = w_fourier[j].astype(BF16)
            xc = _fourier(xc, mods, i, ctx_len, ctx_row, cs, dft_ctx, wf, g1, b1, tq=t["fourier_ctx"])
            xl = _fourier(xl, mods, i, lat_len, lambda s: 1 + s, cs, dft_lat, wf, g1, b1,
                          tq=t["fourier_lat"])
        else:
            lam_init = 0.8 - 0.6 * math.exp(-0.3 * i)
            lam_vecs = jnp.stack([lambda_q1[j], lambda_k1[j], lambda_q2[j], lambda_k2[j]])
            sg = subln_g[j][None, :]
            wqkv = w_qkv[j].astype(BF16)
            wo = w_o[j].astype(BF16)
            q0, q1, k, v, new_k, new_v = _qkv(
                xc, mods, i, ctx_len, ctx_row, wqkv, tm=t["row_ctx"], cache=(new_k, new_v), cache_slot=j)
            oc = _attention(q0, q1, k, v, lam_vecs, sg, lam_init, ctx_len,
                            tq=t["attn_q_ctx"], hb=t["attn_heads_ctx"])
            xc = _out_proj(xc, oc, mods, i, ctx_row, wo, g1, b1, tm=t["row_ctx"])
            lat_row = lat_row_of(t["row_lat"])
            q0, q1, k, v = _qkv(xl, mods, i, lat_len, lat_row, wqkv, tm=t["row_lat"], rope_tables=rope)
            ol = _attention(q0, q1, k, v, lam_vecs, sg, lam_init, lat_len,
                            tq=t["attn_q_lat"], hb=t["attn_heads_lat"], cache=(ck_in, cv_in, j))
            xl = _out_proj(xl, ol, mods, i, lat_row, wo, g1, b1, tm=t["row_lat"])
        wup, wdn = w_up[i].astype(BF16), w_down[i].astype(BF16)
        cw, cb = conv_w[i], conv_b[i][None, :]
        xc = _ffn(xc, mods, i, ctx_len, ctx_row, wup, cw, cb, wdn, g2, b2, tm=t["ffn_ctx"])
        xl = _ffn(xl, mods, i, lat_len, lat_row_of(t["ffn_lat"]), wup, cw, cb, wdn, g2, b2,
                  tm=t["ffn_lat"])

    y_prompt = xc.reshape(x_prompt.shape)
    y_sample = xl.reshape(x_sample.shape)
    new_cache_k = new_k.reshape(n_ctx_seq, DEPTH // 2, ctx_len, N_HEADS, 2, HEAD_DIM)
    new_cache_v = new_v.reshape(n_ctx_seq, DEPTH // 2, ctx_len, N_HEADS, V_DIM)
    return (y_prompt, y_sample, new_cache_k, new_cache_v)
```

```python
import functools
import math

import jax
import jax.numpy as jnp
from jax import lax
from jax.experimental import pallas as pl
from jax.experimental.pallas import tpu as pltpu

F32 = jnp.float32
BF16 = jnp.bfloat16

D_MODEL = 1024
DEPTH = 4
GRID_W = 64
N_HEADS = 8
HEAD_DIM = 64
V_DIM = 2 * HEAD_DIM
N_FOURIER_GROUPS = 4
FOURIER_GROUP = D_MODEL // N_FOURIER_GROUPS
D_FF = 2816
ROPE_THETA = 10000.0
N_FREQ = HEAD_DIM // 4
DN_ALPHA = (2 * DEPTH) ** 0.25
LN_EPS = 1e-6
SUBLN_EPS = 1e-5

LANES = 128
SUBLANES = 8
MOD_ROWS = 8
FF_CHUNK = 256
HALO = 16
QKV_CHUNK = 256
VMEM_LIMIT = 56 * 1024 * 1024
Q_SCALE = HEAD_DIM ** -0.5 * math.log2(math.e)


def _params(sem, vmem=VMEM_LIMIT):
    return pltpu.CompilerParams(dimension_semantics=sem, vmem_limit_bytes=vmem)


def _ln(x):
    mu = jnp.mean(x, axis=-1, keepdims=True)
    xc = x - mu
    var = jnp.mean(xc * xc, axis=-1, keepdims=True)
    return xc * lax.rsqrt(var + LN_EPS)


def _modulate(x, shift, scale):
    return _ln(x) * (1.0 + scale) + shift


def _post_norm(x, update, g, b):
    return _ln(DN_ALPHA * x + update) * g + b


def _silu(x):
    return x / (1.0 + jnp.exp(-x))


def _const_spec(shape):
    return pl.BlockSpec(shape, lambda *_: (0,) * len(shape))


def _layer_spec(shape, layer):
    return pl.BlockSpec((None,) + tuple(shape), lambda *_: (layer,) + (0,) * len(shape))


def _ada_kernel(c_ref, w_ref, b_ref, o_ref):
    s = _silu(c_ref[...]).astype(BF16)
    a = jnp.dot(s, w_ref[0].astype(BF16), preferred_element_type=F32)
    o_ref[0] = a + b_ref[0]


def _ada_table(cvec, w_ada, b_ada):
    nc = D_MODEL
    n_col = w_ada.shape[-1] // nc
    out = pl.pallas_call(
        _ada_kernel,
        out_shape=jax.ShapeDtypeStruct((DEPTH, MOD_ROWS, w_ada.shape[-1]), F32),
        grid=(DEPTH, n_col),
        in_specs=[
            pl.BlockSpec((MOD_ROWS, D_MODEL), lambda l, j: (0, 0)),
            pl.BlockSpec((1, D_MODEL, nc), lambda l, j: (l, 0, j)),
            pl.BlockSpec((1, 1, nc), lambda l, j: (l, 0, j)),
        ],
        out_specs=pl.BlockSpec((1, MOD_ROWS, nc), lambda l, j: (l, 0, j)),
        compiler_params=_params(("arbitrary", "arbitrary")),
        name="ada_table",
    )(cvec, w_ada, b_ada.reshape(DEPTH, 1, -1))
    return out.reshape(DEPTH * MOD_ROWS, 6, D_MODEL)


def _ffn_kernel(*refs, tm, seq_len):
    halo = seq_len > tm
    x_ref = refs[0]
    if halo:
        xp_ref, xn_ref = refs[1:3]
        refs = refs[2:]
    (mod_ref, wup_ref, cw_ref, cb_ref, wdn_ref, g_ref, b_ref,
     o_ref, slab_ref, hext_ref, act_ref) = refs[1:]
    seg = tm // SUBLANES
    n_slab = D_MODEL // LANES
    shift, scale, gate = mod_ref[0, 3:4, :], mod_ref[0, 4:5, :], mod_ref[0, 5:6, :]
    x = x_ref[...]
    h = _modulate(x, shift, scale)
    for cb in range(n_slab):
        for s in range(SUBLANES):
            slab_ref[cb, pl.ds(s, seg, stride=SUBLANES), :] = (
                h[s * seg:(s + 1) * seg, cb * LANES:(cb + 1) * LANES])
    for cb in range(n_slab):
        hext_ref[0:tm, cb * LANES:(cb + 1) * LANES] = slab_ref[cb].astype(BF16)
    row = lax.broadcasted_iota(jnp.int32, (SUBLANES, FF_CHUNK), 0)
    if halo:
        tiles_per_seq = seq_len // tm
        pos = pl.program_id(0) % tiles_per_seq
        hp = jnp.where(pos > 0, _modulate(xp_ref[...], shift, scale), 0.0)
        hn = jnp.where(pos < tiles_per_seq - 1, _modulate(xn_ref[...], shift, scale), 0.0)
        hext_ref[tm:, :] = jnp.concatenate([hp, hn], axis=0).astype(BF16)
    else:
        assert tm % seq_len == 0 and seq_len % seg == 0
        seq_start = functools.reduce(jnp.logical_or, [row == s for s in range(SUBLANES) if (s * seg) % seq_len == 0])
        seq_end = functools.reduce(jnp.logical_or, [row == s for s in range(SUBLANES) if ((s + 1) * seg) % seq_len == 0])
    hext = hext_ref[...]

    def conv(u, col):
        cw = cw_ref[:, col:col + FF_CHUNK]
        first, last = pltpu.roll(u[tm - 8:tm], 1, 0), pltpu.roll(u[0:8], 7, 0)
        if halo:
            first = jnp.where(row == 0, u[tm + 7:tm + 8], first)
            last = jnp.where(row == 7, u[tm + 8:tm + 9], last)
        else:
            first = jnp.where(seq_start, 0.0, first)
            last = jnp.where(seq_end, 0.0, last)
        prev = jnp.concatenate([first, u[0:tm - 8]], axis=0)
        nxt = jnp.concatenate([u[8:tm], last], axis=0)
        return prev * cw[0:1] + u[0:tm] * cw[1:2] + nxt * cw[2:3] + cb_ref[:, col:col + FF_CHUNK]

    for c in range(D_FF // FF_CHUNK):
        ca, cg = c * FF_CHUNK, D_FF + c * FF_CHUNK
        ua = jnp.dot(hext, wup_ref[:, ca:ca + FF_CHUNK], preferred_element_type=F32)
        ug = jnp.dot(hext, wup_ref[:, cg:cg + FF_CHUNK], preferred_element_type=F32)
        act_ref[:, ca:ca + FF_CHUNK] = (_silu(conv(ua, ca)) * conv(ug, cg)).astype(BF16)
    f = jnp.dot(act_ref[...], wdn_ref[...], preferred_element_type=F32)
    for cb in range(n_slab):
        slab_ref[cb] = f[:, cb * LANES:(cb + 1) * LANES]
    for cb in range(n_slab):
        for s in range(SUBLANES):
            o_ref[s * seg:(s + 1) * seg, cb * LANES:(cb + 1) * LANES] = (
                slab_ref[cb, pl.ds(s, seg, stride=SUBLANES), :])
    o_ref[...] = _post_norm(x, gate * o_ref[...], g_ref[...], b_ref[...])


def _ffn(x, mods, layer, seq_len, mod_row, w_up, conv_w, conv_b, w_down, g, b, tm):
    rows = x.shape[0]
    halo = seq_len > tm
    blocks8 = tm // SUBLANES
    kern = functools.partial(_ffn_kernel, tm=tm, seq_len=seq_len)
    in_specs = [pl.BlockSpec((tm, D_MODEL), lambda i: (i, 0))]
    args = [x]
    if halo:
        in_specs += [
            pl.BlockSpec((SUBLANES, D_MODEL), lambda i: (jnp.maximum(i * blocks8 - 1, 0), 0)),
            pl.BlockSpec((SUBLANES, D_MODEL),
                         lambda i: (jnp.minimum((i + 1) * blocks8, rows // SUBLANES - 1), 0)),
        ]
        args += [x, x]
    in_specs += [
        pl.BlockSpec((1, 6, D_MODEL), lambda i: (layer * MOD_ROWS + mod_row(i), 0, 0)),
        _layer_spec((D_MODEL, 2 * D_FF), layer),
        _layer_spec((3, 2 * D_FF), layer),
        _layer_spec((1, 2 * D_FF), layer),
        _layer_spec((D_FF, D_MODEL), layer),
        _layer_spec((1, D_MODEL), layer),
        _layer_spec((1, D_MODEL), layer),
    ]
    args += [mods, w_up, conv_w, conv_b, w_down, g, b]
    return pl.pallas_call(
        kern,
        out_shape=jax.ShapeDtypeStruct(x.shape, F32),
        grid=(rows // tm,),
        in_specs=in_specs,
        out_specs=pl.BlockSpec((tm, D_MODEL), lambda i: (i, 0)),
        scratch_shapes=[
            pltpu.VMEM((D_MODEL // LANES, tm, LANES), F32),
            pltpu.VMEM((tm + (HALO if halo else 0), D_MODEL), BF16),
            pltpu.VMEM((tm, D_FF), BF16),
        ],
        compiler_params=_params(("arbitrary",)),
        name="conv_ffn",
    )(*args)


def _fourier_kernel(x_ref, mod_ref, cs_ref, dft_ref, wf_ref, g_ref, b_ref, o_ref, y_ref,
                    *, n, tq, rows_per_step):
    j = pl.program_id(1)
    shift, scale, gate = mod_ref[0, 0:1, :], mod_ref[0, 1:2, :], mod_ref[0, 2:3, :]

    @pl.when(j == 0)
    def _():
        for r in range(n // rows_per_step):
            r0 = r * rows_per_step
            h = _modulate(x_ref[r0:r0 + rows_per_step, :], shift, scale).astype(BF16)
            for gi in range(N_FOURIER_GROUPS):
                c0 = gi * FOURIER_GROUP
                y = jnp.dot(h[:, c0:c0 + FOURIER_GROUP], cs_ref[...], preferred_element_type=F32)
                y_ref[r0:r0 + rows_per_step, c0:c0 + FOURIER_GROUP] = y[:, :FOURIER_GROUP].astype(BF16)
                y_ref[n + r0:n + r0 + rows_per_step, c0:c0 + FOURIER_GROUP] = y[:, FOURIER_GROUP:].astype(BF16)

    f = jnp.dot(dft_ref[...], y_ref[...], preferred_element_type=F32)
    m = jnp.dot(f.astype(BF16), wf_ref[...], preferred_element_type=F32)
    x = x_ref[pl.ds(pl.multiple_of(j * tq, tq), tq), :]
    o_ref[...] = _post_norm(x, gate * m, g_ref[...], b_ref[...])


def _fourier(x, mods, layer, seq_len, mod_row, cs, dft, w_f, g, b, tq):
    rows = x.shape[0]
    n_seq = rows // seq_len
    steps = seq_len // tq
    kern = functools.partial(_fourier_kernel, n=seq_len, tq=tq, rows_per_step=min(seq_len, 512))
    return pl.pallas_call(
        kern,
        out_shape=jax.ShapeDtypeStruct(x.shape, F32),
        grid=(n_seq, steps),
        in_specs=[
            pl.BlockSpec((seq_len, D_MODEL), lambda s, j: (s, 0)),
            pl.BlockSpec((1, 6, D_MODEL), lambda s, j: (layer * MOD_ROWS + mod_row(s), 0, 0)),
            _const_spec((FOURIER_GROUP, 2 * FOURIER_GROUP)),
            pl.BlockSpec((tq, 2 * seq_len), lambda s, j: (j, 0)),
            _layer_spec((D_MODEL, D_MODEL), layer // 2),
            _layer_spec((1, D_MODEL), layer),
            _layer_spec((1, D_MODEL), layer),
        ],
        out_specs=pl.BlockSpec((tq, D_MODEL), lambda s, j: (s * steps + j, 0)),
        scratch_shapes=[pltpu.VMEM((2 * seq_len, D_MODEL), BF16)],
        compiler_params=_params(("arbitrary", "arbitrary")),
        name="fourier_mix",
    )(x, mods, cs, dft, w_f, g, b)


def _cos_sin(k, t, n):
    ang = ((k * t) % n).astype(F32) * (2.0 * math.pi / n)
    return jnp.cos(ang), jnp.sin(ang)


def _dft_table(n):
    split = min(n, LANES)
    k = lax.broadcasted_iota(jnp.int32, (n, n // split), 0)
    t_hi = lax.broadcasted_iota(jnp.int32, (n, n // split), 1) * split
    ca, sa = _cos_sin(k, t_hi, n)
    k = lax.broadcasted_iota(jnp.int32, (n, split), 0)
    t_lo = lax.broadcasted_iota(jnp.int32, (n, split), 1)
    cb, sb = _cos_sin(k, t_lo, n)
    ca, sa, cb, sb = ca[:, :, None], sa[:, :, None], cb[:, None, :], sb[:, None, :]
    cos = (ca * cb - sa * sb).reshape(n, n)
    sin = (sa * cb + ca * sb).reshape(n, n)
    return (jnp.concatenate([cos, -sin], axis=1) * (n ** -0.5)).astype(BF16)


def _channel_dft_table():
    n = FOURIER_GROUP
    k = lax.broadcasted_iota(jnp.int32, (n, n), 0)
    t = lax.broadcasted_iota(jnp.int32, (n, n), 1)
    cos, sin = _cos_sin(k, t, n)
    return (jnp.concatenate([cos, sin], axis=1) * (n ** -0.5)).astype(BF16)


def _qkv_kernel(*refs, tm, rope, cache):
    x_ref, mod_ref, w_ref = refs[:3]
    refs = refs[3:]
    if rope:
        cos_ref, sin_ref = refs[:2]
        refs = refs[2:]
    if cache:
        refs = refs[2:]
    q0_ref, q1_ref, k_ref, v_ref = refs[:4]
    if cache:
        kc_ref, vc_ref = refs[4:6]
    shift, scale = mod_ref[0, 0:1, :], mod_ref[0, 1:2, :]
    h = _modulate(x_ref[...], shift, scale).astype(BF16)
    lane = lax.broadcasted_iota(jnp.int32, (tm, QKV_CHUNK), 1)
    first_half = (lane % V_DIM) < HEAD_DIM
    if rope:
        reps = QKV_CHUNK // V_DIM
        cos = jnp.concatenate([cos_ref[...]] * reps, axis=1)
        sin = jnp.concatenate([sin_ref[...]] * reps, axis=1)
        swap_up = (lane % (2 * N_FREQ)) < N_FREQ

    def rotary(y):
        up = jnp.concatenate([pltpu.roll(y[:, c:c + LANES], LANES - N_FREQ, 1)
                              for c in range(0, QKV_CHUNK, LANES)], axis=1)
        down = jnp.concatenate([pltpu.roll(y[:, c:c + LANES], N_FREQ, 1)
                                for c in range(0, QKV_CHUNK, LANES)], axis=1)
        return y * cos + jnp.where(swap_up, up, down) * sin

    def proj(col):
        return jnp.dot(h, w_ref[:, col:col + QKV_CHUNK], preferred_element_type=F32)

    for c0 in range(0, D_MODEL, QKV_CHUNK):
        cols = slice(c0, c0 + QKV_CHUNK)
        q, k, v = proj(c0), proj(D_MODEL + c0), proj(2 * D_MODEL + c0)
        if rope:
            q, k = rotary(q), rotary(k)
        q = q * Q_SCALE
        q0_ref[:, cols] = jnp.where(first_half, q, 0.0).astype(BF16)
        q1_ref[:, cols] = jnp.where(first_half, 0.0, q).astype(BF16)
        k_ref[:, cols] = k.astype(BF16)
        v_ref[:, cols] = v.astype(BF16)
        if cache:
            kc_ref[:, 0, :, cols] = k.reshape(kc_ref.shape[0], kc_ref.shape[2], QKV_CHUNK)
            vc_ref[:, 0, :, cols] = v.reshape(vc_ref.shape[0], vc_ref.shape[2], QKV_CHUNK)


def _qkv(x, mods, layer, seq_len, mod_row, w_qkv, tm, rope_tables=None, cache=None, cache_slot=0):
    rows = x.shape[0]
    rope = rope_tables is not None
    with_cache = cache is not None
    tiles_per_seq = max(seq_len // tm, 1)
    in_specs = [
        pl.BlockSpec((tm, D_MODEL), lambda i: (i, 0)),
        pl.BlockSpec((1, 6, D_MODEL), lambda i: (layer * MOD_ROWS + mod_row(i), 0, 0)),
        _layer_spec((D_MODEL, 3 * D_MODEL), layer // 2),
    ]
    args = [x, mods, w_qkv]
    if rope:
        in_specs += [pl.BlockSpec((tm, V_DIM), lambda i: (i % tiles_per_seq, 0))] * 2
        args += list(rope_tables)
    act = jax.ShapeDtypeStruct((rows, D_MODEL), BF16)
    out_shape = [act] * 4
    out_specs = [pl.BlockSpec((tm, D_MODEL), lambda i: (i, 0))] * 4
    aliases = {}
    if with_cache:
        seqs = tm // seq_len
        ck, cv = cache
        in_specs += [pl.BlockSpec(memory_space=pl.ANY)] * 2
        aliases = {len(args): 4, len(args) + 1: 5}
        args += [ck, cv]
        out_shape += [jax.ShapeDtypeStruct(ck.shape, F32), jax.ShapeDtypeStruct(cv.shape, F32)]
        out_specs += [pl.BlockSpec((seqs, 1, seq_len, D_MODEL), lambda i: (i, cache_slot, 0, 0))] * 2
    kern = functools.partial(_qkv_kernel, tm=tm, rope=rope, cache=with_cache)
    return pl.pallas_call(
        kern,
        out_shape=out_shape,
        grid=(rows // tm,),
        in_specs=in_specs,
        out_specs=out_specs,
        input_output_aliases=aliases,
        compiler_params=_params(("arbitrary",)),
        name="qkv_proj",
    )(*args)


def _attn_kernel(*refs, hb, tq, lam_init, cache):
    lam_ref, g_ref, q0_ref, q1_ref, k_ref, v_ref = refs[:6]
    if cache:
        ck_ref, cv_ref = refs[6:8]
    o_ref = refs[-1]
    lv = lam_ref[...]
    lam = (jnp.exp(jnp.sum(lv[0:1] * lv[1:2], axis=-1, keepdims=True))
           - jnp.exp(jnp.sum(lv[2:3] * lv[3:4], axis=-1, keepdims=True)) + lam_init)
    nt = (((1,), (1,)), ((), ()))

    def scores(hh):
        cols = slice(hh * V_DIM, (hh + 1) * V_DIM)
        qq = jnp.concatenate([q0_ref[:, cols], q1_ref[:, cols]], axis=0)
        s = lax.dot_general(qq, k_ref[:, cols], nt, preferred_element_type=F32)
        sc = None
        if cache:
            sc = lax.dot_general(qq, ck_ref[:, cols].astype(BF16), nt, preferred_element_type=F32)
        return s, sc

    def finish(hh, s, sc):
        cols = slice(hh * V_DIM, (hh + 1) * V_DIM)
        m = jnp.max(s, axis=-1, keepdims=True)
        if cache:
            m = jnp.maximum(m, jnp.max(sc, axis=-1, keepdims=True))
        p = jnp.exp2(s - m)
        l = jnp.sum(p, axis=-1, keepdims=True)
        if cache:
            pc = jnp.exp2(sc - m)
            l = l + jnp.sum(pc, axis=-1, keepdims=True)
        ratio = l[:tq] * lam / l[tq:]
        w = (p[:tq] - p[tq:] * ratio).astype(BF16)
        o = jnp.dot(w, v_ref[:, cols], preferred_element_type=F32)
        if cache:
            wc = (pc[:tq] - pc[tq:] * ratio).astype(BF16)
            o = o + jnp.dot(wc, cv_ref[:, cols].astype(BF16), preferred_element_type=F32)
        o = o / l[:tq]
        o = o * lax.rsqrt(jnp.mean(o * o, axis=-1, keepdims=True) + SUBLN_EPS)
        o = o * g_ref[...] * (1.0 - lam_init)
        o_ref[:, cols] = o.astype(BF16)

    pending = scores(0)
    for hh in range(hb):
        following = scores(hh + 1) if hh + 1 < hb else None
        finish(hh, *pending)
        pending = following


def _attention(q0, q1, k, v, lam_vecs, subln_g, lam_init, seq_len, tq, hb, cache=None):
    rows = q0.shape[0]
    n_seq = rows // seq_len
    q_steps = seq_len // tq
    wb = hb * V_DIM
    with_cache = cache is not None
    q_spec = pl.BlockSpec((tq, wb), lambda s, h, j: (s * q_steps + j, h))
    kv_spec = pl.BlockSpec((seq_len, wb), lambda s, h, j: (s, h))
    in_specs = [_const_spec((4, HEAD_DIM)), _const_spec((1, V_DIM)), q_spec, q_spec, kv_spec, kv_spec]
    args = [lam_vecs, subln_g, q0, q1, k, v]
    if with_cache:
        ck, cv, slot = cache
        past = ck.shape[2]
        c_spec = pl.BlockSpec((None, None, past, wb), lambda s, h, j: (s, slot, 0, h))
        in_specs += [c_spec, c_spec]
        args += [ck, cv]
    kern = functools.partial(_attn_kernel, hb=hb, tq=tq, lam_init=lam_init, cache=with_cache)
    return pl.pallas_call(
        kern,
        out_shape=jax.ShapeDtypeStruct((rows, D_MODEL), BF16),
        grid=(n_seq, N_HEADS // hb, q_steps),
        in_specs=in_specs,
        out_specs=q_spec,
        compiler_params=_params(("arbitrary", "arbitrary", "arbitrary")),
        name="diff_attention",
    )(*args)


def _proj_kernel(x_ref, o_ref_in, mod_ref, w_ref, g_ref, b_ref, out_ref):
    gate = mod_ref[0, 2:3, :]
    m = jnp.dot(o_ref_in[...], w_ref[...], preferred_element_type=F32)
    out_ref[...] = _post_norm(x_ref[...], gate * m, g_ref[...], b_ref[...])


def _out_proj(x, o, mods, layer, mod_row, w_o, g, b, tm):
    rows = x.shape[0]
    row_spec = pl.BlockSpec((tm, D_MODEL), lambda i: (i, 0))
    return pl.pallas_call(
        _proj_kernel,
        out_shape=jax.ShapeDtypeStruct(x.shape, F32),
        grid=(rows // tm,),
        in_specs=[
            row_spec, row_spec,
            pl.BlockSpec((1, 6, D_MODEL), lambda i: (layer * MOD_ROWS + mod_row(i), 0, 0)),
            _layer_spec((D_MODEL, D_MODEL), layer // 2),
            _layer_spec((1, D_MODEL), layer),
            _layer_spec((1, D_MODEL), layer),
        ],
        out_specs=row_spec,
        compiler_params=_params(("arbitrary",)),
        name="attn_out_proj",
    )(x, o, mods, w_o, g, b)


def _rope_tables(n):
    token = lax.broadcasted_iota(jnp.int32, (n, N_FREQ), 0)
    row = (token // GRID_W).astype(F32)
    col = (token % GRID_W).astype(F32)
    inv = 1.0 / (ROPE_THETA ** (jnp.arange(N_FREQ, dtype=F32) / N_FREQ))
    ar = row * inv
    ac = col * inv
    ang = jnp.concatenate([ar, ar, ac, ac], axis=-1)
    sign = jnp.tile(jnp.concatenate([-jnp.ones((N_FREQ,), F32), jnp.ones((N_FREQ,), F32)]), 2)
    cos = jnp.cos(ang)
    sin = jnp.sin(ang) * sign
    return jnp.tile(cos, (1, 2)), jnp.tile(sin, (1, 2))


def _tiles(ctx_len, lat_len):
    return dict(
        ffn_ctx=2 * ctx_len, ffn_lat=512,
        row_ctx=512, row_lat=512,
        fourier_ctx=ctx_len, fourier_lat=256,
        attn_q_ctx=ctx_len, attn_q_lat=256,
        attn_heads_ctx=N_HEADS, attn_heads_lat=4,
    )


def kernel(x_prompt, x_sample, cache_k, cache_v, c, c_ctx, w_ada, b_ada, w_fourier, w_qkv,
           lambda_q1, lambda_k1, lambda_q2, lambda_k2, subln_g, w_o, w_up, conv_w, conv_b,
           w_down, ln1_g, ln1_b, ln2_g, ln2_b):
    n_ctx_seq, ctx_len, d = x_prompt.shape
    n_lat_seq, lat_len, _ = x_sample.shape
    past = cache_k.shape[2]
    assert d == D_MODEL and n_lat_seq + 1 <= MOD_ROWS
    t = _tiles(ctx_len, lat_len)

    cvec = jnp.concatenate(
        [c_ctx[None, :], c, jnp.zeros((MOD_ROWS - 1 - n_lat_seq, d), F32)], axis=0)
    mods = _ada_table(cvec, w_ada, b_ada)

    xc = x_prompt.reshape(n_ctx_seq * ctx_len, d)
    xl = x_sample.reshape(n_lat_seq * lat_len, d)
    ck_in = cache_k.reshape(n_lat_seq, DEPTH // 2, past, d)
    cv_in = cache_v.reshape(n_lat_seq, DEPTH // 2, past, d)

    ctx_row = lambda i: 0
    lat_row_of = lambda tile: (lambda i: 1 + i // (lat_len // tile))
    cs = _channel_dft_table()
    dft_ctx, dft_lat = _dft_table(ctx_len), _dft_table(lat_len)
    rope = _rope_tables(lat_len)
    new_k = jnp.zeros((n_ctx_seq, DEPTH // 2, ctx_len, d), F32)
    new_v = jnp.zeros((n_ctx_seq, DEPTH // 2, ctx_len, d), F32)

    wf, wqkv, wo = w_fourier.astype(BF16), w_qkv.astype(BF16), w_o.astype(BF16)
    wup, wdn = w_up.astype(BF16), w_down.astype(BF16)
    cw, cb = conv_w, conv_b[:, None, :]
    g1, b1 = ln1_g[:, None, :], ln1_b[:, None, :]
    g2, b2 = ln2_g[:, None, :], ln2_b[:, None, :]

    for i in range(DEPTH):
        j = i // 2
        if i % 2 == 0:
            xc = _fourier(xc, mods, i, ctx_len, ctx_row, cs, dft_ctx, wf, g1, b1, tq=t["fourier_ctx"])
            xl = _fourier(xl, mods, i, lat_len, lambda s: 1 + s, cs, dft_lat, wf, g1, b1,
                          tq=t["fourier_lat"])
        else:
            lam_init = 0.8 - 0.6 * math.exp(-0.3 * i)
            lam_vecs = jnp.stack([lambda_q1[j], lambda_k1[j], lambda_q2[j], lambda_k2[j]])
            sg = subln_g[j][None, :]
            q0, q1, k, v, new_k, new_v = _qkv(
                xc, mods, i, ctx_len, ctx_row, wqkv, tm=t["row_ctx"], cache=(new_k, new_v), cache_slot=j)
            oc = _attention(q0, q1, k, v, lam_vecs, sg, lam_init, ctx_len,
                            tq=t["attn_q_ctx"], hb=t["attn_heads_ctx"])
            xc = _out_proj(xc, oc, mods, i, ctx_row, wo, g1, b1, tm=t["row_ctx"])
            lat_row = lat_row_of(t["row_lat"])
            q0, q1, k, v = _qkv(xl, mods, i, lat_len, lat_row, wqkv, tm=t["row_lat"], rope_tables=rope)
            ol = _attention(q0, q1, k, v, lam_vecs, sg, lam_init, lat_len,
                            tq=t["attn_q_lat"], hb=t["attn_heads_lat"], cache=(ck_in, cv_in, j))
            xl = _out_proj(xl, ol, mods, i, lat_row, wo, g1, b1, tm=t["row_lat"])
        xc = _ffn(xc, mods, i, ctx_len, ctx_row, wup, cw, cb, wdn, g2, b2, tm=t["ffn_ctx"])
        xl = _ffn(xl, mods, i, lat_len, lat_row_of(t["ffn_lat"]), wup, cw, cb, wdn, g2, b2,
                  tm=t["ffn_lat"])

    y_prompt = xc.reshape(x_prompt.shape)
    y_sample = xl.reshape(x_sample.shape)
    new_cache_k = new_k.reshape(n_ctx_seq, DEPTH // 2, ctx_len, N_HEADS, 2, HEAD_DIM)
    new_cache_v = new_v.reshape(n_ctx_seq, DEPTH // 2, ctx_len, N_HEADS, V_DIM)
    return (y_prompt, y_sample, new_cache_k, new_cache_v)
```

```python
import functools
import math

import jax
import jax.numpy as jnp
from jax import lax
from jax.experimental import pallas as pl
from jax.experimental.pallas import tpu as pltpu

F32 = jnp.float32
BF16 = jnp.bfloat16

D_MODEL = 1024
DEPTH = 4
GRID_W = 64
N_HEADS = 8
HEAD_DIM = 64
V_DIM = 2 * HEAD_DIM
N_FOURIER_GROUPS = 4
FOURIER_GROUP = D_MODEL // N_FOURIER_GROUPS
D_FF = 2816
ROPE_THETA = 10000.0
N_FREQ = HEAD_DIM // 4
DN_ALPHA = (2 * DEPTH) ** 0.25
LN_EPS = 1e-6
SUBLN_EPS = 1e-5

LANES = 128
SUBLANES = 8
MOD_ROWS = 8
FF_CHUNK = 256
HALO = 16
QKV_CHUNK = 256
VMEM_LIMIT = 56 * 1024 * 1024
Q_SCALE = HEAD_DIM ** -0.5 * math.log2(math.e)


def _params(sem, vmem=VMEM_LIMIT):
    return pltpu.CompilerParams(dimension_semantics=sem, vmem_limit_bytes=vmem)


def _ln(x):
    mu = jnp.mean(x, axis=-1, keepdims=True)
    xc = x - mu
    var = jnp.mean(xc * xc, axis=-1, keepdims=True)
    return xc * lax.rsqrt(var + LN_EPS)


def _modulate(x, shift, scale):
    return _ln(x) * (1.0 + scale) + shift


def _post_norm(x, update, g, b):
    return _ln(DN_ALPHA * x + update) * g + b


def _silu(x):
    return x / (1.0 + jnp.exp(-x))


def _const_spec(shape):
    return pl.BlockSpec(shape, lambda *_: (0,) * len(shape))


def _layer_spec(shape, layer):
    return pl.BlockSpec((None,) + tuple(shape), lambda *_: (layer,) + (0,) * len(shape))


def _ada_kernel(c_ref, w_ref, b_ref, o_ref):
    s = _silu(c_ref[...]).astype(BF16)
    a = jnp.dot(s, w_ref[0].astype(BF16), preferred_element_type=F32)
    o_ref[0] = a + b_ref[0]


def _ada_table(cvec, w_ada, b_ada):
    nc = D_MODEL
    n_col = w_ada.shape[-1] // nc
    out = pl.pallas_call(
        _ada_kernel,
        out_shape=jax.ShapeDtypeStruct((DEPTH, MOD_ROWS, w_ada.shape[-1]), F32),
        grid=(DEPTH, n_col),
        in_specs=[
            pl.BlockSpec((MOD_ROWS, D_MODEL), lambda l, j: (0, 0)),
            pl.BlockSpec((1, D_MODEL, nc), lambda l, j: (l, 0, j)),
            pl.BlockSpec((1, 1, nc), lambda l, j: (l, 0, j)),
        ],
        out_specs=pl.BlockSpec((1, MOD_ROWS, nc), lambda l, j: (l, 0, j)),
        compiler_params=_params(("arbitrary", "arbitrary")),
        name="ada_table",
    )(cvec, w_ada, b_ada.reshape(DEPTH, 1, -1))
    return out.reshape(DEPTH * MOD_ROWS, 6, D_MODEL)


def _ffn_kernel(*refs, tm, seq_len, attn_proj):
    halo = seq_len > tm
    x_ref = refs[0]
    if halo:
        xp_ref, xn_ref = refs[1:3]
        refs = refs[2:]
    if attn_proj:
        a_ref = refs[1]
        refs = refs[1:]
        if halo:
            ap_ref, an_ref = refs[1:3]
            refs = refs[2:]
        wo_ref, g1_ref, b1_ref = refs[1:4]
        refs = refs[3:]
    (mod_ref, wup_ref, cw_ref, cb_ref, wdn_ref, g_ref, b_ref,
     o_ref, slab_ref, hext_ref, act_ref) = refs[1:]
    seg = tm // SUBLANES
    n_slab = D_MODEL // LANES
    shift, scale, gate = mod_ref[0, 3:4, :], mod_ref[0, 4:5, :], mod_ref[0, 5:6, :]
    x = x_ref[...]
    if halo:
        xp, xn = xp_ref[...], xn_ref[...]
    if attn_proj:
        gate1 = mod_ref[0, 2:3, :]
        a = a_ref[...]
        if halo:
            a = jnp.concatenate([a, ap_ref[...], an_ref[...]], axis=0)
        m = jnp.dot(a, wo_ref[...], preferred_element_type=F32)
        x = _post_norm(x, gate1 * m[0:tm], g1_ref[...], b1_ref[...])
        if halo:
            xp = _post_norm(xp, gate1 * m[tm + 8:tm + 16], g1_ref[...], b1_ref[...])
            xn = _post_norm(xn, gate1 * m[tm + 16:tm + 24], g1_ref[...], b1_ref[...])
    h = _modulate(x, shift, scale)
    for cb in range(n_slab):
        for s in range(SUBLANES):
            slab_ref[cb, pl.ds(s, seg, stride=SUBLANES), :] = (
                h[s * seg:(s + 1) * seg, cb * LANES:(cb + 1) * LANES])
    for cb in range(n_slab):
        hext_ref[0:tm, cb * LANES:(cb + 1) * LANES] = slab_ref[cb].astype(BF16)
    row = lax.broadcasted_iota(jnp.int32, (SUBLANES, FF_CHUNK), 0)
    if halo:
        tiles_per_seq = seq_len // tm
        pos = pl.program_id(0) % tiles_per_seq
        hp = jnp.where(pos > 0, _modulate(xp, shift, scale), 0.0)
        hn = jnp.where(pos < tiles_per_seq - 1, _modulate(xn, shift, scale), 0.0)
        hext_ref[tm:, :] = jnp.concatenate([hp, hn], axis=0).astype(BF16)
    else:
        assert tm % seq_len == 0 and seq_len % seg == 0
        seq_start = functools.reduce(jnp.logical_or, [row == s for s in range(SUBLANES) if (s * seg) % seq_len == 0])
        seq_end = functools.reduce(jnp.logical_or, [row == s for s in range(SUBLANES) if ((s + 1) * seg) % seq_len == 0])
    hext = hext_ref[...]

    def conv(u, col):
        cw = cw_ref[:, col:col + FF_CHUNK]
        first, last = pltpu.roll(u[tm - 8:tm], 1, 0), pltpu.roll(u[0:8], 7, 0)
        if halo:
            first = jnp.where(row == 0, u[tm + 7:tm + 8], first)
            last = jnp.where(row == 7, u[tm + 8:tm + 9], last)
        else:
            first = jnp.where(seq_start, 0.0, first)
            last = jnp.where(seq_end, 0.0, last)
        prev = jnp.concatenate([first, u[0:tm - 8]], axis=0)
        nxt = jnp.concatenate([u[8:tm], last], axis=0)
        return prev * cw[0:1] + u[0:tm] * cw[1:2] + nxt * cw[2:3] + cb_ref[:, col:col + FF_CHUNK]

    for c in range(D_FF // FF_CHUNK):
        ca, cg = c * FF_CHUNK, D_FF + c * FF_CHUNK
        ua = jnp.dot(hext, wup_ref[:, ca:ca + FF_CHUNK], preferred_element_type=F32)
        ug = jnp.dot(hext, wup_ref[:, cg:cg + FF_CHUNK], preferred_element_type=F32)
        act_ref[:, ca:ca + FF_CHUNK] = (_silu(conv(ua, ca)) * conv(ug, cg)).astype(BF16)
    f = jnp.dot(act_ref[...], wdn_ref[...], preferred_element_type=F32)
    for cb in range(n_slab):
        slab_ref[cb] = f[:, cb * LANES:(cb + 1) * LANES]
    for cb in range(n_slab):
        for s in range(SUBLANES):
            o_ref[s * seg:(s + 1) * seg, cb * LANES:(cb + 1) * LANES] = (
                slab_ref[cb, pl.ds(s, seg, stride=SUBLANES), :])
    o_ref[...] = _post_norm(x, gate * o_ref[...], g_ref[...], b_ref[...])


def _ffn(x, mods, layer, seq_len, mod_row, w_up, conv_w, conv_b, w_down, g, b, tm, attn=None):
    rows = x.shape[0]
    halo = seq_len > tm
    kern = functools.partial(_ffn_kernel, tm=tm, seq_len=seq_len, attn_proj=attn is not None)

    def halo_specs(block_rows):
        per_tile = tm // block_rows
        return [
            pl.BlockSpec((block_rows, D_MODEL), lambda i: (jnp.maximum(i * per_tile - 1, 0), 0)),
            pl.BlockSpec((block_rows, D_MODEL),
                         lambda i: (jnp.minimum((i + 1) * per_tile, rows // block_rows - 1), 0)),
        ]

    in_specs = [pl.BlockSpec((tm, D_MODEL), lambda i: (i, 0))]
    args = [x]
    if halo:
        in_specs += halo_specs(SUBLANES)
        args += [x, x]
    if attn is not None:
        a, w_o, g1, b1 = attn
        in_specs += [pl.BlockSpec((tm, D_MODEL), lambda i: (i, 0))]
        args += [a]
        if halo:
            in_specs += halo_specs(HALO)
            args += [a, a]
        in_specs += [_layer_spec((D_MODEL, D_MODEL), layer // 2),
                     _layer_spec((1, D_MODEL), layer), _layer_spec((1, D_MODEL), layer)]
        args += [w_o, g1, b1]
    in_specs += [
        pl.BlockSpec((1, 6, D_MODEL), lambda i: (layer * MOD_ROWS + mod_row(i), 0, 0)),
        _layer_spec((D_MODEL, 2 * D_FF), layer),
        _layer_spec((3, 2 * D_FF), layer),
        _layer_spec((1, 2 * D_FF), layer),
        _layer_spec((D_FF, D_MODEL), layer),
        _layer_spec((1, D_MODEL), layer),
        _layer_spec((1, D_MODEL), layer),
    ]
    args += [mods, w_up, conv_w, conv_b, w_down, g, b]
    return pl.pallas_call(
        kern,
        out_shape=jax.ShapeDtypeStruct(x.shape, F32),
        grid=(rows // tm,),
        in_specs=in_specs,
        out_specs=pl.BlockSpec((tm, D_MODEL), lambda i: (i, 0)),
        scratch_shapes=[
            pltpu.VMEM((D_MODEL // LANES, tm, LANES), F32),
            pltpu.VMEM((tm + (HALO if halo else 0), D_MODEL), BF16),
            pltpu.VMEM((tm, D_FF), BF16),
        ],
        compiler_params=_params(("arbitrary",)),
        name="conv_ffn",
    )(*args)


def _fourier_kernel(x_ref, mod_ref, cs_ref, dft_ref, wf_ref, g_ref, b_ref, o_ref, y_ref,
                    *, n, tq, rows_per_step):
    j = pl.program_id(1)
    shift, scale, gate = mod_ref[0, 0:1, :], mod_ref[0, 1:2, :], mod_ref[0, 2:3, :]

    @pl.when(j == 0)
    def _():
        for r in range(n // rows_per_step):
            r0 = r * rows_per_step
            h = _modulate(x_ref[r0:r0 + rows_per_step, :], shift, scale).astype(BF16)
            for gi in range(N_FOURIER_GROUPS):
                c0 = gi * FOURIER_GROUP
                y = jnp.dot(h[:, c0:c0 + FOURIER_GROUP], cs_ref[...], preferred_element_type=F32)
                y_ref[r0:r0 + rows_per_step, c0:c0 + FOURIER_GROUP] = y[:, :FOURIER_GROUP].astype(BF16)
                y_ref[n + r0:n + r0 + rows_per_step, c0:c0 + FOURIER_GROUP] = y[:, FOURIER_GROUP:].astype(BF16)

    f = jnp.dot(dft_ref[...], y_ref[...], preferred_element_type=F32)
    m = jnp.dot(f.astype(BF16), wf_ref[...], preferred_element_type=F32)
    x = x_ref[pl.ds(pl.multiple_of(j * tq, tq), tq), :]
    o_ref[...] = _post_norm(x, gate * m, g_ref[...], b_ref[...])


def _fourier(x, mods, layer, seq_len, mod_row, cs, dft, w_f, g, b, tq):
    rows = x.shape[0]
    n_seq = rows // seq_len
    steps = seq_len // tq
    kern = functools.partial(_fourier_kernel, n=seq_len, tq=tq, rows_per_step=min(seq_len, 512))
    return pl.pallas_call(
        kern,
        out_shape=jax.ShapeDtypeStruct(x.shape, F32),
        grid=(n_seq, steps),
        in_specs=[
            pl.BlockSpec((seq_len, D_MODEL), lambda s, j: (s, 0)),
            pl.BlockSpec((1, 6, D_MODEL), lambda s, j: (layer * MOD_ROWS + mod_row(s), 0, 0)),
            _const_spec((FOURIER_GROUP, 2 * FOURIER_GROUP)),
            pl.BlockSpec((tq, 2 * seq_len), lambda s, j: (j, 0)),
            _layer_spec((D_MODEL, D_MODEL), layer // 2),
            _layer_spec((1, D_MODEL), layer),
            _layer_spec((1, D_MODEL), layer),
        ],
        out_specs=pl.BlockSpec((tq, D_MODEL), lambda s, j: (s * steps + j, 0)),
        scratch_shapes=[pltpu.VMEM((2 * seq_len, D_MODEL), BF16)],
        compiler_params=_params(("arbitrary", "arbitrary")),
        name="fourier_mix",
    )(x, mods, cs, dft, w_f, g, b)


def _cos_sin(k, t, n):
    ang = ((k * t) % n).astype(F32) * (2.0 * math.pi / n)
    return jnp.cos(ang), jnp.sin(ang)


def _dft_table(n):
    split = min(n, LANES)
    k = lax.broadcasted_iota(jnp.int32, (n, n // split), 0)
    t_hi = lax.broadcasted_iota(jnp.int32, (n, n // split), 1) * split
    ca, sa = _cos_sin(k, t_hi, n)
    k = lax.broadcasted_iota(jnp.int32, (n, split), 0)
    t_lo = lax.broadcasted_iota(jnp.int32, (n, split), 1)
    cb, sb = _cos_sin(k, t_lo, n)
    ca, sa, cb, sb = ca[:, :, None], sa[:, :, None], cb[:, None, :], sb[:, None, :]
    cos = (ca * cb - sa * sb).reshape(n, n)
    sin = (sa * cb + ca * sb).reshape(n, n)
    return (jnp.concatenate([cos, -sin], axis=1) * (n ** -0.5)).astype(BF16)


def _channel_dft_table():
    n = FOURIER_GROUP
    k = lax.broadcasted_iota(jnp.int32, (n, n), 0)
    t = lax.broadcasted_iota(jnp.int32, (n, n), 1)
    cos, sin = _cos_sin(k, t, n)
    return (jnp.concatenate([cos, sin], axis=1) * (n ** -0.5)).astype(BF16)


def _qkv_kernel(*refs, tm, rope, cache, n_prev):
    x_ref, mod_ref, w_ref = refs[:3]
    refs = refs[3:]
    if rope:
        cos_ref, sin_ref = refs[:2]
        refs = refs[2:]
    if n_prev:
        prev_k_ref, prev_v_ref = refs[:2]
        refs = refs[2:]
    q0_ref, q1_ref, k_ref, v_ref = refs[:4]
    if cache:
        kc_ref, vc_ref = refs[4:6]
    if n_prev:
        kc_ref[:, 0:n_prev] = prev_k_ref[...]
        vc_ref[:, 0:n_prev] = prev_v_ref[...]
    shift, scale = mod_ref[0, 0:1, :], mod_ref[0, 1:2, :]
    h = _modulate(x_ref[...], shift, scale).astype(BF16)
    lane = lax.broadcasted_iota(jnp.int32, (tm, QKV_CHUNK), 1)
    first_half = (lane % V_DIM) < HEAD_DIM
    if rope:
        reps = QKV_CHUNK // V_DIM
        cos = jnp.concatenate([cos_ref[...]] * reps, axis=1)
        sin = jnp.concatenate([sin_ref[...]] * reps, axis=1)
        swap_up = (lane % (2 * N_FREQ)) < N_FREQ

    def rotary(y):
        up = jnp.concatenate([pltpu.roll(y[:, c:c + LANES], LANES - N_FREQ, 1)
                              for c in range(0, QKV_CHUNK, LANES)], axis=1)
        down = jnp.concatenate([pltpu.roll(y[:, c:c + LANES], N_FREQ, 1)
                                for c in range(0, QKV_CHUNK, LANES)], axis=1)
        return y * cos + jnp.where(swap_up, up, down) * sin

    def proj(col):
        return jnp.dot(h, w_ref[:, col:col + QKV_CHUNK], preferred_element_type=F32)

    for c0 in range(0, D_MODEL, QKV_CHUNK):
        cols = slice(c0, c0 + QKV_CHUNK)
        q, k, v = proj(c0), proj(D_MODEL + c0), proj(2 * D_MODEL + c0)
        if rope:
            q, k = rotary(q), rotary(k)
        q = q * Q_SCALE
        q0_ref[:, cols] = jnp.where(first_half, q, 0.0).astype(BF16)
        q1_ref[:, cols] = jnp.where(first_half, 0.0, q).astype(BF16)
        k_ref[:, cols] = k.astype(BF16)
        v_ref[:, cols] = v.astype(BF16)
        if cache:
            kc_ref[:, n_prev, :, cols] = k.reshape(kc_ref.shape[0], kc_ref.shape[2], QKV_CHUNK)
            vc_ref[:, n_prev, :, cols] = v.reshape(vc_ref.shape[0], vc_ref.shape[2], QKV_CHUNK)


def _qkv(x, mods, layer, seq_len, mod_row, w_qkv, tm, rope_tables=None, cache=None):
    rows = x.shape[0]
    rope = rope_tables is not None
    with_cache = cache is not None
    n_prev = cache[0].shape[1] if cache else 0
    tiles_per_seq = max(seq_len // tm, 1)
    in_specs = [
        pl.BlockSpec((tm, D_MODEL), lambda i: (i, 0)),
        pl.BlockSpec((1, 6, D_MODEL), lambda i: (layer * MOD_ROWS + mod_row(i), 0, 0)),
        _layer_spec((D_MODEL, 3 * D_MODEL), layer // 2),
    ]
    args = [x, mods, w_qkv]
    if rope:
        in_specs += [pl.BlockSpec((tm, V_DIM), lambda i: (i % tiles_per_seq, 0))] * 2
        args += list(rope_tables)
    act = jax.ShapeDtypeStruct((rows, D_MODEL), BF16)
    out_shape = [act] * 4
    out_specs = [pl.BlockSpec((tm, D_MODEL), lambda i: (i, 0))] * 4
    if with_cache:
        seqs = tm // seq_len
        n_seq = rows // seq_len
        if n_prev:
            in_specs += [pl.BlockSpec((seqs, n_prev, seq_len, D_MODEL), lambda i: (i, 0, 0, 0))] * 2
            args += list(cache)
        grown = jax.ShapeDtypeStruct((n_seq, n_prev + 1, seq_len, D_MODEL), F32)
        out_shape += [grown, grown]
        out_specs += [pl.BlockSpec((seqs, n_prev + 1, seq_len, D_MODEL), lambda i: (i, 0, 0, 0))] * 2
    kern = functools.partial(_qkv_kernel, tm=tm, rope=rope, cache=with_cache, n_prev=n_prev)
    return pl.pallas_call(
        kern,
        out_shape=out_shape,
        grid=(rows // tm,),
        in_specs=in_specs,
        out_specs=out_specs,
        compiler_params=_params(("arbitrary",)),
        name="qkv_proj",
    )(*args)


def _attn_kernel(*refs, hb, tq, lam_init, cache):
    lam_ref, g_ref, q0_ref, q1_ref, k_ref, v_ref = refs[:6]
    if cache:
        ck_ref, cv_ref = refs[6:8]
    o_ref = refs[-1]
    lv = lam_ref[...]
    lam = (jnp.exp(jnp.sum(lv[0:1] * lv[1:2], axis=-1, keepdims=True))
           - jnp.exp(jnp.sum(lv[2:3] * lv[3:4], axis=-1, keepdims=True)) + lam_init)
    nt = (((1,), (1,)), ((), ()))

    def scores(hh):
        cols = slice(hh * V_DIM, (hh + 1) * V_DIM)
        qq = jnp.concatenate([q0_ref[:, cols], q1_ref[:, cols]], axis=0)
        s = lax.dot_general(qq, k_ref[:, cols], nt, preferred_element_type=F32)
        sc = None
        if cache:
            sc = lax.dot_general(qq, ck_ref[:, cols].astype(BF16), nt, preferred_element_type=F32)
        return s, sc

    def finish(hh, s, sc):
        cols = slice(hh * V_DIM, (hh + 1) * V_DIM)
        m = jnp.max(s, axis=-1, keepdims=True)
        if cache:
            m = jnp.maximum(m, jnp.max(sc, axis=-1, keepdims=True))
        p = jnp.exp2(s - m)
        l = jnp.sum(p, axis=-1, keepdims=True)
        if cache:
            pc = jnp.exp2(sc - m)
            l = l + jnp.sum(pc, axis=-1, keepdims=True)
        ratio = l[:tq] * lam / l[tq:]
        w = (p[:tq] - p[tq:] * ratio).astype(BF16)
        o = jnp.dot(w, v_ref[:, cols], preferred_element_type=F32)
        if cache:
            wc = (pc[:tq] - pc[tq:] * ratio).astype(BF16)
            o = o + jnp.dot(wc, cv_ref[:, cols].astype(BF16), preferred_element_type=F32)
        o = o / l[:tq]
        o = o * lax.rsqrt(jnp.mean(o * o, axis=-1, keepdims=True) + SUBLN_EPS)
        o = o * g_ref[...] * (1.0 - lam_init)
        o_ref[:, cols] = o.astype(BF16)

    pending = scores(0)
    for hh in range(hb):
        following = scores(hh + 1) if hh + 1 < hb else None
        finish(hh, *pending)
        pending = following


def _attention(q0, q1, k, v, lam_vecs, subln_g, lam_init, seq_len, tq, hb, cache=None):
    rows = q0.shape[0]
    n_seq = rows // seq_len
    q_steps = seq_len // tq
    wb = hb * V_DIM
    with_cache = cache is not None
    q_spec = pl.BlockSpec((tq, wb), lambda s, h, j: (s * q_steps + j, h))
    kv_spec = pl.BlockSpec((seq_len, wb), lambda s, h, j: (s, h))
    in_specs = [_const_spec((4, HEAD_DIM)), _const_spec((1, V_DIM)), q_spec, q_spec, kv_spec, kv_spec]
    args = [lam_vecs, subln_g, q0, q1, k, v]
    if with_cache:
        ck, cv, slot = cache
        past = ck.shape[2]
        c_spec = pl.BlockSpec((None, None, past, wb), lambda s, h, j: (s, slot, 0, h))
        in_specs += [c_spec, c_spec]
        args += [ck, cv]
    kern = functools.partial(_attn_kernel, hb=hb, tq=tq, lam_init=lam_init, cache=with_cache)
    return pl.pallas_call(
        kern,
        out_shape=jax.ShapeDtypeStruct((rows, D_MODEL), BF16),
        grid=(n_seq, N_HEADS // hb, q_steps),
        in_specs=in_specs,
        out_specs=q_spec,
        compiler_params=_params(("arbitrary", "arbitrary", "arbitrary")),
        name="diff_attention",
    )(*args)


def _rope_tables(n):
    token = lax.broadcasted_iota(jnp.int32, (n, N_FREQ), 0)
    row = (token // GRID_W).astype(F32)
    col = (token % GRID_W).astype(F32)
    inv = 1.0 / (ROPE_THETA ** (jnp.arange(N_FREQ, dtype=F32) / N_FREQ))
    ar = row * inv
    ac = col * inv
    ang = jnp.concatenate([ar, ar, ac, ac], axis=-1)
    sign = jnp.tile(jnp.concatenate([-jnp.ones((N_FREQ,), F32), jnp.ones((N_FREQ,), F32)]), 2)
    cos = jnp.cos(ang)
    sin = jnp.sin(ang) * sign
    return jnp.tile(cos, (1, 2)), jnp.tile(sin, (1, 2))


def _tiles(ctx_len, lat_len):
    return dict(
        ffn_ctx=2 * ctx_len, ffn_lat=512,
        row_ctx=512, row_lat=512,
        fourier_ctx=ctx_len, fourier_lat=256,
        attn_q_ctx=ctx_len, attn_q_lat=256,
        attn_heads_ctx=N_HEADS, attn_heads_lat=4,
    )


def kernel(x_prompt, x_sample, cache_k, cache_v, c, c_ctx, w_ada, b_ada, w_fourier, w_qkv,
           lambda_q1, lambda_k1, lambda_q2, lambda_k2, subln_g, w_o, w_up, conv_w, conv_b,
           w_down, ln1_g, ln1_b, ln2_g, ln2_b):
    n_ctx_seq, ctx_len, d = x_prompt.shape
    n_lat_seq, lat_len, _ = x_sample.shape
    past = cache_k.shape[2]
    assert d == D_MODEL and n_lat_seq + 1 <= MOD_ROWS
    t = _tiles(ctx_len, lat_len)

    cvec = jnp.concatenate(
        [c_ctx[None, :], c, jnp.zeros((MOD_ROWS - 1 - n_lat_seq, d), F32)], axis=0)
    mods = _ada_table(cvec, w_ada, b_ada)

    xc = x_prompt.reshape(n_ctx_seq * ctx_len, d)
    xl = x_sample.reshape(n_lat_seq * lat_len, d)
    ck_in = cache_k.reshape(n_lat_seq, DEPTH // 2, past, d)
    cv_in = cache_v.reshape(n_lat_seq, DEPTH // 2, past, d)

    ctx_row = lambda i: 0
    lat_row_of = lambda tile: (lambda i: 1 + i // (lat_len // tile))
    cs = _channel_dft_table()
    dft_ctx, dft_lat = _dft_table(ctx_len), _dft_table(lat_len)
    rope = _rope_tables(lat_len)
    caches = ()

    wf, wqkv, wo = w_fourier.astype(BF16), w_qkv.astype(BF16), w_o.astype(BF16)
    wup, wdn = w_up.astype(BF16), w_down.astype(BF16)
    cw, cb = conv_w, conv_b[:, None, :]
    g1, b1 = ln1_g[:, None, :], ln1_b[:, None, :]
    g2, b2 = ln2_g[:, None, :], ln2_b[:, None, :]

    for i in range(DEPTH):
        j = i // 2
        attn_c = attn_l = None
        if i % 2 == 0:
            xc = _fourier(xc, mods, i, ctx_len, ctx_row, cs, dft_ctx, wf, g1, b1, tq=t["fourier_ctx"])
            xl = _fourier(xl, mods, i, lat_len, lambda s: 1 + s, cs, dft_lat, wf, g1, b1,
                          tq=t["fourier_lat"])
        else:
            lam_init = 0.8 - 0.6 * math.exp(-0.3 * i)
            lam_vecs = jnp.stack([lambda_q1[j], lambda_k1[j], lambda_q2[j], lambda_k2[j]])
            sg = subln_g[j][None, :]
            q0, q1, k, v, new_k, new_v = _qkv(
                xc, mods, i, ctx_len, ctx_row, wqkv, tm=t["row_ctx"], cache=caches)
            caches = (new_k, new_v)
            oc = _attention(q0, q1, k, v, lam_vecs, sg, lam_init, ctx_len,
                            tq=t["attn_q_ctx"], hb=t["attn_heads_ctx"])
            lat_row = lat_row_of(t["row_lat"])
            q0, q1, k, v = _qkv(xl, mods, i, lat_len, lat_row, wqkv, tm=t["row_lat"], rope_tables=rope)
            ol = _attention(q0, q1, k, v, lam_vecs, sg, lam_init, lat_len,
                            tq=t["attn_q_lat"], hb=t["attn_heads_lat"], cache=(ck_in, cv_in, j))
            attn_c, attn_l = (oc, wo, g1, b1), (ol, wo, g1, b1)
        xc = _ffn(xc, mods, i, ctx_len, ctx_row, wup, cw, cb, wdn, g2, b2, tm=t["ffn_ctx"], attn=attn_c)
        xl = _ffn(xl, mods, i, lat_len, lat_row_of(t["ffn_lat"]), wup, cw, cb, wdn, g2, b2,
                  tm=t["ffn_lat"], attn=attn_l)

    y_prompt = xc.reshape(x_prompt.shape)
    y_sample = xl.reshape(x_sample.shape)
    new_cache_k = new_k.reshape(n_ctx_seq, DEPTH // 2, ctx_len, N_HEADS, 2, HEAD_DIM)
    new_cache_v = new_v.reshape(n_ctx_seq, DEPTH // 2, ctx_len, N_HEADS, V_DIM)
    return (y_prompt, y_sample, new_cache_k, new_cache_v)
```

```python
import functools
import math

import jax
import jax.numpy as jnp
from jax import lax
from jax.experimental import pallas as pl
from jax.experimental.pallas import tpu as pltpu

F32 = jnp.float32
BF16 = jnp.bfloat16

D_MODEL = 1024
DEPTH = 4
GRID_W = 64
N_HEADS = 8
HEAD_DIM = 64
V_DIM = 2 * HEAD_DIM
N_FOURIER_GROUPS = 4
FOURIER_GROUP = D_MODEL // N_FOURIER_GROUPS
D_FF = 2816
ROPE_THETA = 10000.0
N_FREQ = HEAD_DIM // 4
DN_ALPHA = (2 * DEPTH) ** 0.25
LN_EPS = 1e-6
SUBLN_EPS = 1e-5

LANES = 128
SUBLANES = 8
MOD_ROWS = 8
FF_CHUNK = 256
HALO = 16
QKV_CHUNK = 256
VMEM_LIMIT = 56 * 1024 * 1024
Q_SCALE = HEAD_DIM ** -0.5 * math.log2(math.e)


def _params(sem, vmem=VMEM_LIMIT):
    return pltpu.CompilerParams(dimension_semantics=sem, vmem_limit_bytes=vmem)


def _ln(x):
    mu = jnp.mean(x, axis=-1, keepdims=True)
    xc = x - mu
    var = jnp.mean(xc * xc, axis=-1, keepdims=True)
    return xc * lax.rsqrt(var + LN_EPS)


def _modulate(x, shift, scale):
    return _ln(x) * (1.0 + scale) + shift


def _post_norm(x, update, g, b):
    return _ln(DN_ALPHA * x + update) * g + b


def _silu(x):
    return x / (1.0 + jnp.exp(-x))


def _const_spec(shape):
    return pl.BlockSpec(shape, lambda *_: (0,) * len(shape))


def _layer_spec(shape, layer):
    return pl.BlockSpec((None,) + tuple(shape), lambda *_: (layer,) + (0,) * len(shape))


def _ada_kernel(c_ref, w_ref, b_ref, o_ref):
    s = _silu(c_ref[...]).astype(BF16)
    a = jnp.dot(s, w_ref[0].astype(BF16), preferred_element_type=F32)
    o_ref[0] = a + b_ref[0]


def _ada_table(cvec, w_ada, b_ada):
    nc = D_MODEL
    n_col = w_ada.shape[-1] // nc
    out = pl.pallas_call(
        _ada_kernel,
        out_shape=jax.ShapeDtypeStruct((DEPTH, MOD_ROWS, w_ada.shape[-1]), F32),
        grid=(DEPTH, n_col),
        in_specs=[
            pl.BlockSpec((MOD_ROWS, D_MODEL), lambda l, j: (0, 0)),
            pl.BlockSpec((1, D_MODEL, nc), lambda l, j: (l, 0, j)),
            pl.BlockSpec((1, 1, nc), lambda l, j: (l, 0, j)),
        ],
        out_specs=pl.BlockSpec((1, MOD_ROWS, nc), lambda l, j: (l, 0, j)),
        compiler_params=_params(("arbitrary", "arbitrary")),
        name="ada_table",
    )(cvec, w_ada, b_ada.reshape(DEPTH, 1, -1))
    return out.reshape(DEPTH * MOD_ROWS, 6, D_MODEL)


def _ffn_kernel(*refs, tm, seq_len, attn_proj):
    halo = seq_len > tm
    x_ref = refs[0]
    if halo:
        xp_ref, xn_ref = refs[1:3]
        refs = refs[2:]
    if attn_proj:
        a_ref = refs[1]
        refs = refs[1:]
        if halo:
            ap_ref, an_ref = refs[1:3]
            refs = refs[2:]
        wo_ref, g1_ref, b1_ref = refs[1:4]
        refs = refs[3:]
    (mod_ref, wup_ref, cw_ref, cb_ref, wdn_ref, g_ref, b_ref,
     o_ref, slab_ref, hext_ref, act_ref) = refs[1:]
    seg = tm // SUBLANES
    n_slab = D_MODEL // LANES
    shift, scale, gate = mod_ref[0, 3:4, :], mod_ref[0, 4:5, :], mod_ref[0, 5:6, :]
    x = x_ref[...]
    if halo:
        xp, xn = xp_ref[...], xn_ref[...]
    if attn_proj:
        gate1 = mod_ref[0, 2:3, :]
        a = a_ref[...]
        if halo:
            a = jnp.concatenate([a, ap_ref[...], an_ref[...]], axis=0)
        m = jnp.dot(a, wo_ref[...], preferred_element_type=F32)
        x = _post_norm(x, gate1 * m[0:tm], g1_ref[...], b1_ref[...])
        if halo:
            xp = _post_norm(xp, gate1 * m[tm + 8:tm + 16], g1_ref[...], b1_ref[...])
            xn = _post_norm(xn, gate1 * m[tm + 16:tm + 24], g1_ref[...], b1_ref[...])
    h = _modulate(x, shift, scale)
    for cb in range(n_slab):
        for s in range(SUBLANES):
            slab_ref[cb, pl.ds(s, seg, stride=SUBLANES), :] = (
                h[s * seg:(s + 1) * seg, cb * LANES:(cb + 1) * LANES])
    for cb in range(n_slab):
        hext_ref[0:tm, cb * LANES:(cb + 1) * LANES] = slab_ref[cb].astype(BF16)
    row = lax.broadcasted_iota(jnp.int32, (SUBLANES, FF_CHUNK), 0)
    if halo:
        tiles_per_seq = seq_len // tm
        pos = pl.program_id(0) % tiles_per_seq
        hp = jnp.where(pos > 0, _modulate(xp, shift, scale), 0.0)
        hn = jnp.where(pos < tiles_per_seq - 1, _modulate(xn, shift, scale), 0.0)
        hext_ref[tm:, :] = jnp.concatenate([hp, hn], axis=0).astype(BF16)
    else:
        assert tm % seq_len == 0 and seq_len % seg == 0
        seq_start = functools.reduce(jnp.logical_or, [row == s for s in range(SUBLANES) if (s * seg) % seq_len == 0])
        seq_end = functools.reduce(jnp.logical_or, [row == s for s in range(SUBLANES) if ((s + 1) * seg) % seq_len == 0])
    hext = hext_ref[...]

    def conv(u, col):
        cw = cw_ref[:, col:col + FF_CHUNK]
        first, last = pltpu.roll(u[tm - 8:tm], 1, 0), pltpu.roll(u[0:8], 7, 0)
        if halo:
            first = jnp.where(row == 0, u[tm + 7:tm + 8], first)
            last = jnp.where(row == 7, u[tm + 8:tm + 9], last)
        else:
            first = jnp.where(seq_start, 0.0, first)
            last = jnp.where(seq_end, 0.0, last)
        prev = jnp.concatenate([first, u[0:tm - 8]], axis=0)
        nxt = jnp.concatenate([u[8:tm], last], axis=0)
        return prev * cw[0:1] + u[0:tm] * cw[1:2] + nxt * cw[2:3] + cb_ref[:, col:col + FF_CHUNK]

    for c in range(D_FF // FF_CHUNK):
        ca, cg = c * FF_CHUNK, D_FF + c * FF_CHUNK
        ua = jnp.dot(hext, wup_ref[:, ca:ca + FF_CHUNK], preferred_element_type=F32)
        ug = jnp.dot(hext, wup_ref[:, cg:cg + FF_CHUNK], preferred_element_type=F32)
        act_ref[:, ca:ca + FF_CHUNK] = (_silu(conv(ua, ca)) * conv(ug, cg)).astype(BF16)
    f = jnp.dot(act_ref[...], wdn_ref[...], preferred_element_type=F32)
    for cb in range(n_slab):
        slab_ref[cb] = f[:, cb * LANES:(cb + 1) * LANES]
    for cb in range(n_slab):
        for s in range(SUBLANES):
            o_ref[s * seg:(s + 1) * seg, cb * LANES:(cb + 1) * LANES] = (
                slab_ref[cb, pl.ds(s, seg, stride=SUBLANES), :])
    o_ref[...] = _post_norm(x, gate * o_ref[...], g_ref[...], b_ref[...])


def _ffn(x, mods, layer, seq_len, mod_row, w_up, conv_w, conv_b, w_down, g, b, tm, attn=None):
    rows = x.shape[0]
    halo = seq_len > tm
    kern = functools.partial(_ffn_kernel, tm=tm, seq_len=seq_len, attn_proj=attn is not None)

    def halo_specs(block_rows):
        per_tile = tm // block_rows
        return [
            pl.BlockSpec((block_rows, D_MODEL), lambda i: (jnp.maximum(i * per_tile - 1, 0), 0)),
            pl.BlockSpec((block_rows, D_MODEL),
                         lambda i: (jnp.minimum((i + 1) * per_tile, rows // block_rows - 1), 0)),
        ]

    in_specs = [pl.BlockSpec((tm, D_MODEL), lambda i: (i, 0))]
    args = [x]
    if halo:
        in_specs += halo_specs(SUBLANES)
        args += [x, x]
    if attn is not None:
        a, w_o, g1, b1 = attn
        in_specs += [pl.BlockSpec((tm, D_MODEL), lambda i: (i, 0))]
        args += [a]
        if halo:
            in_specs += halo_specs(HALO)
            args += [a, a]
        in_specs += [_layer_spec((D_MODEL, D_MODEL), layer // 2),
                     _layer_spec((1, D_MODEL), layer), _layer_spec((1, D_MODEL), layer)]
        args += [w_o, g1, b1]
    in_specs += [
        pl.BlockSpec((1, 6, D_MODEL), lambda i: (layer * MOD_ROWS + mod_row(i), 0, 0)),
        _layer_spec((D_MODEL, 2 * D_FF), layer),
        _layer_spec((3, 2 * D_FF), layer),
        _layer_spec((1, 2 * D_FF), layer),
        _layer_spec((D_FF, D_MODEL), layer),
        _layer_spec((1, D_MODEL), layer),
        _layer_spec((1, D_MODEL), layer),
    ]
    args += [mods, w_up, conv_w, conv_b, w_down, g, b]
    return pl.pallas_call(
        kern,
        out_shape=jax.ShapeDtypeStruct(x.shape, F32),
        grid=(rows // tm,),
        in_specs=in_specs,
        out_specs=pl.BlockSpec((tm, D_MODEL), lambda i: (i, 0)),
        scratch_shapes=[
            pltpu.VMEM((D_MODEL // LANES, tm, LANES), F32),
            pltpu.VMEM((tm + (HALO if halo else 0), D_MODEL), BF16),
            pltpu.VMEM((tm, D_FF), BF16),
        ],
        compiler_params=_params(("arbitrary",)),
        name="conv_ffn",
    )(*args)


def _fourier_kernel(x_ref, mod_ref, cs_ref, dc_ref, ds_ref, wf_ref, g_ref, b_ref, o_ref, y_ref,
                    *, n, tq, rows_per_step):
    j = pl.program_id(1)
    shift, scale, gate = mod_ref[0, 0:1, :], mod_ref[0, 1:2, :], mod_ref[0, 2:3, :]

    @pl.when(j == 0)
    def _():
        for r in range(n // rows_per_step):
            r0 = r * rows_per_step
            h = _modulate(x_ref[r0:r0 + rows_per_step, :], shift, scale).astype(BF16)
            for gi in range(N_FOURIER_GROUPS):
                c0 = gi * FOURIER_GROUP
                y = jnp.dot(h[:, c0:c0 + FOURIER_GROUP], cs_ref[...], preferred_element_type=F32)
                y_ref[r0:r0 + rows_per_step, c0:c0 + FOURIER_GROUP] = y[:, :FOURIER_GROUP].astype(BF16)
                y_ref[n + r0:n + r0 + rows_per_step, c0:c0 + FOURIER_GROUP] = y[:, FOURIER_GROUP:].astype(BF16)

    f = (jnp.dot(dc_ref[...], y_ref[0:n, :], preferred_element_type=F32)
         + jnp.dot(ds_ref[...], y_ref[n:, :], preferred_element_type=F32))
    m = jnp.dot(f.astype(BF16), wf_ref[...], preferred_element_type=F32)
    x = x_ref[pl.ds(pl.multiple_of(j * tq, tq), tq), :]
    o_ref[...] = _post_norm(x, gate * m, g_ref[...], b_ref[...])


def _fourier(x, mods, layer, seq_len, mod_row, cs, dft, w_f, g, b, tq):
    rows = x.shape[0]
    n_seq = rows // seq_len
    steps = seq_len // tq
    kern = functools.partial(_fourier_kernel, n=seq_len, tq=tq, rows_per_step=min(seq_len, 512))
    return pl.pallas_call(
        kern,
        out_shape=jax.ShapeDtypeStruct(x.shape, F32),
        grid=(n_seq, steps),
        in_specs=[
            pl.BlockSpec((seq_len, D_MODEL), lambda s, j: (s, 0)),
            pl.BlockSpec((1, 6, D_MODEL), lambda s, j: (layer * MOD_ROWS + mod_row(s), 0, 0)),
            _const_spec((FOURIER_GROUP, 2 * FOURIER_GROUP)),
            pl.BlockSpec((tq, seq_len), lambda s, j: (j, 0)),
            pl.BlockSpec((tq, seq_len), lambda s, j: (j, 0)),
            _layer_spec((D_MODEL, D_MODEL), layer // 2),
            _layer_spec((1, D_MODEL), layer),
            _layer_spec((1, D_MODEL), layer),
        ],
        out_specs=pl.BlockSpec((tq, D_MODEL), lambda s, j: (s * steps + j, 0)),
        scratch_shapes=[pltpu.VMEM((2 * seq_len, D_MODEL), BF16)],
        compiler_params=_params(("arbitrary", "arbitrary")),
        name="fourier_mix",
    )(x, mods, cs, dft[0], dft[1], w_f, g, b)


def _cos_sin(k, t, n):
    ang = ((k * t) % n).astype(F32) * (2.0 * math.pi / n)
    return jnp.cos(ang), jnp.sin(ang)


def _dft_table(n):
    split = min(n, LANES)
    k = lax.broadcasted_iota(jnp.int32, (n, n // split), 0)
    t_hi = lax.broadcasted_iota(jnp.int32, (n, n // split), 1) * split
    ca, sa = _cos_sin(k, t_hi, n)
    k = lax.broadcasted_iota(jnp.int32, (n, split), 0)
    t_lo = lax.broadcasted_iota(jnp.int32, (n, split), 1)
    cb, sb = _cos_sin(k, t_lo, n)
    ca, sa, cb, sb = ca[:, :, None], sa[:, :, None], cb[:, None, :], sb[:, None, :]
    cos = (ca * cb - sa * sb).reshape(n, n)
    sin = (sa * cb + ca * sb).reshape(n, n)
    norm = n ** -0.5
    return (cos * norm).astype(BF16), (sin * -norm).astype(BF16)


def _channel_dft_table():
    n = FOURIER_GROUP
    k = lax.broadcasted_iota(jnp.int32, (n, n), 0)
    t = lax.broadcasted_iota(jnp.int32, (n, n), 1)
    cos, sin = _cos_sin(k, t, n)
    return (jnp.concatenate([cos, sin], axis=1) * (n ** -0.5)).astype(BF16)


def _qkv_kernel(*refs, tm, rope, cache, n_prev):
    x_ref, mod_ref, w_ref = refs[:3]
    refs = refs[3:]
    if rope:
        cos_ref, sin_ref = refs[:2]
        refs = refs[2:]
    if n_prev:
        prev_k_ref, prev_v_ref = refs[:2]
        refs = refs[2:]
    q0_ref, q1_ref = refs[:2]
    if cache:
        kc_ref, vc_ref = refs[2:4]
    else:
        k_ref, v_ref = refs[2:4]
    if n_prev:
        kc_ref[:, 0:n_prev] = prev_k_ref[...]
        vc_ref[:, 0:n_prev] = prev_v_ref[...]
    shift, scale = mod_ref[0, 0:1, :], mod_ref[0, 1:2, :]
    h = _modulate(x_ref[...], shift, scale).astype(BF16)
    lane = lax.broadcasted_iota(jnp.int32, (tm, QKV_CHUNK), 1)
    first_half = (lane % V_DIM) < HEAD_DIM
    if rope:
        reps = QKV_CHUNK // V_DIM
        cos = jnp.concatenate([cos_ref[...]] * reps, axis=1)
        sin = jnp.concatenate([sin_ref[...]] * reps, axis=1)
        swap_up = (lane % (2 * N_FREQ)) < N_FREQ

    def rotary(y):
        up = jnp.concatenate([pltpu.roll(y[:, c:c + LANES], LANES - N_FREQ, 1)
                              for c in range(0, QKV_CHUNK, LANES)], axis=1)
        down = jnp.concatenate([pltpu.roll(y[:, c:c + LANES], N_FREQ, 1)
                                for c in range(0, QKV_CHUNK, LANES)], axis=1)
        return y * cos + jnp.where(swap_up, up, down) * sin

    def proj(col):
        return jnp.dot(h, w_ref[:, col:col + QKV_CHUNK], preferred_element_type=F32)

    for c0 in range(0, D_MODEL, QKV_CHUNK):
        cols = slice(c0, c0 + QKV_CHUNK)
        q, k, v = proj(c0), proj(D_MODEL + c0), proj(2 * D_MODEL + c0)
        if rope:
            q, k = rotary(q), rotary(k)
        q = q * Q_SCALE
        q0_ref[:, cols] = jnp.where(first_half, q, 0.0).astype(BF16)
        q1_ref[:, cols] = jnp.where(first_half, 0.0, q).astype(BF16)
        if cache:
            kc_ref[:, n_prev, :, cols] = k.reshape(kc_ref.shape[0], kc_ref.shape[2], QKV_CHUNK)
            vc_ref[:, n_prev, :, cols] = v.reshape(vc_ref.shape[0], vc_ref.shape[2], QKV_CHUNK)
        else:
            k_ref[:, cols] = k.astype(BF16)
            v_ref[:, cols] = v.astype(BF16)


def _qkv(x, mods, layer, seq_len, mod_row, w_qkv, tm, rope_tables=None, cache=None):
    rows = x.shape[0]
    rope = rope_tables is not None
    with_cache = cache is not None
    n_prev = cache[0].shape[1] if cache else 0
    tiles_per_seq = max(seq_len // tm, 1)
    in_specs = [
        pl.BlockSpec((tm, D_MODEL), lambda i: (i, 0)),
        pl.BlockSpec((1, 6, D_MODEL), lambda i: (layer * MOD_ROWS + mod_row(i), 0, 0)),
        _layer_spec((D_MODEL, 3 * D_MODEL), layer // 2),
    ]
    args = [x, mods, w_qkv]
    if rope:
        in_specs += [pl.BlockSpec((tm, V_DIM), lambda i: (i % tiles_per_seq, 0))] * 2
        args += list(rope_tables)
    act = jax.ShapeDtypeStruct((rows, D_MODEL), BF16)
    n_act = 2 if with_cache else 4
    out_shape = [act] * n_act
    out_specs = [pl.BlockSpec((tm, D_MODEL), lambda i: (i, 0))] * n_act
    if with_cache:
        seqs = tm // seq_len
        n_seq = rows // seq_len
        if n_prev:
            in_specs += [pl.BlockSpec((seqs, n_prev, seq_len, D_MODEL), lambda i: (i, 0, 0, 0))] * 2
            args += list(cache)
        grown = jax.ShapeDtypeStruct((n_seq, n_prev + 1, seq_len, D_MODEL), F32)
        out_shape += [grown, grown]
        out_specs += [pl.BlockSpec((seqs, n_prev + 1, seq_len, D_MODEL), lambda i: (i, 0, 0, 0))] * 2
    kern = functools.partial(_qkv_kernel, tm=tm, rope=rope, cache=with_cache, n_prev=n_prev)
    return pl.pallas_call(
        kern,
        out_shape=out_shape,
        grid=(rows // tm,),
        in_specs=in_specs,
        out_specs=out_specs,
        compiler_params=_params(("arbitrary",)),
        name="qkv_proj",
    )(*args)


def _attn_kernel(*refs, hb, tq, lam_init, cache):
    lam_ref, g_ref, q0_ref, q1_ref, k_ref, v_ref = refs[:6]
    if cache:
        ck_ref, cv_ref = refs[6:8]
    o_ref = refs[-1]
    lv = lam_ref[...]
    lam = (jnp.exp(jnp.sum(lv[0:1] * lv[1:2], axis=-1, keepdims=True))
           - jnp.exp(jnp.sum(lv[2:3] * lv[3:4], axis=-1, keepdims=True)) + lam_init)
    nt = (((1,), (1,)), ((), ()))

    def scores(hh):
        cols = slice(hh * V_DIM, (hh + 1) * V_DIM)
        qq = jnp.concatenate([q0_ref[:, cols], q1_ref[:, cols]], axis=0)
        s = lax.dot_general(qq, k_ref[:, cols].astype(BF16), nt, preferred_element_type=F32)
        sc = None
        if cache:
            sc = lax.dot_general(qq, ck_ref[:, cols].astype(BF16), nt, preferred_element_type=F32)
        return s, sc

    def finish(hh, s, sc):
        cols = slice(hh * V_DIM, (hh + 1) * V_DIM)
        m = jnp.max(s, axis=-1, keepdims=True)
        if cache:
            m = jnp.maximum(m, jnp.max(sc, axis=-1, keepdims=True))
        p = jnp.exp2(s - m)
        l = jnp.sum(p, axis=-1, keepdims=True)
        if cache:
            pc = jnp.exp2(sc - m)
            l = l + jnp.sum(pc, axis=-1, keepdims=True)
        ratio = l[:tq] * lam / l[tq:]
        w = (p[:tq] - p[tq:] * ratio).astype(BF16)
        o = jnp.dot(w, v_ref[:, cols].astype(BF16), preferred_element_type=F32)
        if cache:
            wc = (pc[:tq] - pc[tq:] * ratio).astype(BF16)
            o = o + jnp.dot(wc, cv_ref[:, cols].astype(BF16), preferred_element_type=F32)
        o = o / l[:tq]
        o = o * lax.rsqrt(jnp.mean(o * o, axis=-1, keepdims=True) + SUBLN_EPS)
        o = o * g_ref[...] * (1.0 - lam_init)
        o_ref[:, cols] = o.astype(BF16)

    pending = scores(0)
    for hh in range(hb):
        following = scores(hh + 1) if hh + 1 < hb else None
        finish(hh, *pending)
        pending = following


def _attention(q0, q1, k, v, lam_vecs, subln_g, lam_init, seq_len, tq, hb, cache=None, kv_slot=None):
    rows = q0.shape[0]
    n_seq = rows // seq_len
    q_steps = seq_len // tq
    wb = hb * V_DIM
    with_cache = cache is not None
    q_spec = pl.BlockSpec((tq, wb), lambda s, h, j: (s * q_steps + j, h))
    if kv_slot is None:
        kv_spec = pl.BlockSpec((seq_len, wb), lambda s, h, j: (s, h))
    else:
        kv_spec = pl.BlockSpec((None, None, seq_len, wb), lambda s, h, j: (s, kv_slot, 0, h))
    in_specs = [_const_spec((4, HEAD_DIM)), _const_spec((1, V_DIM)), q_spec, q_spec, kv_spec, kv_spec]
    args = [lam_vecs, subln_g, q0, q1, k, v]
    if with_cache:
        ck, cv, slot = cache
        past = ck.shape[2]
        c_spec = pl.BlockSpec((None, None, past, wb), lambda s, h, j: (s, slot, 0, h))
        in_specs += [c_spec, c_spec]
        args += [ck, cv]
    kern = functools.partial(_attn_kernel, hb=hb, tq=tq, lam_init=lam_init, cache=with_cache)
    return pl.pallas_call(
        kern,
        out_shape=jax.ShapeDtypeStruct((rows, D_MODEL), BF16),
        grid=(n_seq, N_HEADS // hb, q_steps),
        in_specs=in_specs,
        out_specs=q_spec,
        compiler_params=_params(("arbitrary", "arbitrary", "arbitrary")),
        name="diff_attention",
    )(*args)


def _rope_tables(n):
    token = lax.broadcasted_iota(jnp.int32, (n, N_FREQ), 0)
    row = (token // GRID_W).astype(F32)
    col = (token % GRID_W).astype(F32)
    inv = 1.0 / (ROPE_THETA ** (jnp.arange(N_FREQ, dtype=F32) / N_FREQ))
    ar = row * inv
    ac = col * inv
    ang = jnp.concatenate([ar, ar, ac, ac], axis=-1)
    sign = jnp.tile(jnp.concatenate([-jnp.ones((N_FREQ,), F32), jnp.ones((N_FREQ,), F32)]), 2)
    cos = jnp.cos(ang)
    sin = jnp.sin(ang) * sign
    return jnp.tile(cos, (1, 2)), jnp.tile(sin, (1, 2))


def _tiles(ctx_len, lat_len):
    return dict(
        ffn_ctx=2 * ctx_len, ffn_lat=512,
        row_ctx=512, row_lat=512,
        fourier_ctx=ctx_len, fourier_lat=512,
        attn_q_ctx=ctx_len, attn_q_lat=256,
        attn_heads_ctx=N_HEADS, attn_heads_lat=N_HEADS,
    )


def kernel(x_prompt, x_sample, cache_k, cache_v, c, c_ctx, w_ada, b_ada, w_fourier, w_qkv,
           lambda_q1, lambda_k1, lambda_q2, lambda_k2, subln_g, w_o, w_up, conv_w, conv_b,
           w_down, ln1_g, ln1_b, ln2_g, ln2_b):
    n_ctx_seq, ctx_len, d = x_prompt.shape
    n_lat_seq, lat_len, _ = x_sample.shape
    past = cache_k.shape[2]
    assert d == D_MODEL and n_lat_seq + 1 <= MOD_ROWS
    t = _tiles(ctx_len, lat_len)

    cvec = jnp.concatenate(
        [c_ctx[None, :], c, jnp.zeros((MOD_ROWS - 1 - n_lat_seq, d), F32)], axis=0)
    mods = _ada_table(cvec, w_ada, b_ada)

    xc = x_prompt.reshape(n_ctx_seq * ctx_len, d)
    xl = x_sample.reshape(n_lat_seq * lat_len, d)
    ck_in = cache_k.reshape(n_lat_seq, DEPTH // 2, past, d)
    cv_in = cache_v.reshape(n_lat_seq, DEPTH // 2, past, d)

    ctx_row = lambda i: 0
    lat_row_of = lambda tile: (lambda i: 1 + i // (lat_len // tile))
    cs = _channel_dft_table()
    dft_ctx, dft_lat = _dft_table(ctx_len), _dft_table(lat_len)
    rope = _rope_tables(lat_len)
    caches = ()

    wf, wqkv, wo = w_fourier.astype(BF16), w_qkv.astype(BF16), w_o.astype(BF16)
    wup, wdn = w_up.astype(BF16), w_down.astype(BF16)
    cw, cb = conv_w, conv_b[:, None, :]
    g1, b1 = ln1_g[:, None, :], ln1_b[:, None, :]
    g2, b2 = ln2_g[:, None, :], ln2_b[:, None, :]

    for i in range(DEPTH):
        j = i // 2
        attn_c = attn_l = None
        if i % 2 == 0:
            xc = _fourier(xc, mods, i, ctx_len, ctx_row, cs, dft_ctx, wf, g1, b1, tq=t["fourier_ctx"])
            xl = _fourier(xl, mods, i, lat_len, lambda s: 1 + s, cs, dft_lat, wf, g1, b1,
                          tq=t["fourier_lat"])
        else:
            lam_init = 0.8 - 0.6 * math.exp(-0.3 * i)
            lam_vecs = jnp.stack([lambda_q1[j], lambda_k1[j], lambda_q2[j], lambda_k2[j]])
            sg = subln_g[j][None, :]
            q0, q1, new_k, new_v = _qkv(
                xc, mods, i, ctx_len, ctx_row, wqkv, tm=t["row_ctx"], cache=caches)
            caches = (new_k, new_v)
            oc = _attention(q0, q1, new_k, new_v, lam_vecs, sg, lam_init, ctx_len,
                            tq=t["attn_q_ctx"], hb=t["attn_heads_ctx"], kv_slot=j)
            lat_row = lat_row_of(t["row_lat"])
            q0, q1, k, v = _qkv(xl, mods, i, lat_len, lat_row, wqkv, tm=t["row_lat"], rope_tables=rope)
            ol = _attention(q0, q1, k, v, lam_vecs, sg, lam_init, lat_len,
                            tq=t["attn_q_lat"], hb=t["attn_heads_lat"], cache=(ck_in, cv_in, j))
            attn_c, attn_l = (oc, wo, g1, b1), (ol, wo, g1, b1)
        xc = _ffn(xc, mods, i, ctx_len, ctx_row, wup, cw, cb, wdn, g2, b2, tm=t["ffn_ctx"], attn=attn_c)
        xl = _ffn(xl, mods, i, lat_len, lat_row_of(t["ffn_lat"]), wup, cw, cb, wdn, g2, b2,
                  tm=t["ffn_lat"], attn=attn_l)

    y_prompt = xc.reshape(x_prompt.shape)
    y_sample = xl.reshape(x_sample.shape)
    new_cache_k = new_k.reshape(n_ctx_seq, DEPTH // 2, ctx_len, N_HEADS, 2, HEAD_DIM)
    new_cache_v = new_v.reshape(n_ctx_seq, DEPTH // 2, ctx_len, N_HEADS, V_DIM)
    return (y_prompt, y_sample, new_cache_k, new_cache_v)
```

```python
import functools
import math

import jax
import jax.numpy as jnp
from jax import lax
from jax.experimental import pallas as pl
from jax.experimental.pallas import tpu as pltpu

F32 = jnp.float32
BF16 = jnp.bfloat16

D_MODEL = 1024
DEPTH = 4
GRID_W = 64
N_HEADS = 8
HEAD_DIM = 64
V_DIM = 2 * HEAD_DIM
N_FOURIER_GROUPS = 4
FOURIER_GROUP = D_MODEL // N_FOURIER_GROUPS
D_FF = 2816
ROPE_THETA = 10000.0
N_FREQ = HEAD_DIM // 4
DN_ALPHA = (2 * DEPTH) ** 0.25
LN_EPS = 1e-6
SUBLN_EPS = 1e-5

LANES = 128
SUBLANES = 8
MOD_ROWS = 8
FF_CHUNK = 256
HALO = 16
QKV_CHUNK = 256
VMEM_LIMIT = 56 * 1024 * 1024
Q_SCALE = HEAD_DIM ** -0.5 * math.log2(math.e)


def _params(sem, vmem=VMEM_LIMIT):
    return pltpu.CompilerParams(dimension_semantics=sem, vmem_limit_bytes=vmem)


def _ln(x):
    mu = jnp.mean(x, axis=-1, keepdims=True)
    xc = x - mu
    var = jnp.mean(xc * xc, axis=-1, keepdims=True)
    return xc * lax.rsqrt(var + LN_EPS)


def _modulate(x, shift, scale):
    return _ln(x) * (1.0 + scale) + shift


def _post_norm(x, update, g, b):
    return _ln(DN_ALPHA * x + update) * g + b


def _silu(x):
    return x / (1.0 + jnp.exp(-x))


def _const_spec(shape):
    return pl.BlockSpec(shape, lambda *_: (0,) * len(shape))


def _layer_spec(shape, layer):
    return pl.BlockSpec((None,) + tuple(shape), lambda *_: (layer,) + (0,) * len(shape))


def _ada_kernel(c_ref, w_ref, b_ref, o_ref):
    s = _silu(c_ref[...]).astype(BF16)
    a = jnp.dot(s, w_ref[0].astype(BF16), preferred_element_type=F32)
    o_ref[0] = a + b_ref[0]


def _ada_table(cvec, w_ada, b_ada):
    nc = D_MODEL
    n_col = w_ada.shape[-1] // nc
    out = pl.pallas_call(
        _ada_kernel,
        out_shape=jax.ShapeDtypeStruct((DEPTH, MOD_ROWS, w_ada.shape[-1]), F32),
        grid=(DEPTH, n_col),
        in_specs=[
            pl.BlockSpec((MOD_ROWS, D_MODEL), lambda l, j: (0, 0)),
            pl.BlockSpec((1, D_MODEL, nc), lambda l, j: (l, 0, j)),
            pl.BlockSpec((1, 1, nc), lambda l, j: (l, 0, j)),
        ],
        out_specs=pl.BlockSpec((1, MOD_ROWS, nc), lambda l, j: (l, 0, j)),
        compiler_params=_params(("arbitrary", "arbitrary")),
        name="ada_table",
    )(cvec, w_ada, b_ada.reshape(DEPTH, 1, -1))
    return out.reshape(DEPTH * MOD_ROWS, 6, D_MODEL)


def _ffn_kernel(*refs, tm, seq_len, attn_proj):
    halo = seq_len > tm
    x_ref = refs[0]
    if halo:
        xp_ref, xn_ref = refs[1:3]
        refs = refs[2:]
    if attn_proj:
        a_ref = refs[1]
        refs = refs[1:]
        if halo:
            ap_ref, an_ref = refs[1:3]
            refs = refs[2:]
        wo_ref, g1_ref, b1_ref = refs[1:4]
        refs = refs[3:]
    (mod_ref, wup_ref, cw_ref, cb_ref, wdn_ref, g_ref, b_ref,
     o_ref, slab_ref, hext_ref, act_ref) = refs[1:]
    seg = tm // SUBLANES
    n_slab = D_MODEL // LANES
    shift, scale, gate = mod_ref[0, 3:4, :], mod_ref[0, 4:5, :], mod_ref[0, 5:6, :]
    x = x_ref[...]
    if halo:
        xp, xn = xp_ref[...], xn_ref[...]
    if attn_proj:
        gate1 = mod_ref[0, 2:3, :]
        a = a_ref[...]
        if halo:
            a = jnp.concatenate([a, ap_ref[...], an_ref[...]], axis=0)
        m = jnp.dot(a, wo_ref[...], preferred_element_type=F32)
        x = _post_norm(x, gate1 * m[0:tm], g1_ref[...], b1_ref[...])
        if halo:
            xp = _post_norm(xp, gate1 * m[tm + 8:tm + 16], g1_ref[...], b1_ref[...])
            xn = _post_norm(xn, gate1 * m[tm + 16:tm + 24], g1_ref[...], b1_ref[...])
    h = _modulate(x, shift, scale)
    for cb in range(n_slab):
        for s in range(SUBLANES):
            slab_ref[cb, pl.ds(s, seg, stride=SUBLANES), :] = (
                h[s * seg:(s + 1) * seg, cb * LANES:(cb + 1) * LANES])
    for cb in range(n_slab):
        hext_ref[0:tm, cb * LANES:(cb + 1) * LANES] = slab_ref[cb].astype(BF16)
    row = lax.broadcasted_iota(jnp.int32, (SUBLANES, FF_CHUNK), 0)
    if halo:
        tiles_per_seq = seq_len // tm
        pos = pl.program_id(0) % tiles_per_seq
        hp = jnp.where(pos > 0, _modulate(xp, shift, scale), 0.0)
        hn = jnp.where(pos < tiles_per_seq - 1, _modulate(xn, shift, scale), 0.0)
        hext_ref[tm:, :] = jnp.concatenate([hp, hn], axis=0).astype(BF16)
    else:
        assert tm % seq_len == 0 and seq_len % seg == 0
        seq_start = functools.reduce(jnp.logical_or, [row == s for s in range(SUBLANES) if (s * seg) % seq_len == 0])
        seq_end = functools.reduce(jnp.logical_or, [row == s for s in range(SUBLANES) if ((s + 1) * seg) % seq_len == 0])
    hext = hext_ref[...]

    def conv(u, col):
        cw = cw_ref[:, col:col + FF_CHUNK]
        first, last = pltpu.roll(u[tm - 8:tm], 1, 0), pltpu.roll(u[0:8], 7, 0)
        if halo:
            first = jnp.where(row == 0, u[tm + 7:tm + 8], first)
            last = jnp.where(row == 7, u[tm + 8:tm + 9], last)
        else:
            first = jnp.where(seq_start, 0.0, first)
            last = jnp.where(seq_end, 0.0, last)
        prev = jnp.concatenate([first, u[0:tm - 8]], axis=0)
        nxt = jnp.concatenate([u[8:tm], last], axis=0)
        return prev * cw[0:1] + u[0:tm] * cw[1:2] + nxt * cw[2:3] + cb_ref[:, col:col + FF_CHUNK]

    for c in range(D_FF // FF_CHUNK):
        ca, cg = c * FF_CHUNK, D_FF + c * FF_CHUNK
        ua = jnp.dot(hext, wup_ref[:, ca:ca + FF_CHUNK], preferred_element_type=F32)
        ug = jnp.dot(hext, wup_ref[:, cg:cg + FF_CHUNK], preferred_element_type=F32)
        act_ref[:, ca:ca + FF_CHUNK] = (_silu(conv(ua, ca)) * conv(ug, cg)).astype(BF16)
    f = jnp.dot(act_ref[...], wdn_ref[...], preferred_element_type=F32)
    for cb in range(n_slab):
        slab_ref[cb] = f[:, cb * LANES:(cb + 1) * LANES]
    for cb in range(n_slab):
        for s in range(SUBLANES):
            o_ref[s * seg:(s + 1) * seg, cb * LANES:(cb + 1) * LANES] = (
                slab_ref[cb, pl.ds(s, seg, stride=SUBLANES), :])
    o_ref[...] = _post_norm(x, gate * o_ref[...], g_ref[...], b_ref[...])


def _ffn(x, mods, layer, seq_len, mod_row, w_up, conv_w, conv_b, w_down, g, b, tm, attn=None):
    rows = x.shape[0]
    halo = seq_len > tm
    kern = functools.partial(_ffn_kernel, tm=tm, seq_len=seq_len, attn_proj=attn is not None)

    def halo_specs(block_rows):
        per_tile = tm // block_rows
        return [
            pl.BlockSpec((block_rows, D_MODEL), lambda i: (jnp.maximum(i * per_tile - 1, 0), 0)),
            pl.BlockSpec((block_rows, D_MODEL),
                         lambda i: (jnp.minimum((i + 1) * per_tile, rows // block_rows - 1), 0)),
        ]

    in_specs = [pl.BlockSpec((tm, D_MODEL), lambda i: (i, 0))]
    args = [x]
    if halo:
        in_specs += halo_specs(SUBLANES)
        args += [x, x]
    if attn is not None:
        a, w_o, g1, b1 = attn
        in_specs += [pl.BlockSpec((tm, D_MODEL), lambda i: (i, 0))]
        args += [a]
        if halo:
            in_specs += halo_specs(HALO)
            args += [a, a]
        in_specs += [_layer_spec((D_MODEL, D_MODEL), layer // 2),
                     _layer_spec((1, D_MODEL), layer), _layer_spec((1, D_MODEL), layer)]
        args += [w_o, g1, b1]
    in_specs += [
        pl.BlockSpec((1, 6, D_MODEL), lambda i: (layer * MOD_ROWS + mod_row(i), 0, 0)),
        _layer_spec((D_MODEL, 2 * D_FF), layer),
        _layer_spec((3, 2 * D_FF), layer),
        _layer_spec((1, 2 * D_FF), layer),
        _layer_spec((D_FF, D_MODEL), layer),
        _layer_spec((1, D_MODEL), layer),
        _layer_spec((1, D_MODEL), layer),
    ]
    args += [mods, w_up, conv_w, conv_b, w_down, g, b]
    return pl.pallas_call(
        kern,
        out_shape=jax.ShapeDtypeStruct(x.shape, F32),
        grid=(rows // tm,),
        in_specs=in_specs,
        out_specs=pl.BlockSpec((tm, D_MODEL), lambda i: (i, 0)),
        scratch_shapes=[
            pltpu.VMEM((D_MODEL // LANES, tm, LANES), F32),
            pltpu.VMEM((tm + (HALO if halo else 0), D_MODEL), BF16),
            pltpu.VMEM((tm, D_FF), BF16),
        ],
        compiler_params=_params(("arbitrary",)),
        name="conv_ffn",
    )(*args)


def _fourier_kernel(x_ref, mod_ref, cs_ref, dc_ref, ds_ref, wf_ref, g_ref, b_ref, o_ref, y_ref,
                    *, n, tq, rows_per_step):
    j = pl.program_id(1)
    shift, scale, gate = mod_ref[0, 0:1, :], mod_ref[0, 1:2, :], mod_ref[0, 2:3, :]

    @pl.when(j == 0)
    def _():
        for r in range(n // rows_per_step):
            r0 = r * rows_per_step
            h = _modulate(x_ref[r0:r0 + rows_per_step, :], shift, scale).astype(BF16)
            for gi in range(N_FOURIER_GROUPS):
                c0 = gi * FOURIER_GROUP
                y = jnp.dot(h[:, c0:c0 + FOURIER_GROUP], cs_ref[...], preferred_element_type=F32)
                y_ref[r0:r0 + rows_per_step, c0:c0 + FOURIER_GROUP] = y[:, :FOURIER_GROUP].astype(BF16)
                y_ref[n + r0:n + r0 + rows_per_step, c0:c0 + FOURIER_GROUP] = y[:, FOURIER_GROUP:].astype(BF16)

    f = (jnp.dot(dc_ref[...], y_ref[0:n, :], preferred_element_type=F32)
         + jnp.dot(ds_ref[...], y_ref[n:, :], preferred_element_type=F32))
    m = jnp.dot(f.astype(BF16), wf_ref[...], preferred_element_type=F32)
    x = x_ref[pl.ds(pl.multiple_of(j * tq, tq), tq), :]
    o_ref[...] = _post_norm(x, gate * m, g_ref[...], b_ref[...])


def _fourier(x, mods, layer, seq_len, mod_row, cs, dft, w_f, g, b, tq):
    rows = x.shape[0]
    n_seq = rows // seq_len
    steps = seq_len // tq
    kern = functools.partial(_fourier_kernel, n=seq_len, tq=tq, rows_per_step=min(seq_len, 512))
    return pl.pallas_call(
        kern,
        out_shape=jax.ShapeDtypeStruct(x.shape, F32),
        grid=(n_seq, steps),
        in_specs=[
            pl.BlockSpec((seq_len, D_MODEL), lambda s, j: (s, 0)),
            pl.BlockSpec((1, 6, D_MODEL), lambda s, j: (layer * MOD_ROWS + mod_row(s), 0, 0)),
            _const_spec((FOURIER_GROUP, 2 * FOURIER_GROUP)),
            pl.BlockSpec((tq, seq_len), lambda s, j: (j, 0)),
            pl.BlockSpec((tq, seq_len), lambda s, j: (j, 0)),
            _layer_spec((D_MODEL, D_MODEL), layer // 2),
            _layer_spec((1, D_MODEL), layer),
            _layer_spec((1, D_MODEL), layer),
        ],
        out_specs=pl.BlockSpec((tq, D_MODEL), lambda s, j: (s * steps + j, 0)),
        scratch_shapes=[pltpu.VMEM((2 * seq_len, D_MODEL), BF16)],
        compiler_params=_params(("arbitrary", "arbitrary")),
        name="fourier_mix",
    )(x, mods, cs, dft[0], dft[1], w_f, g, b)


def _cos_sin(k, t, n):
    ang = ((k * t) % n).astype(F32) * (2.0 * math.pi / n)
    return jnp.cos(ang), jnp.sin(ang)


def _dft_table_kernel(ca_ref, sa_ref, cb_ref, sb_ref, c_ref, s_ref, *, n):
    cb, sb = cb_ref[...], sb_ref[...]
    norm = n ** -0.5
    for i in range(n // LANES):
        ca, sa = ca_ref[:, i:i + 1], sa_ref[:, i:i + 1]
        c_ref[:, i * LANES:(i + 1) * LANES] = ((ca * cb - sa * sb) * norm).astype(BF16)
        s_ref[:, i * LANES:(i + 1) * LANES] = ((sa * cb + ca * sb) * -norm).astype(BF16)


def _dft_table(n):
    n_hi = n // LANES
    k = lax.broadcasted_iota(jnp.int32, (n, n_hi), 0)
    t_hi = lax.broadcasted_iota(jnp.int32, (n, n_hi), 1) * LANES
    ca, sa = _cos_sin(k, t_hi, n)
    k = lax.broadcasted_iota(jnp.int32, (n, LANES), 0)
    t_lo = lax.broadcasted_iota(jnp.int32, (n, LANES), 1)
    cb, sb = _cos_sin(k, t_lo, n)
    tr = min(n, 256)
    hi_spec = pl.BlockSpec((tr, n_hi), lambda i: (i, 0))
    lo_spec = pl.BlockSpec((tr, LANES), lambda i: (i, 0))
    out_spec = pl.BlockSpec((tr, n), lambda i: (i, 0))
    table = jax.ShapeDtypeStruct((n, n), BF16)
    return pl.pallas_call(
        functools.partial(_dft_table_kernel, n=n),
        out_shape=[table, table],
        grid=(n // tr,),
        in_specs=[hi_spec, hi_spec, lo_spec, lo_spec],
        out_specs=[out_spec, out_spec],
        compiler_params=_params(("arbitrary",)),
        name="dft_table",
    )(ca, sa, cb, sb)


def _channel_dft_table():
    n = FOURIER_GROUP
    k = lax.broadcasted_iota(jnp.int32, (n, n), 0)
    t = lax.broadcasted_iota(jnp.int32, (n, n), 1)
    cos, sin = _cos_sin(k, t, n)
    return (jnp.concatenate([cos, sin], axis=1) * (n ** -0.5)).astype(BF16)


def _qkv_kernel(*refs, tm, rope, cache, n_prev):
    x_ref, mod_ref, w_ref = refs[:3]
    refs = refs[3:]
    if rope:
        cos_ref, sin_ref = refs[:2]
        refs = refs[2:]
    if n_prev:
        prev_k_ref, prev_v_ref = refs[:2]
        refs = refs[2:]
    q0_ref, q1_ref, k_ref, v_ref = refs[:4]
    if cache:
        kc_ref, vc_ref = refs[4:6]
    if n_prev:
        kc_ref[:, 0:n_prev] = prev_k_ref[...]
        vc_ref[:, 0:n_prev] = prev_v_ref[...]
    shift, scale = mod_ref[0, 0:1, :], mod_ref[0, 1:2, :]
    h = _modulate(x_ref[...], shift, scale).astype(BF16)
    lane = lax.broadcasted_iota(jnp.int32, (tm, QKV_CHUNK), 1)
    first_half = (lane % V_DIM) < HEAD_DIM
    if rope:
        reps = QKV_CHUNK // V_DIM
        cos = jnp.concatenate([cos_ref[...]] * reps, axis=1)
        sin = jnp.concatenate([sin_ref[...]] * reps, axis=1)
        swap_up = (lane % (2 * N_FREQ)) < N_FREQ

    def rotary(y):
        up = jnp.concatenate([pltpu.roll(y[:, c:c + LANES], LANES - N_FREQ, 1)
                              for c in range(0, QKV_CHUNK, LANES)], axis=1)
        down = jnp.concatenate([pltpu.roll(y[:, c:c + LANES], N_FREQ, 1)
                                for c in range(0, QKV_CHUNK, LANES)], axis=1)
        return y * cos + jnp.where(swap_up, up, down) * sin

    def proj(col):
        return jnp.dot(h, w_ref[:, col:col + QKV_CHUNK], preferred_element_type=F32)

    for c0 in range(0, D_MODEL, QKV_CHUNK):
        cols = slice(c0, c0 + QKV_CHUNK)
        q, k, v = proj(c0), proj(D_MODEL + c0), proj(2 * D_MODEL + c0)
        if rope:
            q, k = rotary(q), rotary(k)
        q = q * Q_SCALE
        q0_ref[:, cols] = jnp.where(first_half, q, 0.0).astype(BF16)
        q1_ref[:, cols] = jnp.where(first_half, 0.0, q).astype(BF16)
        k_ref[:, cols] = k.astype(BF16)
        v_ref[:, cols] = v.astype(BF16)
        if cache:
            kc_ref[:, n_prev, :, cols] = k.reshape(kc_ref.shape[0], kc_ref.shape[2], QKV_CHUNK)
            vc_ref[:, n_prev, :, cols] = v.reshape(vc_ref.shape[0], vc_ref.shape[2], QKV_CHUNK)


def _qkv(x, mods, layer, seq_len, mod_row, w_qkv, tm, rope_tables=None, cache=None):
    rows = x.shape[0]
    rope = rope_tables is not None
    with_cache = cache is not None
    n_prev = cache[0].shape[1] if cache else 0
    tiles_per_seq = max(seq_len // tm, 1)
    in_specs = [
        pl.BlockSpec((tm, D_MODEL), lambda i: (i, 0)),
        pl.BlockSpec((1, 6, D_MODEL), lambda i: (layer * MOD_ROWS + mod_row(i), 0, 0)),
        _layer_spec((D_MODEL, 3 * D_MODEL), layer // 2),
    ]
    args = [x, mods, w_qkv]
    if rope:
        in_specs += [pl.BlockSpec((tm, V_DIM), lambda i: (i % tiles_per_seq, 0))] * 2
        args += list(rope_tables)
    act = jax.ShapeDtypeStruct((rows, D_MODEL), BF16)
    out_shape = [act] * 4
    out_specs = [pl.BlockSpec((tm, D_MODEL), lambda i: (i, 0))] * 4
    if with_cache:
        seqs = tm // seq_len
        n_seq = rows // seq_len
        if n_prev:
            in_specs += [pl.BlockSpec((seqs, n_prev, seq_len, D_MODEL), lambda i: (i, 0, 0, 0))] * 2
            args += list(cache)
        grown = jax.ShapeDtypeStruct((n_seq, n_prev + 1, seq_len, D_MODEL), F32)
        out_shape += [grown, grown]
        out_specs += [pl.BlockSpec((seqs, n_prev + 1, seq_len, D_MODEL), lambda i: (i, 0, 0, 0))] * 2
    kern = functools.partial(_qkv_kernel, tm=tm, rope=rope, cache=with_cache, n_prev=n_prev)
    return pl.pallas_call(
        kern,
        out_shape=out_shape,
        grid=(rows // tm,),
        in_specs=in_specs,
        out_specs=out_specs,
        compiler_params=_params(("arbitrary",)),
        name="qkv_proj",
    )(*args)


def _attn_kernel(*refs, hb, tq, lam_init, cache):
    lam_ref, g_ref, q0_ref, q1_ref, k_ref, v_ref = refs[:6]
    if cache:
        ck_ref, cv_ref = refs[6:8]
    o_ref = refs[-1]
    lv = lam_ref[...]
    lam = (jnp.exp(jnp.sum(lv[0:1] * lv[1:2], axis=-1, keepdims=True))
           - jnp.exp(jnp.sum(lv[2:3] * lv[3:4], axis=-1, keepdims=True)) + lam_init)
    nt = (((1,), (1,)), ((), ()))

    def scores(hh):
        cols = slice(hh * V_DIM, (hh + 1) * V_DIM)
        qq = jnp.concatenate([q0_ref[:, cols], q1_ref[:, cols]], axis=0)
        s = lax.dot_general(qq, k_ref[:, cols], nt, preferred_element_type=F32)
        sc = None
        if cache:
            sc = lax.dot_general(qq, ck_ref[:, cols].astype(BF16), nt, preferred_element_type=F32)
        return s, sc

    def finish(hh, s, sc):
        cols = slice(hh * V_DIM, (hh + 1) * V_DIM)
        m = jnp.max(s, axis=-1, keepdims=True)
        if cache:
            m = jnp.maximum(m, jnp.max(sc, axis=-1, keepdims=True))
        p = jnp.exp2(s - m)
        l = jnp.sum(p, axis=-1, keepdims=True)
        if cache:
            pc = jnp.exp2(sc - m)
            l = l + jnp.sum(pc, axis=-1, keepdims=True)
        ratio = l[:tq] * lam / l[tq:]
        w = (p[:tq] - p[tq:] * ratio).astype(BF16)
        o = jnp.dot(w, v_ref[:, cols], preferred_element_type=F32)
        if cache:
            wc = (pc[:tq] - pc[tq:] * ratio).astype(BF16)
            o = o + jnp.dot(wc, cv_ref[:, cols].astype(BF16), preferred_element_type=F32)
        o = o / l[:tq]
        o = o * lax.rsqrt(jnp.mean(o * o, axis=-1, keepdims=True) + SUBLN_EPS)
        o = o * g_ref[...] * (1.0 - lam_init)
        o_ref[:, cols] = o.astype(BF16)

    pending = scores(0)
    for hh in range(hb):
        following = scores(hh + 1) if hh + 1 < hb else None
        finish(hh, *pending)
        pending = following


def _attention(q0, q1, k, v, lam_vecs, subln_g, lam_init, seq_len, tq, hb, cache=None):
    rows = q0.shape[0]
    n_seq = rows // seq_len
    q_steps = seq_len // tq
    wb = hb * V_DIM
    with_cache = cache is not None
    q_spec = pl.BlockSpec((tq, wb), lambda s, h, j: (s * q_steps + j, h))
    kv_spec = pl.BlockSpec((seq_len, wb), lambda s, h, j: (s, h))
    in_specs = [_const_spec((4, HEAD_DIM)), _const_spec((1, V_DIM)), q_spec, q_spec, kv_spec, kv_spec]
    args = [lam_vecs, subln_g, q0, q1, k, v]
    if with_cache:
        ck, cv, slot = cache
        past = ck.shape[2]
        c_spec = pl.BlockSpec((None, None, past, wb), lambda s, h, j: (s, slot, 0, h))
        in_specs += [c_spec, c_spec]
        args += [ck, cv]
    kern = functools.partial(_attn_kernel, hb=hb, tq=tq, lam_init=lam_init, cache=with_cache)
    return pl.pallas_call(
        kern,
        out_shape=jax.ShapeDtypeStruct((rows, D_MODEL), BF16),
        grid=(n_seq, N_HEADS // hb, q_steps),
        in_specs=in_specs,
        out_specs=q_spec,
        compiler_params=_params(("arbitrary", "arbitrary", "arbitrary")),
        name="diff_attention",
    )(*args)


def _rope_tables(n):
    token = lax.broadcasted_iota(jnp.int32, (n, N_FREQ), 0)
    row = (token // GRID_W).astype(F32)
    col = (token % GRID_W).astype(F32)
    inv = 1.0 / (ROPE_THETA ** (jnp.arange(N_FREQ, dtype=F32) / N_FREQ))
    ar = row * inv
    ac = col * inv
    ang = jnp.concatenate([ar, ar, ac, ac], axis=-1)
    sign = jnp.tile(jnp.concatenate([-jnp.ones((N_FREQ,), F32), jnp.ones((N_FREQ,), F32)]), 2)
    cos = jnp.cos(ang)
    sin = jnp.sin(ang) * sign
    return jnp.tile(cos, (1, 2)), jnp.tile(sin, (1, 2))


def _tiles(ctx_len, lat_len):
    return dict(
        ffn_ctx=2 * ctx_len, ffn_lat=512,
        row_ctx=512, row_lat=512,
        fourier_ctx=ctx_len, fourier_lat=512,
        attn_q_ctx=ctx_len, attn_q_lat=256,
        attn_heads_ctx=N_HEADS, attn_heads_lat=N_HEADS,
    )


def kernel(x_prompt, x_sample, cache_k, cache_v, c, c_ctx, w_ada, b_ada, w_fourier, w_qkv,
           lambda_q1, lambda_k1, lambda_q2, lambda_k2, subln_g, w_o, w_up, conv_w, conv_b,
           w_down, ln1_g, ln1_b, ln2_g, ln2_b):
    n_ctx_seq, ctx_len, d = x_prompt.shape
    n_lat_seq, lat_len, _ = x_sample.shape
    past = cache_k.shape[2]
    assert d == D_MODEL and n_lat_seq + 1 <= MOD_ROWS
    t = _tiles(ctx_len, lat_len)

    cvec = jnp.concatenate(
        [c_ctx[None, :], c, jnp.zeros((MOD_ROWS - 1 - n_lat_seq, d), F32)], axis=0)
    mods = _ada_table(cvec, w_ada, b_ada)

    xc = x_prompt.reshape(n_ctx_seq * ctx_len, d)
    xl = x_sample.reshape(n_lat_seq * lat_len, d)
    ck_in = cache_k.reshape(n_lat_seq, DEPTH // 2, past, d)
    cv_in = cache_v.reshape(n_lat_seq, DEPTH // 2, past, d)

    ctx_row = lambda i: 0
    lat_row_of = lambda tile: (lambda i: 1 + i // (lat_len // tile))
    cs = _channel_dft_table()
    dft_ctx, dft_lat = _dft_table(ctx_len), _dft_table(lat_len)
    rope = _rope_tables(lat_len)
    caches = ()

    wf, wqkv, wo = w_fourier.astype(BF16), w_qkv.astype(BF16), w_o.astype(BF16)
    wup, wdn = w_up.astype(BF16), w_down.astype(BF16)
    cw, cb = conv_w, conv_b[:, None, :]
    g1, b1 = ln1_g[:, None, :], ln1_b[:, None, :]
    g2, b2 = ln2_g[:, None, :], ln2_b[:, None, :]

    for i in range(DEPTH):
        j = i // 2
        attn_c = attn_l = None
        if i % 2 == 0:
            xc = _fourier(xc, mods, i, ctx_len, ctx_row, cs, dft_ctx, wf, g1, b1, tq=t["fourier_ctx"])
            xl = _fourier(xl, mods, i, lat_len, lambda s: 1 + s, cs, dft_lat, wf, g1, b1,
                          tq=t["fourier_lat"])
        else:
            lam_init = 0.8 - 0.6 * math.exp(-0.3 * i)
            lam_vecs = jnp.stack([lambda_q1[j], lambda_k1[j], lambda_q2[j], lambda_k2[j]])
            sg = subln_g[j][None, :]
            q0, q1, k, v, new_k, new_v = _qkv(
                xc, mods, i, ctx_len, ctx_row, wqkv, tm=t["row_ctx"], cache=caches)
            caches = (new_k, new_v)
            oc = _attention(q0, q1, k, v, lam_vecs, sg, lam_init, ctx_len,
                            tq=t["attn_q_ctx"], hb=t["attn_heads_ctx"])
            lat_row = lat_row_of(t["row_lat"])
            q0, q1, k, v = _qkv(xl, mods, i, lat_len, lat_row, wqkv, tm=t["row_lat"], rope_tables=rope)
            ol = _attention(q0, q1, k, v, lam_vecs, sg, lam_init, lat_len,
                            tq=t["attn_q_lat"], hb=t["attn_heads_lat"], cache=(ck_in, cv_in, j))
            attn_c, attn_l = (oc, wo, g1, b1), (ol, wo, g1, b1)
        xc = _ffn(xc, mods, i, ctx_len, ctx_row, wup, cw, cb, wdn, g2, b2, tm=t["ffn_ctx"], attn=attn_c)
        xl = _ffn(xl, mods, i, lat_len, lat_row_of(t["ffn_lat"]), wup, cw, cb, wdn, g2, b2,
                  tm=t["ffn_lat"], attn=attn_l)

    y_prompt = xc.reshape(x_prompt.shape)
    y_sample = xl.reshape(x_sample.shape)
    new_cache_k = new_k.reshape(n_ctx_seq, DEPTH // 2, ctx_len, N_HEADS, 2, HEAD_DIM)
    new_cache_v = new_v.reshape(n_ctx_seq, DEPTH // 2, ctx_len, N_HEADS, V_DIM)
    return (y_prompt, y_sample, new_cache_k, new_cache_v)
```

```python
import functools
import math

import jax
import jax.numpy as jnp
from jax import lax
from jax.experimental import pallas as pl
from jax.experimental.pallas import tpu as pltpu

F32 = jnp.float32
BF16 = jnp.bfloat16

D_MODEL = 1024
DEPTH = 4
GRID_W = 64
N_HEADS = 8
HEAD_DIM = 64
V_DIM = 2 * HEAD_DIM
N_FOURIER_GROUPS = 4
FOURIER_GROUP = D_MODEL // N_FOURIER_GROUPS
D_FF = 2816
ROPE_THETA = 10000.0
N_FREQ = HEAD_DIM // 4
DN_ALPHA = (2 * DEPTH) ** 0.25
LN_EPS = 1e-6
SUBLN_EPS = 1e-5

LANES = 128
SUBLANES = 8
MOD_ROWS = 8
FF_CHUNK = 256
HALO = 16
QKV_CHUNK = 256
VMEM_LIMIT = 56 * 1024 * 1024
LOCKSTEP_SCORE_BYTES = 8 * 1024 * 1024
Q_SCALE = HEAD_DIM ** -0.5 * math.log2(math.e)


def _params(sem, vmem=VMEM_LIMIT):
    return pltpu.CompilerParams(dimension_semantics=sem, vmem_limit_bytes=vmem)


def _ln(x):
    mu = jnp.mean(x, axis=-1, keepdims=True)
    xc = x - mu
    var = jnp.mean(xc * xc, axis=-1, keepdims=True)
    return xc * lax.rsqrt(var + LN_EPS)


def _modulate(x, shift, scale):
    return _ln(x) * (1.0 + scale) + shift


def _post_norm(x, update, g, b):
    return _ln(DN_ALPHA * x + update) * g + b


def _silu(x):
    return x / (1.0 + jnp.exp(-x))


def _const_spec(shape):
    return pl.BlockSpec(shape, lambda *_: (0,) * len(shape))


def _layer_spec(shape, layer):
    return pl.BlockSpec((None,) + tuple(shape), lambda *_: (layer,) + (0,) * len(shape))


def _ada_kernel(c_ref, w_ref, b_ref, o_ref):
    s = _silu(c_ref[...]).astype(BF16)
    a = jnp.dot(s, w_ref[0].astype(BF16), preferred_element_type=F32)
    o_ref[0] = a + b_ref[0]


def _ada_table(cvec, w_ada, b_ada):
    nc = D_MODEL
    n_col = w_ada.shape[-1] // nc
    out = pl.pallas_call(
        _ada_kernel,
        out_shape=jax.ShapeDtypeStruct((DEPTH, MOD_ROWS, w_ada.shape[-1]), F32),
        grid=(DEPTH, n_col),
        in_specs=[
            pl.BlockSpec((MOD_ROWS, D_MODEL), lambda l, j: (0, 0)),
            pl.BlockSpec((1, D_MODEL, nc), lambda l, j: (l, 0, j)),
            pl.BlockSpec((1, 1, nc), lambda l, j: (l, 0, j)),
        ],
        out_specs=pl.BlockSpec((1, MOD_ROWS, nc), lambda l, j: (l, 0, j)),
        compiler_params=_params(("arbitrary", "arbitrary")),
        name="ada_table",
    )(cvec, w_ada, b_ada.reshape(DEPTH, 1, -1))
    return out.reshape(DEPTH * MOD_ROWS, 6, D_MODEL)


def _ffn_kernel(*refs, tm, seq_len, attn_proj):
    halo = seq_len > tm
    x_ref = refs[0]
    if halo:
        xp_ref, xn_ref = refs[1:3]
        refs = refs[2:]
    if attn_proj:
        a_ref = refs[1]
        refs = refs[1:]
        if halo:
            ap_ref, an_ref = refs[1:3]
            refs = refs[2:]
        wo_ref, g1_ref, b1_ref = refs[1:4]
        refs = refs[3:]
    (mod_ref, wup_ref, cw_ref, cb_ref, wdn_ref, g_ref, b_ref,
     o_ref, slab_ref, hext_ref, act_ref) = refs[1:]
    seg = tm // SUBLANES
    n_slab = D_MODEL // LANES
    shift, scale, gate = mod_ref[0, 3:4, :], mod_ref[0, 4:5, :], mod_ref[0, 5:6, :]
    x = x_ref[...]
    if halo:
        xp, xn = xp_ref[...], xn_ref[...]
    if attn_proj:
        gate1 = mod_ref[0, 2:3, :]
        a = a_ref[...]
        if halo:
            a = jnp.concatenate([a, ap_ref[...], an_ref[...]], axis=0)
        m = jnp.dot(a, wo_ref[...], preferred_element_type=F32)
        x = _post_norm(x, gate1 * m[0:tm], g1_ref[...], b1_ref[...])
        if halo:
            xp = _post_norm(xp, gate1 * m[tm + 8:tm + 16], g1_ref[...], b1_ref[...])
            xn = _post_norm(xn, gate1 * m[tm + 16:tm + 24], g1_ref[...], b1_ref[...])
    h = _modulate(x, shift, scale)
    for cb in range(n_slab):
        for s in range(SUBLANES):
            slab_ref[cb, pl.ds(s, seg, stride=SUBLANES), :] = (
                h[s * seg:(s + 1) * seg, cb * LANES:(cb + 1) * LANES])
    for cb in range(n_slab):
        hext_ref[0:tm, cb * LANES:(cb + 1) * LANES] = slab_ref[cb].astype(BF16)
    row = lax.broadcasted_iota(jnp.int32, (SUBLANES, FF_CHUNK), 0)
    if halo:
        tiles_per_seq = seq_len // tm
        pos = pl.program_id(0) % tiles_per_seq
        hp = jnp.where(pos > 0, _modulate(xp, shift, scale), 0.0)
        hn = jnp.where(pos < tiles_per_seq - 1, _modulate(xn, shift, scale), 0.0)
        hext_ref[tm:, :] = jnp.concatenate([hp, hn], axis=0).astype(BF16)
    else:
        assert tm % seq_len == 0 and seq_len % seg == 0
        seq_start = functools.reduce(jnp.logical_or, [row == s for s in range(SUBLANES) if (s * seg) % seq_len == 0])
        seq_end = functools.reduce(jnp.logical_or, [row == s for s in range(SUBLANES) if ((s + 1) * seg) % seq_len == 0])
    hext = hext_ref[...]

    def conv(u, col):
        cw = cw_ref[:, col:col + FF_CHUNK]
        first, last = pltpu.roll(u[tm - 8:tm], 1, 0), pltpu.roll(u[0:8], 7, 0)
        if halo:
            first = jnp.where(row == 0, u[tm + 7:tm + 8], first)
            last = jnp.where(row == 7, u[tm + 8:tm + 9], last)
        else:
            first = jnp.where(seq_start, 0.0, first)
            last = jnp.where(seq_end, 0.0, last)
        prev = jnp.concatenate([first, u[0:tm - 8]], axis=0)
        nxt = jnp.concatenate([u[8:tm], last], axis=0)
        return prev * cw[0:1] + u[0:tm] * cw[1:2] + nxt * cw[2:3] + cb_ref[:, col:col + FF_CHUNK]

    for c in range(D_FF // FF_CHUNK):
        ca, cg = c * FF_CHUNK, D_FF + c * FF_CHUNK
        ua = jnp.dot(hext, wup_ref[:, ca:ca + FF_CHUNK], preferred_element_type=F32)
        ug = jnp.dot(hext, wup_ref[:, cg:cg + FF_CHUNK], preferred_element_type=F32)
        act_ref[:, ca:ca + FF_CHUNK] = (_silu(conv(ua, ca)) * conv(ug, cg)).astype(BF16)
    f = jnp.dot(act_ref[...], wdn_ref[...], preferred_element_type=F32)
    for cb in range(n_slab):
        slab_ref[cb] = f[:, cb * LANES:(cb + 1) * LANES]
    for cb in range(n_slab):
        for s in range(SUBLANES):
            o_ref[s * seg:(s + 1) * seg, cb * LANES:(cb + 1) * LANES] = (
                slab_ref[cb, pl.ds(s, seg, stride=SUBLANES), :])
    o_ref[...] = _post_norm(x, gate * o_ref[...], g_ref[...], b_ref[...])


def _ffn(x, mods, layer, seq_len, mod_row, w_up, conv_w, conv_b, w_down, g, b, tm, attn=None):
    rows = x.shape[0]
    halo = seq_len > tm
    kern = functools.partial(_ffn_kernel, tm=tm, seq_len=seq_len, attn_proj=attn is not None)

    def halo_specs(block_rows):
        per_tile = tm // block_rows
        return [
            pl.BlockSpec((block_rows, D_MODEL), lambda i: (jnp.maximum(i * per_tile - 1, 0), 0)),
            pl.BlockSpec((block_rows, D_MODEL),
                         lambda i: (jnp.minimum((i + 1) * per_tile, rows // block_rows - 1), 0)),
        ]

    in_specs = [pl.BlockSpec((tm, D_MODEL), lambda i: (i, 0))]
    args = [x]
    if halo:
        in_specs += halo_specs(SUBLANES)
        args += [x, x]
    if attn is not None:
        a, w_o, g1, b1 = attn
        in_specs += [pl.BlockSpec((tm, D_MODEL), lambda i: (i, 0))]
        args += [a]
        if halo:
            in_specs += halo_specs(HALO)
            args += [a, a]
        in_specs += [_layer_spec((D_MODEL, D_MODEL), layer // 2),
                     _layer_spec((1, D_MODEL), layer), _layer_spec((1, D_MODEL), layer)]
        args += [w_o, g1, b1]
    in_specs += [
        pl.BlockSpec((1, 6, D_MODEL), lambda i: (layer * MOD_ROWS + mod_row(i), 0, 0)),
        _layer_spec((D_MODEL, 2 * D_FF), layer),
        _layer_spec((3, 2 * D_FF), layer),
        _layer_spec((1, 2 * D_FF), layer),
        _layer_spec((D_FF, D_MODEL), layer),
        _layer_spec((1, D_MODEL), layer),
        _layer_spec((1, D_MODEL), layer),
    ]
    args += [mods, w_up, conv_w, conv_b, w_down, g, b]
    return pl.pallas_call(
        kern,
        out_shape=jax.ShapeDtypeStruct(x.shape, F32),
        grid=(rows // tm,),
        in_specs=in_specs,
        out_specs=pl.BlockSpec((tm, D_MODEL), lambda i: (i, 0)),
        scratch_shapes=[
            pltpu.VMEM((D_MODEL // LANES, tm, LANES), F32),
            pltpu.VMEM((tm + (HALO if halo else 0), D_MODEL), BF16),
            pltpu.VMEM((tm, D_FF), BF16),
        ],
        compiler_params=_params(("arbitrary",)),
        name="conv_ffn",
    )(*args)


def _fourier_kernel(x_ref, mod_ref, cs_ref, dc_ref, ds_ref, wf_ref, g_ref, b_ref, o_ref, y_ref,
                    *, n, tq, rows_per_step):
    j = pl.program_id(1)
    shift, scale, gate = mod_ref[0, 0:1, :], mod_ref[0, 1:2, :], mod_ref[0, 2:3, :]

    @pl.when(j == 0)
    def _():
        for r in range(n // rows_per_step):
            r0 = r * rows_per_step
            h = _modulate(x_ref[r0:r0 + rows_per_step, :], shift, scale).astype(BF16)
            for gi in range(N_FOURIER_GROUPS):
                c0 = gi * FOURIER_GROUP
                y = jnp.dot(h[:, c0:c0 + FOURIER_GROUP], cs_ref[...], preferred_element_type=F32)
                y_ref[r0:r0 + rows_per_step, c0:c0 + FOURIER_GROUP] = y[:, :FOURIER_GROUP].astype(BF16)
                y_ref[n + r0:n + r0 + rows_per_step, c0:c0 + FOURIER_GROUP] = y[:, FOURIER_GROUP:].astype(BF16)

    f = (jnp.dot(dc_ref[...], y_ref[0:n, :], preferred_element_type=F32)
         + jnp.dot(ds_ref[...], y_ref[n:, :], preferred_element_type=F32))
    m = jnp.dot(f.astype(BF16), wf_ref[...], preferred_element_type=F32)
    x = x_ref[pl.ds(pl.multiple_of(j * tq, tq), tq), :]
    o_ref[...] = _post_norm(x, gate * m, g_ref[...], b_ref[...])


def _fourier(x, mods, layer, seq_len, mod_row, cs, dft, w_f, g, b, tq):
    rows = x.shape[0]
    n_seq = rows // seq_len
    steps = seq_len // tq
    kern = functools.partial(_fourier_kernel, n=seq_len, tq=tq, rows_per_step=min(seq_len, 512))
    return pl.pallas_call(
        kern,
        out_shape=jax.ShapeDtypeStruct(x.shape, F32),
        grid=(n_seq, steps),
        in_specs=[
            pl.BlockSpec((seq_len, D_MODEL), lambda s, j: (s, 0)),
            pl.BlockSpec((1, 6, D_MODEL), lambda s, j: (layer * MOD_ROWS + mod_row(s), 0, 0)),
            _const_spec((FOURIER_GROUP, 2 * FOURIER_GROUP)),
            pl.BlockSpec((tq, seq_len), lambda s, j: (j, 0)),
            pl.BlockSpec((tq, seq_len), lambda s, j: (j, 0)),
            _layer_spec((D_MODEL, D_MODEL), layer // 2),
            _layer_spec((1, D_MODEL), layer),
            _layer_spec((1, D_MODEL), layer),
        ],
        out_specs=pl.BlockSpec((tq, D_MODEL), lambda s, j: (s * steps + j, 0)),
        scratch_shapes=[pltpu.VMEM((2 * seq_len, D_MODEL), BF16)],
        compiler_params=_params(("arbitrary", "arbitrary")),
        name="fourier_mix",
    )(x, mods, cs, dft[0], dft[1], w_f, g, b)


def _cos_sin(k, t, n):
    ang = ((k * t) % n).astype(F32) * (2.0 * math.pi / n)
    return jnp.cos(ang), jnp.sin(ang)


def _dft_table_kernel(ca_ref, sa_ref, cb_ref, sb_ref, c_ref, s_ref, *, n):
    cb, sb = cb_ref[...], sb_ref[...]
    norm = n ** -0.5
    for i in range(n // LANES):
        ca, sa = ca_ref[:, i:i + 1], sa_ref[:, i:i + 1]
        c_ref[:, i * LANES:(i + 1) * LANES] = ((ca * cb - sa * sb) * norm).astype(BF16)
        s_ref[:, i * LANES:(i + 1) * LANES] = ((sa * cb + ca * sb) * -norm).astype(BF16)


def _dft_table(n):
    n_hi = n // LANES
    k = lax.broadcasted_iota(jnp.int32, (n, n_hi), 0)
    t_hi = lax.broadcasted_iota(jnp.int32, (n, n_hi), 1) * LANES
    ca, sa = _cos_sin(k, t_hi, n)
    k = lax.broadcasted_iota(jnp.int32, (n, LANES), 0)
    t_lo = lax.broadcasted_iota(jnp.int32, (n, LANES), 1)
    cb, sb = _cos_sin(k, t_lo, n)
    tr = min(n, 256)
    hi_spec = pl.BlockSpec((tr, n_hi), lambda i: (i, 0))
    lo_spec = pl.BlockSpec((tr, LANES), lambda i: (i, 0))
    out_spec = pl.BlockSpec((tr, n), lambda i: (i, 0))
    table = jax.ShapeDtypeStruct((n, n), BF16)
    return pl.pallas_call(
        functools.partial(_dft_table_kernel, n=n),
        out_shape=[table, table],
        grid=(n // tr,),
        in_specs=[hi_spec, hi_spec, lo_spec, lo_spec],
        out_specs=[out_spec, out_spec],
        compiler_params=_params(("arbitrary",)),
        name="dft_table",
    )(ca, sa, cb, sb)


def _channel_dft_table():
    n = FOURIER_GROUP
    k = lax.broadcasted_iota(jnp.int32, (n, n), 0)
    t = lax.broadcasted_iota(jnp.int32, (n, n), 1)
    cos, sin = _cos_sin(k, t, n)
    return (jnp.concatenate([cos, sin], axis=1) * (n ** -0.5)).astype(BF16)


def _qkv_kernel(*refs, tm, rope, cache, n_prev):
    x_ref, mod_ref, w_ref = refs[:3]
    refs = refs[3:]
    if rope:
        cos_ref, sin_ref = refs[:2]
        refs = refs[2:]
    if n_prev:
        prev_k_ref, prev_v_ref = refs[:2]
        refs = refs[2:]
    q0_ref, q1_ref, k_ref, v_ref = refs[:4]
    if cache:
        kc_ref, vc_ref = refs[4:6]
    if n_prev:
        kc_ref[:, 0:n_prev] = prev_k_ref[...]
        vc_ref[:, 0:n_prev] = prev_v_ref[...]
    shift, scale = mod_ref[0, 0:1, :], mod_ref[0, 1:2, :]
    h = _modulate(x_ref[...], shift, scale).astype(BF16)
    lane = lax.broadcasted_iota(jnp.int32, (tm, QKV_CHUNK), 1)
    first_half = (lane % V_DIM) < HEAD_DIM
    if rope:
        reps = QKV_CHUNK // V_DIM
        cos = jnp.concatenate([cos_ref[...]] * reps, axis=1)
        sin = jnp.concatenate([sin_ref[...]] * reps, axis=1)
        swap_up = (lane % (2 * N_FREQ)) < N_FREQ

    def rotary(y):
        up = jnp.concatenate([pltpu.roll(y[:, c:c + LANES], LANES - N_FREQ, 1)
                              for c in range(0, QKV_CHUNK, LANES)], axis=1)
        down = jnp.concatenate([pltpu.roll(y[:, c:c + LANES], N_FREQ, 1)
                                for c in range(0, QKV_CHUNK, LANES)], axis=1)
        return y * cos + jnp.where(swap_up, up, down) * sin

    def proj(col):
        return jnp.dot(h, w_ref[:, col:col + QKV_CHUNK], preferred_element_type=F32)

    for c0 in range(0, D_MODEL, QKV_CHUNK):
        cols = slice(c0, c0 + QKV_CHUNK)
        q, k, v = proj(c0), proj(D_MODEL + c0), proj(2 * D_MODEL + c0)
        if rope:
            q, k = rotary(q), rotary(k)
        q = q * Q_SCALE
        q0_ref[:, cols] = jnp.where(first_half, q, 0.0).astype(BF16)
        q1_ref[:, cols] = jnp.where(first_half, 0.0, q).astype(BF16)
        k_ref[:, cols] = k.astype(BF16)
        v_ref[:, cols] = v.astype(BF16)
        if cache:
            kc_ref[:, n_prev, :, cols] = k.reshape(kc_ref.shape[0], kc_ref.shape[2], QKV_CHUNK)
            vc_ref[:, n_prev, :, cols] = v.reshape(vc_ref.shape[0], vc_ref.shape[2], QKV_CHUNK)


def _qkv(x, mods, layer, seq_len, mod_row, w_qkv, tm, rope_tables=None, cache=None):
    rows = x.shape[0]
    rope = rope_tables is not None
    with_cache = cache is not None
    n_prev = cache[0].shape[1] if cache else 0
    tiles_per_seq = max(seq_len // tm, 1)
    in_specs = [
        pl.BlockSpec((tm, D_MODEL), lambda i: (i, 0)),
        pl.BlockSpec((1, 6, D_MODEL), lambda i: (layer * MOD_ROWS + mod_row(i), 0, 0)),
        _layer_spec((D_MODEL, 3 * D_MODEL), layer // 2),
    ]
    args = [x, mods, w_qkv]
    if rope:
        in_specs += [pl.BlockSpec((tm, V_DIM), lambda i: (i % tiles_per_seq, 0))] * 2
        args += list(rope_tables)
    act = jax.ShapeDtypeStruct((rows, D_MODEL), BF16)
    out_shape = [act] * 4
    out_specs = [pl.BlockSpec((tm, D_MODEL), lambda i: (i, 0))] * 4
    if with_cache:
        seqs = tm // seq_len
        n_seq = rows // seq_len
        if n_prev:
            in_specs += [pl.BlockSpec((seqs, n_prev, seq_len, D_MODEL), lambda i: (i, 0, 0, 0))] * 2
            args += list(cache)
        grown = jax.ShapeDtypeStruct((n_seq, n_prev + 1, seq_len, D_MODEL), F32)
        out_shape += [grown, grown]
        out_specs += [pl.BlockSpec((seqs, n_prev + 1, seq_len, D_MODEL), lambda i: (i, 0, 0, 0))] * 2
    kern = functools.partial(_qkv_kernel, tm=tm, rope=rope, cache=with_cache, n_prev=n_prev)
    return pl.pallas_call(
        kern,
        out_shape=out_shape,
        grid=(rows // tm,),
        in_specs=in_specs,
        out_specs=out_specs,
        compiler_params=_params(("arbitrary",)),
        name="qkv_proj",
    )(*args)


def _attn_kernel(*refs, hb, tq, lam_init, cache, lockstep):
    lam_ref, g_ref, q0_ref, q1_ref, k_ref, v_ref = refs[:6]
    if cache:
        ck_ref, cv_ref = refs[6:8]
    o_ref = refs[-1]
    lv = lam_ref[...]
    lam = (jnp.exp(jnp.sum(lv[0:1] * lv[1:2], axis=-1, keepdims=True))
           - jnp.exp(jnp.sum(lv[2:3] * lv[3:4], axis=-1, keepdims=True)) + lam_init)
    nt = (((1,), (1,)), ((), ()))

    def scores(hh):
        cols = slice(hh * V_DIM, (hh + 1) * V_DIM)
        qq = jnp.concatenate([q0_ref[:, cols], q1_ref[:, cols]], axis=0)
        s = lax.dot_general(qq, k_ref[:, cols], nt, preferred_element_type=F32)
        sc = None
        if cache:
            sc = lax.dot_general(qq, ck_ref[:, cols].astype(BF16), nt, preferred_element_type=F32)
        return s, sc

    def row_max(s, sc):
        m = jnp.max(s, axis=-1, keepdims=True)
        return jnp.maximum(m, jnp.max(sc, axis=-1, keepdims=True)) if cache else m

    def exponentials(s, sc, m):
        return jnp.exp2(s - m), (jnp.exp2(sc - m) if cache else None)

    def row_sum(p, pc):
        l = jnp.sum(p, axis=-1, keepdims=True)
        return l + jnp.sum(pc, axis=-1, keepdims=True) if cache else l

    def weighted_values(hh, p, pc, l):
        cols = slice(hh * V_DIM, (hh + 1) * V_DIM)
        ratio = l[:tq] * lam / l[tq:]
        w = (p[:tq] - p[tq:] * ratio).astype(BF16)
        o = jnp.dot(w, v_ref[:, cols], preferred_element_type=F32)
        if cache:
            wc = (pc[:tq] - pc[tq:] * ratio).astype(BF16)
            o = o + jnp.dot(wc, cv_ref[:, cols].astype(BF16), preferred_element_type=F32)
        return o / l[:tq]

    def sub_norm(o):
        return lax.rsqrt(jnp.mean(o * o, axis=-1, keepdims=True) + SUBLN_EPS)

    def emit(hh, o, r):
        o_ref[:, hh * V_DIM:(hh + 1) * V_DIM] = (o * r * g_ref[...] * (1.0 - lam_init)).astype(BF16)

    heads = range(hb)
    if lockstep:
        S = [scores(hh) for hh in heads]
        M = [row_max(*S[hh]) for hh in heads]
        P = [exponentials(*S[hh], M[hh]) for hh in heads]
        L = [row_sum(*P[hh]) for hh in heads]
        O = [weighted_values(hh, *P[hh], L[hh]) for hh in heads]
        R = [sub_norm(O[hh]) for hh in heads]
        for hh in heads:
            emit(hh, O[hh], R[hh])
    else:
        pending = scores(0)
        for hh in heads:
            following = scores(hh + 1) if hh + 1 < hb else None
            s, sc = pending
            p, pc = exponentials(s, sc, row_max(s, sc))
            o = weighted_values(hh, p, pc, row_sum(p, pc))
            emit(hh, o, sub_norm(o))
            pending = following


def _attention(q0, q1, k, v, lam_vecs, subln_g, lam_init, seq_len, tq, hb, cache=None):
    rows = q0.shape[0]
    n_seq = rows // seq_len
    q_steps = seq_len // tq
    wb = hb * V_DIM
    with_cache = cache is not None
    q_spec = pl.BlockSpec((tq, wb), lambda s, h, j: (s * q_steps + j, h))
    kv_spec = pl.BlockSpec((seq_len, wb), lambda s, h, j: (s, h))
    in_specs = [_const_spec((4, HEAD_DIM)), _const_spec((1, V_DIM)), q_spec, q_spec, kv_spec, kv_spec]
    args = [lam_vecs, subln_g, q0, q1, k, v]
    if with_cache:
        ck, cv, slot = cache
        past = ck.shape[2]
        c_spec = pl.BlockSpec((None, None, past, wb), lambda s, h, j: (s, slot, 0, h))
        in_specs += [c_spec, c_spec]
        args += [ck, cv]
    n_keys = seq_len + (cache[0].shape[2] if with_cache else 0)
    lockstep = hb * 2 * tq * n_keys * 4 <= LOCKSTEP_SCORE_BYTES
    kern = functools.partial(_attn_kernel, hb=hb, tq=tq, lam_init=lam_init, cache=with_cache,
                             lockstep=lockstep)
    return pl.pallas_call(
        kern,
        out_shape=jax.ShapeDtypeStruct((rows, D_MODEL), BF16),
        grid=(n_seq, N_HEADS // hb, q_steps),
        in_specs=in_specs,
        out_specs=q_spec,
        compiler_params=_params(("arbitrary", "arbitrary", "arbitrary")),
        name="diff_attention",
    )(*args)


def _rope_tables(n):
    token = lax.broadcasted_iota(jnp.int32, (n, N_FREQ), 0)
    row = (token // GRID_W).astype(F32)
    col = (token % GRID_W).astype(F32)
    inv = 1.0 / (ROPE_THETA ** (jnp.arange(N_FREQ, dtype=F32) / N_FREQ))
    ar = row * inv
    ac = col * inv
    ang = jnp.concatenate([ar, ar, ac, ac], axis=-1)
    sign = jnp.tile(jnp.concatenate([-jnp.ones((N_FREQ,), F32), jnp.ones((N_FREQ,), F32)]), 2)
    cos = jnp.cos(ang)
    sin = jnp.sin(ang) * sign
    return jnp.tile(cos, (1, 2)), jnp.tile(sin, (1, 2))


def _tiles(ctx_len, lat_len):
    return dict(
        ffn_ctx=2 * ctx_len, ffn_lat=512,
        row_ctx=512, row_lat=512,
        fourier_ctx=ctx_len, fourier_lat=512,
        attn_q_ctx=ctx_len, attn_q_lat=256,
        attn_heads_ctx=N_HEADS, attn_heads_lat=N_HEADS,
    )


def kernel(x_prompt, x_sample, cache_k, cache_v, c, c_ctx, w_ada, b_ada, w_fourier, w_qkv,
           lambda_q1, lambda_k1, lambda_q2, lambda_k2, subln_g, w_o, w_up, conv_w, conv_b,
           w_down, ln1_g, ln1_b, ln2_g, ln2_b):
    n_ctx_seq, ctx_len, d = x_prompt.shape
    n_lat_seq, lat_len, _ = x_sample.shape
    past = cache_k.shape[2]
    assert d == D_MODEL and n_lat_seq + 1 <= MOD_ROWS
    t = _tiles(ctx_len, lat_len)

    cvec = jnp.concatenate(
        [c_ctx[None, :], c, jnp.zeros((MOD_ROWS - 1 - n_lat_seq, d), F32)], axis=0)
    mods = _ada_table(cvec, w_ada, b_ada)

    xc = x_prompt.reshape(n_ctx_seq * ctx_len, d)
    xl = x_sample.reshape(n_lat_seq * lat_len, d)
    ck_in = cache_k.reshape(n_lat_seq, DEPTH // 2, past, d)
    cv_in = cache_v.reshape(n_lat_seq, DEPTH // 2, past, d)

    ctx_row = lambda i: 0
    lat_row_of = lambda tile: (lambda i: 1 + i // (lat_len // tile))
    cs = _channel_dft_table()
    dft_ctx, dft_lat = _dft_table(ctx_len), _dft_table(lat_len)
    rope = _rope_tables(lat_len)
    caches = ()

    wf, wqkv, wo = w_fourier.astype(BF16), w_qkv.astype(BF16), w_o.astype(BF16)
    wup, wdn = w_up.astype(BF16), w_down.astype(BF16)
    cw, cb = conv_w, conv_b[:, None, :]
    g1, b1 = ln1_g[:, None, :], ln1_b[:, None, :]
    g2, b2 = ln2_g[:, None, :], ln2_b[:, None, :]

    for i in range(DEPTH):
        j = i // 2
        attn_c = attn_l = None
        if i % 2 == 0:
            xc = _fourier(xc, mods, i, ctx_len, ctx_row, cs, dft_ctx, wf, g1, b1, tq=t["fourier_ctx"])
            xl = _fourier(xl, mods, i, lat_len, lambda s: 1 + s, cs, dft_lat, wf, g1, b1,
                          tq=t["fourier_lat"])
        else:
            lam_init = 0.8 - 0.6 * math.exp(-0.3 * i)
            lam_vecs = jnp.stack([lambda_q1[j], lambda_k1[j], lambda_q2[j], lambda_k2[j]])
            sg = subln_g[j][None, :]
            q0, q1, k, v, new_k, new_v = _qkv(
                xc, mods, i, ctx_len, ctx_row, wqkv, tm=t["row_ctx"], cache=caches)
            caches = (new_k, new_v)
            oc = _attention(q0, q1, k, v, lam_vecs, sg, lam_init, ctx_len,
                            tq=t["attn_q_ctx"], hb=t["attn_heads_ctx"])
            lat_row = lat_row_of(t["row_lat"])
            q0, q1, k, v = _qkv(xl, mods, i, lat_len, lat_row, wqkv, tm=t["row_lat"], rope_tables=rope)
            ol = _attention(q0, q1, k, v, lam_vecs, sg, lam_init, lat_len,
                            tq=t["attn_q_lat"], hb=t["attn_heads_lat"], cache=(ck_in, cv_in, j))
            attn_c, attn_l = (oc, wo, g1, b1), (ol, wo, g1, b1)
        xc = _ffn(xc, mods, i, ctx_len, ctx_row, wup, cw, cb, wdn, g2, b2, tm=t["ffn_ctx"], attn=attn_c)
        xl = _ffn(xl, mods, i, lat_len, lat_row_of(t["ffn_lat"]), wup, cw, cb, wdn, g2, b2,
                  tm=t["ffn_lat"], attn=attn_l)

    y_prompt = xc.reshape(x_prompt.shape)
    y_sample = xl.reshape(x_sample.shape)
    new_cache_k = new_k.reshape(n_ctx_seq, DEPTH // 2, ctx_len, N_HEADS, 2, HEAD_DIM)
    new_cache_v = new_v.reshape(n_ctx_seq, DEPTH // 2, ctx_len, N_HEADS, V_DIM)
    return (y_prompt, y_sample, new_cache_k, new_cache_v)
```

```python
import functools
import math

import jax
import jax.numpy as jnp
from jax import lax
from jax.experimental import pallas as pl
from jax.experimental.pallas import tpu as pltpu

F32 = jnp.float32
BF16 = jnp.bfloat16

D_MODEL = 1024
DEPTH = 4
GRID_W = 64
N_HEADS = 8
HEAD_DIM = 64
V_DIM = 2 * HEAD_DIM
N_FOURIER_GROUPS = 4
FOURIER_GROUP = D_MODEL // N_FOURIER_GROUPS
D_FF = 2816
ROPE_THETA = 10000.0
N_FREQ = HEAD_DIM // 4
DN_ALPHA = (2 * DEPTH) ** 0.25
LN_EPS = 1e-6
SUBLN_EPS = 1e-5

LANES = 128
SUBLANES = 8
MOD_ROWS = 8
FF_CHUNK = 256
HALO = 16
QKV_CHUNK = 256
VMEM_LIMIT = 56 * 1024 * 1024
LOCKSTEP_SCORE_BYTES = 8 * 1024 * 1024
SCORE_LOOKAHEAD = 2
Q_SCALE = HEAD_DIM ** -0.5 * math.log2(math.e)


def _params(sem, vmem=VMEM_LIMIT):
    return pltpu.CompilerParams(dimension_semantics=sem, vmem_limit_bytes=vmem)


def _ln(x):
    mu = jnp.mean(x, axis=-1, keepdims=True)
    xc = x - mu
    var = jnp.mean(xc * xc, axis=-1, keepdims=True)
    return xc * lax.rsqrt(var + LN_EPS)


def _modulate(x, shift, scale):
    return _ln(x) * (1.0 + scale) + shift


def _post_norm(x, update, g, b):
    return _ln(DN_ALPHA * x + update) * g + b


def _silu(x):
    return x / (1.0 + jnp.exp(-x))


def _const_spec(shape):
    return pl.BlockSpec(shape, lambda *_: (0,) * len(shape))


def _layer_spec(shape, layer):
    return pl.BlockSpec((None,) + tuple(shape), lambda *_: (layer,) + (0,) * len(shape))


def _ada_kernel(c_ref, w_ref, b_ref, o_ref):
    s = _silu(c_ref[...]).astype(BF16)
    a = jnp.dot(s, w_ref[0].astype(BF16), preferred_element_type=F32)
    o_ref[0] = a + b_ref[0]


def _ada_table(cvec, w_ada, b_ada):
    nc = D_MODEL
    n_col = w_ada.shape[-1] // nc
    out = pl.pallas_call(
        _ada_kernel,
        out_shape=jax.ShapeDtypeStruct((DEPTH, MOD_ROWS, w_ada.shape[-1]), F32),
        grid=(DEPTH, n_col),
        in_specs=[
            pl.BlockSpec((MOD_ROWS, D_MODEL), lambda l, j: (0, 0)),
            pl.BlockSpec((1, D_MODEL, nc), lambda l, j: (l, 0, j)),
            pl.BlockSpec((1, 1, nc), lambda l, j: (l, 0, j)),
        ],
        out_specs=pl.BlockSpec((1, MOD_ROWS, nc), lambda l, j: (l, 0, j)),
        compiler_params=_params(("arbitrary", "arbitrary")),
        name="ada_table",
    )(cvec, w_ada, b_ada.reshape(DEPTH, 1, -1))
    return out.reshape(DEPTH * MOD_ROWS, 6, D_MODEL)


def _ffn_kernel(*refs, tm, seq_len, attn_proj):
    halo = seq_len > tm
    x_ref = refs[0]
    if halo:
        xp_ref, xn_ref = refs[1:3]
        refs = refs[2:]
    if attn_proj:
        a_ref = refs[1]
        refs = refs[1:]
        if halo:
            ap_ref, an_ref = refs[1:3]
            refs = refs[2:]
        wo_ref, g1_ref, b1_ref = refs[1:4]
        refs = refs[3:]
    (mod_ref, wup_ref, cw_ref, cb_ref, wdn_ref, g_ref, b_ref,
     o_ref, slab_ref, hext_ref, act_ref) = refs[1:]
    seg = tm // SUBLANES
    n_slab = D_MODEL // LANES
    shift, scale, gate = mod_ref[0, 3:4, :], mod_ref[0, 4:5, :], mod_ref[0, 5:6, :]
    x = x_ref[...]
    if halo:
        xp, xn = xp_ref[...], xn_ref[...]
    if attn_proj:
        gate1 = mod_ref[0, 2:3, :]
        a = a_ref[...]
        if halo:
            a = jnp.concatenate([a, ap_ref[...], an_ref[...]], axis=0)
        m = jnp.dot(a, wo_ref[...], preferred_element_type=F32)
        x = _post_norm(x, gate1 * m[0:tm], g1_ref[...], b1_ref[...])
        if halo:
            xp = _post_norm(xp, gate1 * m[tm + 8:tm + 16], g1_ref[...], b1_ref[...])
            xn = _post_norm(xn, gate1 * m[tm + 16:tm + 24], g1_ref[...], b1_ref[...])
    h = _modulate(x, shift, scale)
    for cb in range(n_slab):
        for s in range(SUBLANES):
            slab_ref[cb, pl.ds(s, seg, stride=SUBLANES), :] = (
                h[s * seg:(s + 1) * seg, cb * LANES:(cb + 1) * LANES])
    for cb in range(n_slab):
        hext_ref[0:tm, cb * LANES:(cb + 1) * LANES] = slab_ref[cb].astype(BF16)
    row = lax.broadcasted_iota(jnp.int32, (SUBLANES, FF_CHUNK), 0)
    if halo:
        tiles_per_seq = seq_len // tm
        pos = pl.program_id(0) % tiles_per_seq
        hp = jnp.where(pos > 0, _modulate(xp, shift, scale), 0.0)
        hn = jnp.where(pos < tiles_per_seq - 1, _modulate(xn, shift, scale), 0.0)
        hext_ref[tm:, :] = jnp.concatenate([hp, hn], axis=0).astype(BF16)
    else:
        assert tm % seq_len == 0 and seq_len % seg == 0
        seq_start = functools.reduce(jnp.logical_or, [row == s for s in range(SUBLANES) if (s * seg) % seq_len == 0])
        seq_end = functools.reduce(jnp.logical_or, [row == s for s in range(SUBLANES) if ((s + 1) * seg) % seq_len == 0])
    hext = hext_ref[...]

    def conv(u, col):
        cw = cw_ref[:, col:col + FF_CHUNK]
        first, last = pltpu.roll(u[tm - 8:tm], 1, 0), pltpu.roll(u[0:8], 7, 0)
        if halo:
            first = jnp.where(row == 0, u[tm + 7:tm + 8], first)
            last = jnp.where(row == 7, u[tm + 8:tm + 9], last)
        else:
            first = jnp.where(seq_start, 0.0, first)
            last = jnp.where(seq_end, 0.0, last)
        prev = jnp.concatenate([first, u[0:tm - 8]], axis=0)
        nxt = jnp.concatenate([u[8:tm], last], axis=0)
        return prev * cw[0:1] + u[0:tm] * cw[1:2] + nxt * cw[2:3] + cb_ref[:, col:col + FF_CHUNK]

    for c in range(D_FF // FF_CHUNK):
        ca, cg = c * FF_CHUNK, D_FF + c * FF_CHUNK
        ua = jnp.dot(hext, wup_ref[:, ca:ca + FF_CHUNK], preferred_element_type=F32)
        ug = jnp.dot(hext, wup_ref[:, cg:cg + FF_CHUNK], preferred_element_type=F32)
        act_ref[:, ca:ca + FF_CHUNK] = (_silu(conv(ua, ca)) * conv(ug, cg)).astype(BF16)
    f = jnp.dot(act_ref[...], wdn_ref[...], preferred_element_type=F32)
    for cb in range(n_slab):
        slab_ref[cb] = f[:, cb * LANES:(cb + 1) * LANES]
    for cb in range(n_slab):
        for s in range(SUBLANES):
            o_ref[s * seg:(s + 1) * seg, cb * LANES:(cb + 1) * LANES] = (
                slab_ref[cb, pl.ds(s, seg, stride=SUBLANES), :])
    o_ref[...] = _post_norm(x, gate * o_ref[...], g_ref[...], b_ref[...])


def _ffn(x, mods, layer, seq_len, mod_row, w_up, conv_w, conv_b, w_down, g, b, tm, attn=None):
    rows = x.shape[0]
    halo = seq_len > tm
    kern = functools.partial(_ffn_kernel, tm=tm, seq_len=seq_len, attn_proj=attn is not None)

    def halo_specs(block_rows):
        per_tile = tm // block_rows
        return [
            pl.BlockSpec((block_rows, D_MODEL), lambda i: (jnp.maximum(i * per_tile - 1, 0), 0)),
            pl.BlockSpec((block_rows, D_MODEL),
                         lambda i: (jnp.minimum((i + 1) * per_tile, rows // block_rows - 1), 0)),
        ]

    in_specs = [pl.BlockSpec((tm, D_MODEL), lambda i: (i, 0))]
    args = [x]
    if halo:
        in_specs += halo_specs(SUBLANES)
        args += [x, x]
    if attn is not None:
        a, w_o, g1, b1 = attn
        in_specs += [pl.BlockSpec((tm, D_MODEL), lambda i: (i, 0))]
        args += [a]
        if halo:
            in_specs += halo_specs(HALO)
            args += [a, a]
        in_specs += [_layer_spec((D_MODEL, D_MODEL), layer // 2),
                     _layer_spec((1, D_MODEL), layer), _layer_spec((1, D_MODEL), layer)]
        args += [w_o, g1, b1]
    in_specs += [
        pl.BlockSpec((1, 6, D_MODEL), lambda i: (layer * MOD_ROWS + mod_row(i), 0, 0)),
        _layer_spec((D_MODEL, 2 * D_FF), layer),
        _layer_spec((3, 2 * D_FF), layer),
        _layer_spec((1, 2 * D_FF), layer),
        _layer_spec((D_FF, D_MODEL), layer),
        _layer_spec((1, D_MODEL), layer),
        _layer_spec((1, D_MODEL), layer),
    ]
    args += [mods, w_up, conv_w, conv_b, w_down, g, b]
    return pl.pallas_call(
        kern,
        out_shape=jax.ShapeDtypeStruct(x.shape, F32),
        grid=(rows // tm,),
        in_specs=in_specs,
        out_specs=pl.BlockSpec((tm, D_MODEL), lambda i: (i, 0)),
        scratch_shapes=[
            pltpu.VMEM((D_MODEL // LANES, tm, LANES), F32),
            pltpu.VMEM((tm + (HALO if halo else 0), D_MODEL), BF16),
            pltpu.VMEM((tm, D_FF), BF16),
        ],
        compiler_params=_params(("arbitrary",)),
        name="conv_ffn",
    )(*args)


def _fourier_kernel(x_ref, mod_ref, cs_ref, dc_ref, ds_ref, wf_ref, g_ref, b_ref, o_ref, y_ref,
                    *, n, tq, rows_per_step):
    j = pl.program_id(1)
    shift, scale, gate = mod_ref[0, 0:1, :], mod_ref[0, 1:2, :], mod_ref[0, 2:3, :]

    @pl.when(j == 0)
    def _():
        for r in range(n // rows_per_step):
            r0 = r * rows_per_step
            h = _modulate(x_ref[r0:r0 + rows_per_step, :], shift, scale).astype(BF16)
            for gi in range(N_FOURIER_GROUPS):
                c0 = gi * FOURIER_GROUP
                y = jnp.dot(h[:, c0:c0 + FOURIER_GROUP], cs_ref[...], preferred_element_type=F32)
                y_ref[r0:r0 + rows_per_step, c0:c0 + FOURIER_GROUP] = y[:, :FOURIER_GROUP].astype(BF16)
                y_ref[n + r0:n + r0 + rows_per_step, c0:c0 + FOURIER_GROUP] = y[:, FOURIER_GROUP:].astype(BF16)

    f = (jnp.dot(dc_ref[...], y_ref[0:n, :], preferred_element_type=F32)
         + jnp.dot(ds_ref[...], y_ref[n:, :], preferred_element_type=F32))
    m = jnp.dot(f.astype(BF16), wf_ref[...], preferred_element_type=F32)
    x = x_ref[pl.ds(pl.multiple_of(j * tq, tq), tq), :]
    o_ref[...] = _post_norm(x, gate * m, g_ref[...], b_ref[...])


def _fourier(x, mods, layer, seq_len, mod_row, cs, dft, w_f, g, b, tq):
    rows = x.shape[0]
    n_seq = rows // seq_len
    steps = seq_len // tq
    kern = functools.partial(_fourier_kernel, n=seq_len, tq=tq, rows_per_step=min(seq_len, 512))
    return pl.pallas_call(
        kern,
        out_shape=jax.ShapeDtypeStruct(x.shape, F32),
        grid=(n_seq, steps),
        in_specs=[
            pl.BlockSpec((seq_len, D_MODEL), lambda s, j: (s, 0)),
            pl.BlockSpec((1, 6, D_MODEL), lambda s, j: (layer * MOD_ROWS + mod_row(s), 0, 0)),
            _const_spec((FOURIER_GROUP, 2 * FOURIER_GROUP)),
            pl.BlockSpec((tq, seq_len), lambda s, j: (j, 0)),
            pl.BlockSpec((tq, seq_len), lambda s, j: (j, 0)),
            _layer_spec((D_MODEL, D_MODEL), layer // 2),
            _layer_spec((1, D_MODEL), layer),
            _layer_spec((1, D_MODEL), layer),
        ],
        out_specs=pl.BlockSpec((tq, D_MODEL), lambda s, j: (s * steps + j, 0)),
        scratch_shapes=[pltpu.VMEM((2 * seq_len, D_MODEL), BF16)],
        compiler_params=_params(("arbitrary", "arbitrary")),
        name="fourier_mix",
    )(x, mods, cs, dft[0], dft[1], w_f, g, b)


def _cos_sin(k, t, n):
    ang = ((k * t) % n).astype(F32) * (2.0 * math.pi / n)
    return jnp.cos(ang), jnp.sin(ang)


def _dft_table_kernel(ca_ref, sa_ref, cb_ref, sb_ref, c_ref, s_ref, *, n):
    cb, sb = cb_ref[...], sb_ref[...]
    norm = n ** -0.5
    for i in range(n // LANES):
        ca, sa = ca_ref[:, i:i + 1], sa_ref[:, i:i + 1]
        c_ref[:, i * LANES:(i + 1) * LANES] = ((ca * cb - sa * sb) * norm).astype(BF16)
        s_ref[:, i * LANES:(i + 1) * LANES] = ((sa * cb + ca * sb) * -norm).astype(BF16)


def _dft_table(n):
    n_hi = n // LANES
    k = lax.broadcasted_iota(jnp.int32, (n, n_hi), 0)
    t_hi = lax.broadcasted_iota(jnp.int32, (n, n_hi), 1) * LANES
    ca, sa = _cos_sin(k, t_hi, n)
    k = lax.broadcasted_iota(jnp.int32, (n, LANES), 0)
    t_lo = lax.broadcasted_iota(jnp.int32, (n, LANES), 1)
    cb, sb = _cos_sin(k, t_lo, n)
    tr = min(n, 256)
    hi_spec = pl.BlockSpec((tr, n_hi), lambda i: (i, 0))
    lo_spec = pl.BlockSpec((tr, LANES), lambda i: (i, 0))
    out_spec = pl.BlockSpec((tr, n), lambda i: (i, 0))
    table = jax.ShapeDtypeStruct((n, n), BF16)
    return pl.pallas_call(
        functools.partial(_dft_table_kernel, n=n),
        out_shape=[table, table],
        grid=(n // tr,),
        in_specs=[hi_spec, hi_spec, lo_spec, lo_spec],
        out_specs=[out_spec, out_spec],
        compiler_params=_params(("arbitrary",)),
        name="dft_table",
    )(ca, sa, cb, sb)


def _channel_dft_table():
    n = FOURIER_GROUP
    k = lax.broadcasted_iota(jnp.int32, (n, n), 0)
    t = lax.broadcasted_iota(jnp.int32, (n, n), 1)
    cos, sin = _cos_sin(k, t, n)
    return (jnp.concatenate([cos, sin], axis=1) * (n ** -0.5)).astype(BF16)


def _qkv_kernel(*refs, tm, rope, cache, n_prev):
    x_ref, mod_ref, w_ref = refs[:3]
    refs = refs[3:]
    if rope:
        cos_ref, sin_ref = refs[:2]
        refs = refs[2:]
    if n_prev:
        prev_k_ref, prev_v_ref = refs[:2]
        refs = refs[2:]
    q0_ref, q1_ref, k_ref, v_ref = refs[:4]
    if cache:
        kc_ref, vc_ref = refs[4:6]
    if n_prev:
        kc_ref[:, 0:n_prev] = prev_k_ref[...]
        vc_ref[:, 0:n_prev] = prev_v_ref[...]
    shift, scale = mod_ref[0, 0:1, :], mod_ref[0, 1:2, :]
    h = _modulate(x_ref[...], shift, scale).astype(BF16)
    lane = lax.broadcasted_iota(jnp.int32, (tm, QKV_CHUNK), 1)
    first_half = (lane % V_DIM) < HEAD_DIM
    if rope:
        reps = QKV_CHUNK // V_DIM
        cos = jnp.concatenate([cos_ref[...]] * reps, axis=1)
        sin = jnp.concatenate([sin_ref[...]] * reps, axis=1)
        swap_up = (lane % (2 * N_FREQ)) < N_FREQ

    def rotary(y):
        up = jnp.concatenate([pltpu.roll(y[:, c:c + LANES], LANES - N_FREQ, 1)
                              for c in range(0, QKV_CHUNK, LANES)], axis=1)
        down = jnp.concatenate([pltpu.roll(y[:, c:c + LANES], N_FREQ, 1)
                                for c in range(0, QKV_CHUNK, LANES)], axis=1)
        return y * cos + jnp.where(swap_up, up, down) * sin

    def proj(col):
        return jnp.dot(h, w_ref[:, col:col + QKV_CHUNK], preferred_element_type=F32)

    for c0 in range(0, D_MODEL, QKV_CHUNK):
        cols = slice(c0, c0 + QKV_CHUNK)
        q, k, v = proj(c0), proj(D_MODEL + c0), proj(2 * D_MODEL + c0)
        if rope:
            q, k = rotary(q), rotary(k)
        q = q * Q_SCALE
        q0_ref[:, cols] = jnp.where(first_half, q, 0.0).astype(BF16)
        q1_ref[:, cols] = jnp.where(first_half, 0.0, q).astype(BF16)
        k_ref[:, cols] = k.astype(BF16)
        v_ref[:, cols] = v.astype(BF16)
        if cache:
            kc_ref[:, n_prev, :, cols] = k.reshape(kc_ref.shape[0], kc_ref.shape[2], QKV_CHUNK)
            vc_ref[:, n_prev, :, cols] = v.reshape(vc_ref.shape[0], vc_ref.shape[2], QKV_CHUNK)


def _qkv(x, mods, layer, seq_len, mod_row, w_qkv, tm, rope_tables=None, cache=None):
    rows = x.shape[0]
    rope = rope_tables is not None
    with_cache = cache is not None
    n_prev = cache[0].shape[1] if cache else 0
    tiles_per_seq = max(seq_len // tm, 1)
    in_specs = [
        pl.BlockSpec((tm, D_MODEL), lambda i: (i, 0)),
        pl.BlockSpec((1, 6, D_MODEL), lambda i: (layer * MOD_ROWS + mod_row(i), 0, 0)),
        _layer_spec((D_MODEL, 3 * D_MODEL), layer // 2),
    ]
    args = [x, mods, w_qkv]
    if rope:
        in_specs += [pl.BlockSpec((tm, V_DIM), lambda i: (i % tiles_per_seq, 0))] * 2
        args += list(rope_tables)
    act = jax.ShapeDtypeStruct((rows, D_MODEL), BF16)
    out_shape = [act] * 4
    out_specs = [pl.BlockSpec((tm, D_MODEL), lambda i: (i, 0))] * 4
    if with_cache:
        seqs = tm // seq_len
        n_seq = rows // seq_len
        if n_prev:
            in_specs += [pl.BlockSpec((seqs, n_prev, seq_len, D_MODEL), lambda i: (i, 0, 0, 0))] * 2
            args += list(cache)
        grown = jax.ShapeDtypeStruct((n_seq, n_prev + 1, seq_len, D_MODEL), F32)
        out_shape += [grown, grown]
        out_specs += [pl.BlockSpec((seqs, n_prev + 1, seq_len, D_MODEL), lambda i: (i, 0, 0, 0))] * 2
    kern = functools.partial(_qkv_kernel, tm=tm, rope=rope, cache=with_cache, n_prev=n_prev)
    return pl.pallas_call(
        kern,
        out_shape=out_shape,
        grid=(rows // tm,),
        in_specs=in_specs,
        out_specs=out_specs,
        compiler_params=_params(("arbitrary",)),
        name="qkv_proj",
    )(*args)


def _attn_kernel(*refs, hb, tq, lam_init, cache, lockstep):
    lam_ref, g_ref, q0_ref, q1_ref, k_ref, v_ref = refs[:6]
    if cache:
        ck_ref, cv_ref = refs[6:8]
    o_ref = refs[-1]
    lv = lam_ref[...]
    lam = (jnp.exp(jnp.sum(lv[0:1] * lv[1:2], axis=-1, keepdims=True))
           - jnp.exp(jnp.sum(lv[2:3] * lv[3:4], axis=-1, keepdims=True)) + lam_init)
    nt = (((1,), (1,)), ((), ()))

    def scores(hh):
        cols = slice(hh * V_DIM, (hh + 1) * V_DIM)
        qq = jnp.concatenate([q0_ref[:, cols], q1_ref[:, cols]], axis=0)
        s = lax.dot_general(qq, k_ref[:, cols], nt, preferred_element_type=F32)
        sc = None
        if cache:
            sc = lax.dot_general(qq, ck_ref[:, cols].astype(BF16), nt, preferred_element_type=F32)
        return s, sc

    def row_max(s, sc):
        m = jnp.max(s, axis=-1, keepdims=True)
        return jnp.maximum(m, jnp.max(sc, axis=-1, keepdims=True)) if cache else m

    def exponentials(s, sc, m):
        return jnp.exp2(s - m), (jnp.exp2(sc - m) if cache else None)

    def row_sum(p, pc):
        l = jnp.sum(p, axis=-1, keepdims=True)
        return l + jnp.sum(pc, axis=-1, keepdims=True) if cache else l

    def weighted_values(hh, p, pc, l):
        cols = slice(hh * V_DIM, (hh + 1) * V_DIM)
        ratio = l[:tq] * lam / l[tq:]
        w = (p[:tq] - p[tq:] * ratio).astype(BF16)
        o = jnp.dot(w, v_ref[:, cols], preferred_element_type=F32)
        if cache:
            wc = (pc[:tq] - pc[tq:] * ratio).astype(BF16)
            o = o + jnp.dot(wc, cv_ref[:, cols].astype(BF16), preferred_element_type=F32)
        return o / l[:tq]

    ones_col = (lax.broadcasted_iota(jnp.int32, (1, V_DIM), 1) == 0).astype(BF16)

    def with_ones(v):
        return jnp.concatenate([v, jnp.broadcast_to(ones_col, v.shape)], axis=1)

    def weighted_values_unnormalised(hh, p, pc):
        cols = slice(hh * V_DIM, (hh + 1) * V_DIM)
        o = jnp.dot(p.astype(BF16), with_ones(v_ref[:, cols]), preferred_element_type=F32)
        if cache:
            o = o + jnp.dot(pc.astype(BF16), with_ones(cv_ref[:, cols].astype(BF16)),
                            preferred_element_type=F32)
        o = o[:, :V_DIM] / o[:, V_DIM:V_DIM + 1]
        return o[:tq] - lam * o[tq:]

    def sub_norm(o):
        return lax.rsqrt(jnp.mean(o * o, axis=-1, keepdims=True) + SUBLN_EPS)

    def emit(hh, o, r):
        o_ref[:, hh * V_DIM:(hh + 1) * V_DIM] = (o * r * g_ref[...] * (1.0 - lam_init)).astype(BF16)

    heads = range(hb)
    if lockstep:
        S = [scores(hh) for hh in heads]
        M = [row_max(*S[hh]) for hh in heads]
        P = [exponentials(*S[hh], M[hh]) for hh in heads]
        L = [row_sum(*P[hh]) for hh in heads]
        O = [weighted_values(hh, *P[hh], L[hh]) for hh in heads]
        R = [sub_norm(O[hh]) for hh in heads]
        for hh in heads:
            emit(hh, O[hh], R[hh])
    else:
        pending = [scores(hh) for hh in range(min(SCORE_LOOKAHEAD, hb))]
        for hh in heads:
            if hh + SCORE_LOOKAHEAD < hb:
                pending.append(scores(hh + SCORE_LOOKAHEAD))
            s, sc = pending.pop(0)
            o = weighted_values_unnormalised(hh, *exponentials(s, sc, row_max(s, sc)))
            emit(hh, o, sub_norm(o))


def _attention(q0, q1, k, v, lam_vecs, subln_g, lam_init, seq_len, tq, hb, cache=None):
    rows = q0.shape[0]
    n_seq = rows // seq_len
    q_steps = seq_len // tq
    wb = hb * V_DIM
    with_cache = cache is not None
    q_spec = pl.BlockSpec((tq, wb), lambda s, h, j: (s * q_steps + j, h))
    kv_spec = pl.BlockSpec((seq_len, wb), lambda s, h, j: (s, h))
    in_specs = [_const_spec((4, HEAD_DIM)), _const_spec((1, V_DIM)), q_spec, q_spec, kv_spec, kv_spec]
    args = [lam_vecs, subln_g, q0, q1, k, v]
    if with_cache:
        ck, cv, slot = cache
        past = ck.shape[2]
        c_spec = pl.BlockSpec((None, None, past, wb), lambda s, h, j: (s, slot, 0, h))
        in_specs += [c_spec, c_spec]
        args += [ck, cv]
    n_keys = seq_len + (cache[0].shape[2] if with_cache else 0)
    lockstep = hb * 2 * tq * n_keys * 4 <= LOCKSTEP_SCORE_BYTES
    kern = functools.partial(_attn_kernel, hb=hb, tq=tq, lam_init=lam_init, cache=with_cache,
                             lockstep=lockstep)
    return pl.pallas_call(
        kern,
        out_shape=jax.ShapeDtypeStruct((rows, D_MODEL), BF16),
        grid=(n_seq, N_HEADS // hb, q_steps),
        in_specs=in_specs,
        out_specs=q_spec,
        compiler_params=_params(("arbitrary", "arbitrary", "arbitrary")),
        name="diff_attention",
    )(*args)


def _rope_tables(n):
    token = lax.broadcasted_iota(jnp.int32, (n, N_FREQ), 0)
    row = (token // GRID_W).astype(F32)
    col = (token % GRID_W).astype(F32)
    inv = 1.0 / (ROPE_THETA ** (jnp.arange(N_FREQ, dtype=F32) / N_FREQ))
    ar = row * inv
    ac = col * inv
    ang = jnp.concatenate([ar, ar, ac, ac], axis=-1)
    sign = jnp.tile(jnp.concatenate([-jnp.ones((N_FREQ,), F32), jnp.ones((N_FREQ,), F32)]), 2)
    cos = jnp.cos(ang)
    sin = jnp.sin(ang) * sign
    return jnp.tile(cos, (1, 2)), jnp.tile(sin, (1, 2))


def _tiles(ctx_len, lat_len):
    return dict(
        ffn_ctx=2 * ctx_len, ffn_lat=512,
        row_ctx=512, row_lat=512,
        fourier_ctx=ctx_len, fourier_lat=512,
        attn_q_ctx=ctx_len, attn_q_lat=256,
        attn_heads_ctx=N_HEADS, attn_heads_lat=N_HEADS,
    )


def kernel(x_prompt, x_sample, cache_k, cache_v, c, c_ctx, w_ada, b_ada, w_fourier, w_qkv,
           lambda_q1, lambda_k1, lambda_q2, lambda_k2, subln_g, w_o, w_up, conv_w, conv_b,
           w_down, ln1_g, ln1_b, ln2_g, ln2_b):
    n_ctx_seq, ctx_len, d = x_prompt.shape
    n_lat_seq, lat_len, _ = x_sample.shape
    past = cache_k.shape[2]
    assert d == D_MODEL and n_lat_seq + 1 <= MOD_ROWS
    t = _tiles(ctx_len, lat_len)

    cvec = jnp.concatenate(
        [c_ctx[None, :], c, jnp.zeros((MOD_ROWS - 1 - n_lat_seq, d), F32)], axis=0)
    mods = _ada_table(cvec, w_ada, b_ada)

    xc = x_prompt.reshape(n_ctx_seq * ctx_len, d)
    xl = x_sample.reshape(n_lat_seq * lat_len, d)
    ck_in = cache_k.reshape(n_lat_seq, DEPTH // 2, past, d)
    cv_in = cache_v.reshape(n_lat_seq, DEPTH // 2, past, d)

    ctx_row = lambda i: 0
    lat_row_of = lambda tile: (lambda i: 1 + i // (lat_len // tile))
    cs = _channel_dft_table()
    dft_ctx, dft_lat = _dft_table(ctx_len), _dft_table(lat_len)
    rope = _rope_tables(lat_len)
    caches = ()

    wf, wqkv, wo = w_fourier.astype(BF16), w_qkv.astype(BF16), w_o.astype(BF16)
    wup, wdn = w_up.astype(BF16), w_down.astype(BF16)
    cw, cb = conv_w, conv_b[:, None, :]
    g1, b1 = ln1_g[:, None, :], ln1_b[:, None, :]
    g2, b2 = ln2_g[:, None, :], ln2_b[:, None, :]

    for i in range(DEPTH):
        j = i // 2
        attn_c = attn_l = None
        if i % 2 == 0:
            xc = _fourier(xc, mods, i, ctx_len, ctx_row, cs, dft_ctx, wf, g1, b1, tq=t["fourier_ctx"])
            xl = _fourier(xl, mods, i, lat_len, lambda s: 1 + s, cs, dft_lat, wf, g1, b1,
                          tq=t["fourier_lat"])
        else:
            lam_init = 0.8 - 0.6 * math.exp(-0.3 * i)
            lam_vecs = jnp.stack([lambda_q1[j], lambda_k1[j], lambda_q2[j], lambda_k2[j]])
            sg = subln_g[j][None, :]
            q0, q1, k, v, new_k, new_v = _qkv(
                xc, mods, i, ctx_len, ctx_row, wqkv, tm=t["row_ctx"], cache=caches)
            caches = (new_k, new_v)
            oc = _attention(q0, q1, k, v, lam_vecs, sg, lam_init, ctx_len,
                            tq=t["attn_q_ctx"], hb=t["attn_heads_ctx"])
            lat_row = lat_row_of(t["row_lat"])
            q0, q1, k, v = _qkv(xl, mods, i, lat_len, lat_row, wqkv, tm=t["row_lat"], rope_tables=rope)
            ol = _attention(q0, q1, k, v, lam_vecs, sg, lam_init, lat_len,
                            tq=t["attn_q_lat"], hb=t["attn_heads_lat"], cache=(ck_in, cv_in, j))
            attn_c, attn_l = (oc, wo, g1, b1), (ol, wo, g1, b1)
        xc = _ffn(xc, mods, i, ctx_len, ctx_row, wup, cw, cb, wdn, g2, b2, tm=t["ffn_ctx"], attn=attn_c)
        xl = _ffn(xl, mods, i, lat_len, lat_row_of(t["ffn_lat"]), wup, cw, cb, wdn, g2, b2,
                  tm=t["ffn_lat"], attn=attn_l)

    y_prompt = xc.reshape(x_prompt.shape)
    y_sample = xl.reshape(x_sample.shape)
    new_cache_k = new_k.reshape(n_ctx_seq, DEPTH // 2, ctx_len, N_HEADS, 2, HEAD_DIM)
    new_cache_v = new_v.reshape(n_ctx_seq, DEPTH // 2, ctx_len, N_HEADS, V_DIM)
    return (y_prompt, y_sample, new_cache_k, new_cache_v)
```

```python
import functools
import math

import jax
import jax.numpy as jnp
from jax import lax
from jax.experimental import pallas as pl
from jax.experimental.pallas import tpu as pltpu

F32 = jnp.float32
BF16 = jnp.bfloat16

D_MODEL = 1024
DEPTH = 4
GRID_W = 64
N_HEADS = 8
HEAD_DIM = 64
V_DIM = 2 * HEAD_DIM
N_FOURIER_GROUPS = 4
FOURIER_GROUP = D_MODEL // N_FOURIER_GROUPS
D_FF = 2816
ROPE_THETA = 10000.0
N_FREQ = HEAD_DIM // 4
DN_ALPHA = (2 * DEPTH) ** 0.25
LN_EPS = 1e-6
SUBLN_EPS = 1e-5

LANES = 128
SUBLANES = 8
MOD_ROWS = 8
FF_CHUNK = 256
HALO = 16
QKV_CHUNK = 256
VMEM_LIMIT = 56 * 1024 * 1024
LOCKSTEP_SCORE_BYTES = 8 * 1024 * 1024
SCORE_LOOKAHEAD = 2
Q_SCALE = HEAD_DIM ** -0.5 * math.log2(math.e)


def _params(sem, vmem=VMEM_LIMIT):
    return pltpu.CompilerParams(dimension_semantics=sem, vmem_limit_bytes=vmem)


def _ln(x):
    mu = jnp.mean(x, axis=-1, keepdims=True)
    xc = x - mu
    var = jnp.mean(xc * xc, axis=-1, keepdims=True)
    return xc * lax.rsqrt(var + LN_EPS)


def _modulate(x, shift, scale):
    return _ln(x) * (1.0 + scale) + shift


def _post_norm(x, update, g, b):
    return _ln(DN_ALPHA * x + update) * g + b


def _silu(x):
    return x / (1.0 + jnp.exp(-x))


def _const_spec(shape):
    return pl.BlockSpec(shape, lambda *_: (0,) * len(shape))


def _layer_spec(shape, layer):
    return pl.BlockSpec((None,) + tuple(shape), lambda *_: (layer,) + (0,) * len(shape))


def _ada_kernel(c_ref, w_ref, b_ref, o_ref):
    s = _silu(c_ref[...]).astype(BF16)
    a = jnp.dot(s, w_ref[0].astype(BF16), preferred_element_type=F32)
    o_ref[0] = a + b_ref[0]


def _ada_table(cvec, w_ada, b_ada):
    nc = D_MODEL
    n_col = w_ada.shape[-1] // nc
    out = pl.pallas_call(
        _ada_kernel,
        out_shape=jax.ShapeDtypeStruct((DEPTH, MOD_ROWS, w_ada.shape[-1]), F32),
        grid=(DEPTH, n_col),
        in_specs=[
            pl.BlockSpec((MOD_ROWS, D_MODEL), lambda l, j: (0, 0)),
            pl.BlockSpec((1, D_MODEL, nc), lambda l, j: (l, 0, j)),
            pl.BlockSpec((1, 1, nc), lambda l, j: (l, 0, j)),
        ],
        out_specs=pl.BlockSpec((1, MOD_ROWS, nc), lambda l, j: (l, 0, j)),
        compiler_params=_params(("arbitrary", "arbitrary")),
        name="ada_table",
    )(cvec, w_ada, b_ada.reshape(DEPTH, 1, -1))
    return out.reshape(DEPTH * MOD_ROWS, 6, D_MODEL)


def _ffn_kernel(*refs, tm, seq_len, attn_proj):
    halo = seq_len > tm
    x_ref = refs[0]
    if halo:
        xp_ref, xn_ref = refs[1:3]
        refs = refs[2:]
    if attn_proj:
        a_ref = refs[1]
        refs = refs[1:]
        if halo:
            ap_ref, an_ref = refs[1:3]
            refs = refs[2:]
        wo_ref, g1_ref, b1_ref = refs[1:4]
        refs = refs[3:]
    (mod_ref, wup_ref, cw_ref, cb_ref, wdn_ref, g_ref, b_ref,
     o_ref, slab_ref, hext_ref, act_ref) = refs[1:]
    seg = tm // SUBLANES
    n_slab = D_MODEL // LANES
    shift, scale, gate = mod_ref[0, 3:4, :], mod_ref[0, 4:5, :], mod_ref[0, 5:6, :]
    x = x_ref[...]
    if halo:
        xp, xn = xp_ref[...], xn_ref[...]
    if attn_proj:
        gate1 = mod_ref[0, 2:3, :]
        a = a_ref[...]
        if halo:
            a = jnp.concatenate([a, ap_ref[...], an_ref[...]], axis=0)
        m = jnp.dot(a, wo_ref[...], preferred_element_type=F32)
        x = _post_norm(x, gate1 * m[0:tm], g1_ref[...], b1_ref[...])
        if halo:
            xp = _post_norm(xp, gate1 * m[tm + 8:tm + 16], g1_ref[...], b1_ref[...])
            xn = _post_norm(xn, gate1 * m[tm + 16:tm + 24], g1_ref[...], b1_ref[...])
    h = _modulate(x, shift, scale)
    for cb in range(n_slab):
        for s in range(SUBLANES):
            slab_ref[cb, pl.ds(s, seg, stride=SUBLANES), :] = (
                h[s * seg:(s + 1) * seg, cb * LANES:(cb + 1) * LANES])
    for cb in range(n_slab):
        hext_ref[0:tm, cb * LANES:(cb + 1) * LANES] = slab_ref[cb].astype(BF16)
    row = lax.broadcasted_iota(jnp.int32, (SUBLANES, FF_CHUNK), 0)
    if halo:
        tiles_per_seq = seq_len // tm
        pos = pl.program_id(0) % tiles_per_seq
        hp = jnp.where(pos > 0, _modulate(xp, shift, scale), 0.0)
        hn = jnp.where(pos < tiles_per_seq - 1, _modulate(xn, shift, scale), 0.0)
        hext_ref[tm:, :] = jnp.concatenate([hp, hn], axis=0).astype(BF16)
    else:
        assert tm % seq_len == 0 and seq_len % seg == 0
        seq_start = functools.reduce(jnp.logical_or, [row == s for s in range(SUBLANES) if (s * seg) % seq_len == 0])
        seq_end = functools.reduce(jnp.logical_or, [row == s for s in range(SUBLANES) if ((s + 1) * seg) % seq_len == 0])
    hext = hext_ref[...]

    def conv(u, col):
        cw = cw_ref[:, col:col + FF_CHUNK]
        first, last = pltpu.roll(u[tm - 8:tm], 1, 0), pltpu.roll(u[0:8], 7, 0)
        if halo:
            first = jnp.where(row == 0, u[tm + 7:tm + 8], first)
            last = jnp.where(row == 7, u[tm + 8:tm + 9], last)
        else:
            first = jnp.where(seq_start, 0.0, first)
            last = jnp.where(seq_end, 0.0, last)
        prev = jnp.concatenate([first, u[0:tm - 8]], axis=0)
        nxt = jnp.concatenate([u[8:tm], last], axis=0)
        return prev * cw[0:1] + u[0:tm] * cw[1:2] + nxt * cw[2:3] + cb_ref[:, col:col + FF_CHUNK]

    for c in range(D_FF // FF_CHUNK):
        ca, cg = c * FF_CHUNK, D_FF + c * FF_CHUNK
        ua = jnp.dot(hext, wup_ref[:, ca:ca + FF_CHUNK], preferred_element_type=F32)
        ug = jnp.dot(hext, wup_ref[:, cg:cg + FF_CHUNK], preferred_element_type=F32)
        act_ref[:, ca:ca + FF_CHUNK] = (_silu(conv(ua, ca)) * conv(ug, cg)).astype(BF16)
    f = jnp.dot(act_ref[...], wdn_ref[...], preferred_element_type=F32)
    for cb in range(n_slab):
        slab_ref[cb] = f[:, cb * LANES:(cb + 1) * LANES]
    for cb in range(n_slab):
        for s in range(SUBLANES):
            o_ref[s * seg:(s + 1) * seg, cb * LANES:(cb + 1) * LANES] = (
                slab_ref[cb, pl.ds(s, seg, stride=SUBLANES), :])
    o_ref[...] = _post_norm(x, gate * o_ref[...], g_ref[...], b_ref[...])


def _ffn(x, mods, layer, seq_len, mod_row, w_up, conv_w, conv_b, w_down, g, b, tm, attn=None):
    rows = x.shape[0]
    halo = seq_len > tm
    kern = functools.partial(_ffn_kernel, tm=tm, seq_len=seq_len, attn_proj=attn is not None)

    def halo_specs(block_rows):
        per_tile = tm // block_rows
        return [
            pl.BlockSpec((block_rows, D_MODEL), lambda i: (jnp.maximum(i * per_tile - 1, 0), 0)),
            pl.BlockSpec((block_rows, D_MODEL),
                         lambda i: (jnp.minimum((i + 1) * per_tile, rows // block_rows - 1), 0)),
        ]

    in_specs = [pl.BlockSpec((tm, D_MODEL), lambda i: (i, 0))]
    args = [x]
    if halo:
        in_specs += halo_specs(SUBLANES)
        args += [x, x]
    if attn is not None:
        a, w_o, g1, b1 = attn
        in_specs += [pl.BlockSpec((tm, D_MODEL), lambda i: (i, 0))]
        args += [a]
        if halo:
            in_specs += halo_specs(HALO)
            args += [a, a]
        in_specs += [_layer_spec((D_MODEL, D_MODEL), layer // 2),
                     _layer_spec((1, D_MODEL), layer), _layer_spec((1, D_MODEL), layer)]
        args += [w_o, g1, b1]
    in_specs += [
        pl.BlockSpec((1, 6, D_MODEL), lambda i: (layer * MOD_ROWS + mod_row(i), 0, 0)),
        _layer_spec((D_MODEL, 2 * D_FF), layer),
        _layer_spec((3, 2 * D_FF), layer),
        _layer_spec((1, 2 * D_FF), layer),
        _layer_spec((D_FF, D_MODEL), layer),
        _layer_spec((1, D_MODEL), layer),
        _layer_spec((1, D_MODEL), layer),
    ]
    args += [mods, w_up, conv_w, conv_b, w_down, g, b]
    return pl.pallas_call(
        kern,
        out_shape=jax.ShapeDtypeStruct(x.shape, F32),
        grid=(rows // tm,),
        in_specs=in_specs,
        out_specs=pl.BlockSpec((tm, D_MODEL), lambda i: (i, 0)),
        scratch_shapes=[
            pltpu.VMEM((D_MODEL // LANES, tm, LANES), F32),
            pltpu.VMEM((tm + (HALO if halo else 0), D_MODEL), BF16),
            pltpu.VMEM((tm, D_FF), BF16),
        ],
        compiler_params=_params(("arbitrary",)),
        name="conv_ffn",
    )(*args)


def _fourier_kernel(x_ref, mod_ref, cs_ref, dc_ref, ds_ref, wf_ref, g_ref, b_ref, o_ref, y_ref,
                    *, n, tq, rows_per_step):
    j = pl.program_id(1)
    shift, scale, gate = mod_ref[0, 0:1, :], mod_ref[0, 1:2, :], mod_ref[0, 2:3, :]

    @pl.when(j == 0)
    def _():
        for r in range(n // rows_per_step):
            r0 = r * rows_per_step
            h = _modulate(x_ref[r0:r0 + rows_per_step, :], shift, scale).astype(BF16)
            for gi in range(N_FOURIER_GROUPS):
                c0 = gi * FOURIER_GROUP
                y = jnp.dot(h[:, c0:c0 + FOURIER_GROUP], cs_ref[...], preferred_element_type=F32)
                y_ref[r0:r0 + rows_per_step, c0:c0 + FOURIER_GROUP] = y[:, :FOURIER_GROUP].astype(BF16)
                y_ref[n + r0:n + r0 + rows_per_step, c0:c0 + FOURIER_GROUP] = y[:, FOURIER_GROUP:].astype(BF16)

    f = (jnp.dot(dc_ref[...], y_ref[0:n, :], preferred_element_type=F32)
         + jnp.dot(ds_ref[...], y_ref[n:, :], preferred_element_type=F32))
    m = jnp.dot(f.astype(BF16), wf_ref[...], preferred_element_type=F32)
    x = x_ref[pl.ds(pl.multiple_of(j * tq, tq), tq), :]
    o_ref[...] = _post_norm(x, gate * m, g_ref[...], b_ref[...])


def _fourier(x, mods, layer, seq_len, mod_row, cs, dft, w_f, g, b, tq):
    rows = x.shape[0]
    n_seq = rows // seq_len
    steps = seq_len // tq
    kern = functools.partial(_fourier_kernel, n=seq_len, tq=tq, rows_per_step=min(seq_len, 512))
    return pl.pallas_call(
        kern,
        out_shape=jax.ShapeDtypeStruct(x.shape, F32),
        grid=(n_seq, steps),
        in_specs=[
            pl.BlockSpec((seq_len, D_MODEL), lambda s, j: (s, 0)),
            pl.BlockSpec((1, 6, D_MODEL), lambda s, j: (layer * MOD_ROWS + mod_row(s), 0, 0)),
            _const_spec((FOURIER_GROUP, 2 * FOURIER_GROUP)),
            pl.BlockSpec((tq, seq_len), lambda s, j: (j, 0)),
            pl.BlockSpec((tq, seq_len), lambda s, j: (j, 0)),
            _layer_spec((D_MODEL, D_MODEL), layer // 2),
            _layer_spec((1, D_MODEL), layer),
            _layer_spec((1, D_MODEL), layer),
        ],
        out_specs=pl.BlockSpec((tq, D_MODEL), lambda s, j: (s * steps + j, 0)),
        scratch_shapes=[pltpu.VMEM((2 * seq_len, D_MODEL), BF16)],
        compiler_params=_params(("arbitrary", "arbitrary")),
        name="fourier_mix",
    )(x, mods, cs, dft[0], dft[1], w_f, g, b)


def _cos_sin(k, t, n):
    ang = ((k * t) % n).astype(F32) * (2.0 * math.pi / n)
    return jnp.cos(ang), jnp.sin(ang)


def _dft_table_kernel(ca_ref, sa_ref, cb_ref, sb_ref, c_ref, s_ref, *, n):
    cb, sb = cb_ref[...], sb_ref[...]
    norm = n ** -0.5
    for i in range(n // LANES):
        ca, sa = ca_ref[:, i:i + 1], sa_ref[:, i:i + 1]
        c_ref[:, i * LANES:(i + 1) * LANES] = ((ca * cb - sa * sb) * norm).astype(BF16)
        s_ref[:, i * LANES:(i + 1) * LANES] = ((sa * cb + ca * sb) * -norm).astype(BF16)


def _dft_table(n):
    n_hi = n // LANES
    k = lax.broadcasted_iota(jnp.int32, (n, n_hi), 0)
    t_hi = lax.broadcasted_iota(jnp.int32, (n, n_hi), 1) * LANES
    ca, sa = _cos_sin(k, t_hi, n)
    k = lax.broadcasted_iota(jnp.int32, (n, LANES), 0)
    t_lo = lax.broadcasted_iota(jnp.int32, (n, LANES), 1)
    cb, sb = _cos_sin(k, t_lo, n)
    tr = min(n, 256)
    hi_spec = pl.BlockSpec((tr, n_hi), lambda i: (i, 0))
    lo_spec = pl.BlockSpec((tr, LANES), lambda i: (i, 0))
    out_spec = pl.BlockSpec((tr, n), lambda i: (i, 0))
    table = jax.ShapeDtypeStruct((n, n), BF16)
    return pl.pallas_call(
        functools.partial(_dft_table_kernel, n=n),
        out_shape=[table, table],
        grid=(n // tr,),
        in_specs=[hi_spec, hi_spec, lo_spec, lo_spec],
        out_specs=[out_spec, out_spec],
        compiler_params=_params(("arbitrary",)),
        name="dft_table",
    )(ca, sa, cb, sb)


def _channel_dft_table():
    n = FOURIER_GROUP
    k = lax.broadcasted_iota(jnp.int32, (n, n), 0)
    t = lax.broadcasted_iota(jnp.int32, (n, n), 1)
    cos, sin = _cos_sin(k, t, n)
    return (jnp.concatenate([cos, sin], axis=1) * (n ** -0.5)).astype(BF16)


def _qkv_kernel(*refs, tm, rope, cache):
    x_ref, mod_ref, w_ref = refs[:3]
    refs = refs[3:]
    if rope:
        cos_ref, sin_ref = refs[:2]
        refs = refs[2:]
    if cache == "update":
        refs = refs[2:]
    q0_ref, q1_ref, k_ref, v_ref = refs[:4]
    if cache:
        kc_ref, vc_ref = refs[4:6]
    if cache == "create" and kc_ref.shape[1] > 1:
        kc_ref[:, 1:] = jnp.zeros((kc_ref.shape[0], kc_ref.shape[1] - 1) + kc_ref.shape[2:], F32)
        vc_ref[:, 1:] = jnp.zeros((vc_ref.shape[0], vc_ref.shape[1] - 1) + vc_ref.shape[2:], F32)
    shift, scale = mod_ref[0, 0:1, :], mod_ref[0, 1:2, :]
    h = _modulate(x_ref[...], shift, scale).astype(BF16)
    lane = lax.broadcasted_iota(jnp.int32, (tm, QKV_CHUNK), 1)
    first_half = (lane % V_DIM) < HEAD_DIM
    if rope:
        reps = QKV_CHUNK // V_DIM
        cos = jnp.concatenate([cos_ref[...]] * reps, axis=1)
        sin = jnp.concatenate([sin_ref[...]] * reps, axis=1)
        swap_up = (lane % (2 * N_FREQ)) < N_FREQ

    def rotary(y):
        up = jnp.concatenate([pltpu.roll(y[:, c:c + LANES], LANES - N_FREQ, 1)
                              for c in range(0, QKV_CHUNK, LANES)], axis=1)
        down = jnp.concatenate([pltpu.roll(y[:, c:c + LANES], N_FREQ, 1)
                                for c in range(0, QKV_CHUNK, LANES)], axis=1)
        return y * cos + jnp.where(swap_up, up, down) * sin

    def proj(col):
        return jnp.dot(h, w_ref[:, col:col + QKV_CHUNK], preferred_element_type=F32)

    for c0 in range(0, D_MODEL, QKV_CHUNK):
        cols = slice(c0, c0 + QKV_CHUNK)
        q, k, v = proj(c0), proj(D_MODEL + c0), proj(2 * D_MODEL + c0)
        if rope:
            q, k = rotary(q), rotary(k)
        q = q * Q_SCALE
        q0_ref[:, cols] = jnp.where(first_half, q, 0.0).astype(BF16)
        q1_ref[:, cols] = jnp.where(first_half, 0.0, q).astype(BF16)
        k_ref[:, cols] = k.astype(BF16)
        v_ref[:, cols] = v.astype(BF16)
        if cache:
            kc_ref[:, 0, :, cols] = k.reshape(kc_ref.shape[0], kc_ref.shape[2], QKV_CHUNK)
            vc_ref[:, 0, :, cols] = v.reshape(vc_ref.shape[0], vc_ref.shape[2], QKV_CHUNK)


def _qkv(x, mods, layer, seq_len, mod_row, w_qkv, tm, rope_tables=None, cache=None, slot=0, n_slots=1):
    rows = x.shape[0]
    rope = rope_tables is not None
    with_cache = cache is not None
    tiles_per_seq = max(seq_len // tm, 1)
    in_specs = [
        pl.BlockSpec((tm, D_MODEL), lambda i: (i, 0)),
        pl.BlockSpec((1, 6, D_MODEL), lambda i: (layer * MOD_ROWS + mod_row(i), 0, 0)),
        _layer_spec((D_MODEL, 3 * D_MODEL), layer // 2),
    ]
    args = [x, mods, w_qkv]
    if rope:
        in_specs += [pl.BlockSpec((tm, V_DIM), lambda i: (i % tiles_per_seq, 0))] * 2
        args += list(rope_tables)
    act = jax.ShapeDtypeStruct((rows, D_MODEL), BF16)
    out_shape = [act] * 4
    out_specs = [pl.BlockSpec((tm, D_MODEL), lambda i: (i, 0))] * 4
    aliases, mode = {}, None
    if with_cache:
        seqs = tm // seq_len
        if cache:
            mode = "update"
            in_specs += [pl.BlockSpec(memory_space=pl.ANY)] * 2
            aliases = {len(args): 4, len(args) + 1: 5}
            args += list(cache)
            full = jax.ShapeDtypeStruct(cache[0].shape, F32)
            out_specs += [pl.BlockSpec((seqs, 1, seq_len, D_MODEL), lambda i: (i, slot, 0, 0))] * 2
        else:
            mode = "create"
            full = jax.ShapeDtypeStruct((rows // seq_len, n_slots, seq_len, D_MODEL), F32)
            out_specs += [pl.BlockSpec((seqs, n_slots, seq_len, D_MODEL), lambda i: (i, 0, 0, 0))] * 2
        out_shape += [full, full]
    kern = functools.partial(_qkv_kernel, tm=tm, rope=rope, cache=mode)
    return pl.pallas_call(
        kern,
        out_shape=out_shape,
        grid=(rows // tm,),
        in_specs=in_specs,
        out_specs=out_specs,
        input_output_aliases=aliases,
        compiler_params=_params(("arbitrary",)),
        name="qkv_proj",
    )(*args)


def _attn_kernel(*refs, hb, tq, lam_init, cache, lockstep):
    lam_ref, g_ref, q0_ref, q1_ref, k_ref, v_ref = refs[:6]
    if cache:
        ck_ref, cv_ref = refs[6:8]
    o_ref = refs[-1]
    lv = lam_ref[...]
    lam = (jnp.exp(jnp.sum(lv[0:1] * lv[1:2], axis=-1, keepdims=True))
           - jnp.exp(jnp.sum(lv[2:3] * lv[3:4], axis=-1, keepdims=True)) + lam_init)
    nt = (((1,), (1,)), ((), ()))

    def scores(hh):
        cols = slice(hh * V_DIM, (hh + 1) * V_DIM)
        qq = jnp.concatenate([q0_ref[:, cols], q1_ref[:, cols]], axis=0)
        s = lax.dot_general(qq, k_ref[:, cols], nt, preferred_element_type=F32)
        sc = None
        if cache:
            sc = lax.dot_general(qq, ck_ref[:, cols].astype(BF16), nt, preferred_element_type=F32)
        return s, sc

    def row_max(s, sc):
        m = jnp.max(s, axis=-1, keepdims=True)
        return jnp.maximum(m, jnp.max(sc, axis=-1, keepdims=True)) if cache else m

    def exponentials(s, sc, m):
        return jnp.exp2(s - m), (jnp.exp2(sc - m) if cache else None)

    def row_sum(p, pc):
        l = jnp.sum(p, axis=-1, keepdims=True)
        return l + jnp.sum(pc, axis=-1, keepdims=True) if cache else l

    def weighted_values(hh, p, pc, l):
        cols = slice(hh * V_DIM, (hh + 1) * V_DIM)
        ratio = l[:tq] * lam / l[tq:]
        w = (p[:tq] - p[tq:] * ratio).astype(BF16)
        o = jnp.dot(w, v_ref[:, cols], preferred_element_type=F32)
        if cache:
            wc = (pc[:tq] - pc[tq:] * ratio).astype(BF16)
            o = o + jnp.dot(wc, cv_ref[:, cols].astype(BF16), preferred_element_type=F32)
        return o / l[:tq]

    ones_col = (lax.broadcasted_iota(jnp.int32, (1, V_DIM), 1) == 0).astype(BF16)

    def with_ones(v):
        return jnp.concatenate([v, jnp.broadcast_to(ones_col, v.shape)], axis=1)

    def weighted_values_unnormalised(hh, p, pc):
        cols = slice(hh * V_DIM, (hh + 1) * V_DIM)
        o = jnp.dot(p.astype(BF16), with_ones(v_ref[:, cols]), preferred_element_type=F32)
        if cache:
            o = o + jnp.dot(pc.astype(BF16), with_ones(cv_ref[:, cols].astype(BF16)),
                            preferred_element_type=F32)
        o = o[:, :V_DIM] / o[:, V_DIM:V_DIM + 1]
        return o[:tq] - lam * o[tq:]

    def sub_norm(o):
        return lax.rsqrt(jnp.mean(o * o, axis=-1, keepdims=True) + SUBLN_EPS)

    def emit(hh, o, r):
        o_ref[:, hh * V_DIM:(hh + 1) * V_DIM] = (o * r * g_ref[...] * (1.0 - lam_init)).astype(BF16)

    heads = range(hb)
    if lockstep:
        S = [scores(hh) for hh in heads]
        M = [row_max(*S[hh]) for hh in heads]
        P = [exponentials(*S[hh], M[hh]) for hh in heads]
        L = [row_sum(*P[hh]) for hh in heads]
        O = [weighted_values(hh, *P[hh], L[hh]) for hh in heads]
        R = [sub_norm(O[hh]) for hh in heads]
        for hh in heads:
            emit(hh, O[hh], R[hh])
    else:
        pending = [scores(hh) for hh in range(min(SCORE_LOOKAHEAD, hb))]
        for hh in heads:
            if hh + SCORE_LOOKAHEAD < hb:
                pending.append(scores(hh + SCORE_LOOKAHEAD))
            s, sc = pending.pop(0)
            o = weighted_values_unnormalised(hh, *exponentials(s, sc, row_max(s, sc)))
            emit(hh, o, sub_norm(o))


def _attention(q0, q1, k, v, lam_vecs, subln_g, lam_init, seq_len, tq, hb, cache=None):
    rows = q0.shape[0]
    n_seq = rows // seq_len
    q_steps = seq_len // tq
    wb = hb * V_DIM
    with_cache = cache is not None
    q_spec = pl.BlockSpec((tq, wb), lambda s, h, j: (s * q_steps + j, h))
    kv_spec = pl.BlockSpec((seq_len, wb), lambda s, h, j: (s, h))
    in_specs = [_const_spec((4, HEAD_DIM)), _const_spec((1, V_DIM)), q_spec, q_spec, kv_spec, kv_spec]
    args = [lam_vecs, subln_g, q0, q1, k, v]
    if with_cache:
        ck, cv, slot = cache
        past = ck.shape[2]
        c_spec = pl.BlockSpec((None, None, past, wb), lambda s, h, j: (s, slot, 0, h))
        in_specs += [c_spec, c_spec]
        args += [ck, cv]
    n_keys = seq_len + (cache[0].shape[2] if with_cache else 0)
    lockstep = hb * 2 * tq * n_keys * 4 <= LOCKSTEP_SCORE_BYTES
    kern = functools.partial(_attn_kernel, hb=hb, tq=tq, lam_init=lam_init, cache=with_cache,
                             lockstep=lockstep)
    return pl.pallas_call(
        kern,
        out_shape=jax.ShapeDtypeStruct((rows, D_MODEL), BF16),
        grid=(n_seq, N_HEADS // hb, q_steps),
        in_specs=in_specs,
        out_specs=q_spec,
        compiler_params=_params(("arbitrary", "arbitrary", "arbitrary")),
        name="diff_attention",
    )(*args)


def _rope_tables(n):
    token = lax.broadcasted_iota(jnp.int32, (n, N_FREQ), 0)
    row = (token // GRID_W).astype(F32)
    col = (token % GRID_W).astype(F32)
    inv = 1.0 / (ROPE_THETA ** (jnp.arange(N_FREQ, dtype=F32) / N_FREQ))
    ar = row * inv
    ac = col * inv
    ang = jnp.concatenate([ar, ar, ac, ac], axis=-1)
    sign = jnp.tile(jnp.concatenate([-jnp.ones((N_FREQ,), F32), jnp.ones((N_FREQ,), F32)]), 2)
    cos = jnp.cos(ang)
    sin = jnp.sin(ang) * sign
    return jnp.tile(cos, (1, 2)), jnp.tile(sin, (1, 2))


def _tiles(ctx_len, lat_len):
    return dict(
        ffn_ctx=2 * ctx_len, ffn_lat=512,
        row_ctx=512, row_lat=512,
        fourier_ctx=ctx_len, fourier_lat=512,
        attn_q_ctx=ctx_len, attn_q_lat=256,
        attn_heads_ctx=N_HEADS, attn_heads_lat=N_HEADS,
    )


def kernel(x_prompt, x_sample, cache_k, cache_v, c, c_ctx, w_ada, b_ada, w_fourier, w_qkv,
           lambda_q1, lambda_k1, lambda_q2, lambda_k2, subln_g, w_o, w_up, conv_w, conv_b,
           w_down, ln1_g, ln1_b, ln2_g, ln2_b):
    n_ctx_seq, ctx_len, d = x_prompt.shape
    n_lat_seq, lat_len, _ = x_sample.shape
    past = cache_k.shape[2]
    assert d == D_MODEL and n_lat_seq + 1 <= MOD_ROWS
    t = _tiles(ctx_len, lat_len)

    cvec = jnp.concatenate(
        [c_ctx[None, :], c, jnp.zeros((MOD_ROWS - 1 - n_lat_seq, d), F32)], axis=0)
    mods = _ada_table(cvec, w_ada, b_ada)

    xc = x_prompt.reshape(n_ctx_seq * ctx_len, d)
    xl = x_sample.reshape(n_lat_seq * lat_len, d)
    ck_in = cache_k.reshape(n_lat_seq, DEPTH // 2, past, d)
    cv_in = cache_v.reshape(n_lat_seq, DEPTH // 2, past, d)

    ctx_row = lambda i: 0
    lat_row_of = lambda tile: (lambda i: 1 + i // (lat_len // tile))
    cs = _channel_dft_table()
    dft_ctx, dft_lat = _dft_table(ctx_len), _dft_table(lat_len)
    rope = _rope_tables(lat_len)
    caches = ()

    wf, wqkv, wo = w_fourier.astype(BF16), w_qkv.astype(BF16), w_o.astype(BF16)
    wup, wdn = w_up.astype(BF16), w_down.astype(BF16)
    cw, cb = conv_w, conv_b[:, None, :]
    g1, b1 = ln1_g[:, None, :], ln1_b[:, None, :]
    g2, b2 = ln2_g[:, None, :], ln2_b[:, None, :]

    for i in range(DEPTH):
        j = i // 2
        attn_c = attn_l = None
        if i % 2 == 0:
            xc = _fourier(xc, mods, i, ctx_len, ctx_row, cs, dft_ctx, wf, g1, b1, tq=t["fourier_ctx"])
            xl = _fourier(xl, mods, i, lat_len, lambda s: 1 + s, cs, dft_lat, wf, g1, b1,
                          tq=t["fourier_lat"])
        else:
            lam_init = 0.8 - 0.6 * math.exp(-0.3 * i)
            lam_vecs = jnp.stack([lambda_q1[j], lambda_k1[j], lambda_q2[j], lambda_k2[j]])
            sg = subln_g[j][None, :]
            q0, q1, k, v, new_k, new_v = _qkv(
                xc, mods, i, ctx_len, ctx_row, wqkv, tm=t["row_ctx"], cache=caches, slot=j,
                n_slots=DEPTH // 2)
            caches = (new_k, new_v)
            oc = _attention(q0, q1, k, v, lam_vecs, sg, lam_init, ctx_len,
                            tq=t["attn_q_ctx"], hb=t["attn_heads_ctx"])
            lat_row = lat_row_of(t["row_lat"])
            q0, q1, k, v = _qkv(xl, mods, i, lat_len, lat_row, wqkv, tm=t["row_lat"], rope_tables=rope)
            ol = _attention(q0, q1, k, v, lam_vecs, sg, lam_init, lat_len,
                            tq=t["attn_q_lat"], hb=t["attn_heads_lat"], cache=(ck_in, cv_in, j))
            attn_c, attn_l = (oc, wo, g1, b1), (ol, wo, g1, b1)
        xc = _ffn(xc, mods, i, ctx_len, ctx_row, wup, cw, cb, wdn, g2, b2, tm=t["ffn_ctx"], attn=attn_c)
        xl = _ffn(xl, mods, i, lat_len, lat_row_of(t["ffn_lat"]), wup, cw, cb, wdn, g2, b2,
                  tm=t["ffn_lat"], attn=attn_l)

    y_prompt = xc.reshape(x_prompt.shape)
    y_sample = xl.reshape(x_sample.shape)
    new_cache_k = new_k.reshape(n_ctx_seq, DEPTH // 2, ctx_len, N_HEADS, 2, HEAD_DIM)
    new_cache_v = new_v.reshape(n_ctx_seq, DEPTH // 2, ctx_len, N_HEADS, V_DIM)
    return (y_prompt, y_sample, new_cache_k, new_cache_v)
```

```python
import functools
import math

import jax
import jax.numpy as jnp
from jax import lax
from jax.experimental import pallas as pl
from jax.experimental.pallas import tpu as pltpu

F32 = jnp.float32
BF16 = jnp.bfloat16

D_MODEL = 1024
DEPTH = 4
GRID_W = 64
N_HEADS = 8
HEAD_DIM = 64
V_DIM = 2 * HEAD_DIM
N_FOURIER_GROUPS = 4
FOURIER_GROUP = D_MODEL // N_FOURIER_GROUPS
D_FF = 2816
ROPE_THETA = 10000.0
N_FREQ = HEAD_DIM // 4
DN_ALPHA = (2 * DEPTH) ** 0.25
LN_EPS = 1e-6
SUBLN_EPS = 1e-5

LANES = 128
SUBLANES = 8
MOD_ROWS = 8
FF_CHUNK = 256
HALO = 16
QKV_CHUNK = 256
VMEM_LIMIT = 56 * 1024 * 1024
LOCKSTEP_SCORE_BYTES = 8 * 1024 * 1024
SCORE_LOOKAHEAD = 2
Q_SCALE = HEAD_DIM ** -0.5 * math.log2(math.e)


def _params(sem, vmem=VMEM_LIMIT):
    return pltpu.CompilerParams(dimension_semantics=sem, vmem_limit_bytes=vmem)


def _ln(x):
    mu = jnp.mean(x, axis=-1, keepdims=True)
    xc = x - mu
    var = jnp.mean(xc * xc, axis=-1, keepdims=True)
    return xc * lax.rsqrt(var + LN_EPS)


def _modulate(x, shift, scale):
    return _ln(x) * (1.0 + scale) + shift


def _post_norm(x, update, g, b):
    return _ln(DN_ALPHA * x + update) * g + b


def _silu(x):
    return x / (1.0 + jnp.exp(-x))


def _const_spec(shape):
    return pl.BlockSpec(shape, lambda *_: (0,) * len(shape))


def _layer_spec(shape, layer):
    return pl.BlockSpec((None,) + tuple(shape), lambda *_: (layer,) + (0,) * len(shape))


def _ada_kernel(c_ref, w_ref, b_ref, o_ref):
    s = _silu(c_ref[...]).astype(BF16)
    a = jnp.dot(s, w_ref[0].astype(BF16), preferred_element_type=F32)
    o_ref[0] = a + b_ref[0]


def _ada_table(cvec, w_ada, b_ada):
    nc = 2 * D_MODEL
    n_col = w_ada.shape[-1] // nc
    out = pl.pallas_call(
        _ada_kernel,
        out_shape=jax.ShapeDtypeStruct((DEPTH, MOD_ROWS, w_ada.shape[-1]), F32),
        grid=(DEPTH, n_col),
        in_specs=[
            pl.BlockSpec((MOD_ROWS, D_MODEL), lambda l, j: (0, 0)),
            pl.BlockSpec((1, D_MODEL, nc), lambda l, j: (l, 0, j)),
            pl.BlockSpec((1, 1, nc), lambda l, j: (l, 0, j)),
        ],
        out_specs=pl.BlockSpec((1, MOD_ROWS, nc), lambda l, j: (l, 0, j)),
        compiler_params=_params(("arbitrary", "arbitrary")),
        name="ada_table",
    )(cvec, w_ada, b_ada.reshape(DEPTH, 1, -1))
    return out.reshape(DEPTH * MOD_ROWS, 6, D_MODEL)


def _ffn_kernel(*refs, tm, seq_len, attn_proj):
    halo = seq_len > tm
    x_ref = refs[0]
    if halo:
        xp_ref, xn_ref = refs[1:3]
        refs = refs[2:]
    if attn_proj:
        a_ref = refs[1]
        refs = refs[1:]
        if halo:
            ap_ref, an_ref = refs[1:3]
            refs = refs[2:]
        wo_ref, g1_ref, b1_ref = refs[1:4]
        refs = refs[3:]
    (mod_ref, wup_ref, cw_ref, cb_ref, wdn_ref, g_ref, b_ref,
     o_ref, slab_ref, hext_ref, act_ref) = refs[1:]
    seg = tm // SUBLANES
    n_slab = D_MODEL // LANES
    shift, scale, gate = mod_ref[0, 3:4, :], mod_ref[0, 4:5, :], mod_ref[0, 5:6, :]
    x = x_ref[...]
    if halo:
        xp, xn = xp_ref[...], xn_ref[...]
    if attn_proj:
        gate1 = mod_ref[0, 2:3, :]
        a = a_ref[...]
        if halo:
            a = jnp.concatenate([a, ap_ref[...], an_ref[...]], axis=0)
        m = jnp.dot(a, wo_ref[...], preferred_element_type=F32)
        x = _post_norm(x, gate1 * m[0:tm], g1_ref[...], b1_ref[...])
        if halo:
            xp = _post_norm(xp, gate1 * m[tm + 8:tm + 16], g1_ref[...], b1_ref[...])
            xn = _post_norm(xn, gate1 * m[tm + 16:tm + 24], g1_ref[...], b1_ref[...])
    h = _modulate(x, shift, scale)
    for cb in range(n_slab):
        for s in range(SUBLANES):
            slab_ref[cb, pl.ds(s, seg, stride=SUBLANES), :] = (
                h[s * seg:(s + 1) * seg, cb * LANES:(cb + 1) * LANES])
    for cb in range(n_slab):
        hext_ref[0:tm, cb * LANES:(cb + 1) * LANES] = slab_ref[cb].astype(BF16)
    row = lax.broadcasted_iota(jnp.int32, (SUBLANES, FF_CHUNK), 0)
    if halo:
        tiles_per_seq = seq_len // tm
        pos = pl.program_id(0) % tiles_per_seq
        hp = jnp.where(pos > 0, _modulate(xp, shift, scale), 0.0)
        hn = jnp.where(pos < tiles_per_seq - 1, _modulate(xn, shift, scale), 0.0)
        hext_ref[tm:, :] = jnp.concatenate([hp, hn], axis=0).astype(BF16)
    else:
        assert tm % seq_len == 0 and seq_len % seg == 0
        seq_start = functools.reduce(jnp.logical_or, [row == s for s in range(SUBLANES) if (s * seg) % seq_len == 0])
        seq_end = functools.reduce(jnp.logical_or, [row == s for s in range(SUBLANES) if ((s + 1) * seg) % seq_len == 0])
    hext = hext_ref[...]

    def conv(u, col):
        cw = cw_ref[:, col:col + FF_CHUNK]
        first, last = pltpu.roll(u[tm - 8:tm], 1, 0), pltpu.roll(u[0:8], 7, 0)
        if halo:
            first = jnp.where(row == 0, u[tm + 7:tm + 8], first)
            last = jnp.where(row == 7, u[tm + 8:tm + 9], last)
        else:
            first = jnp.where(seq_start, 0.0, first)
            last = jnp.where(seq_end, 0.0, last)
        prev = jnp.concatenate([first, u[0:tm - 8]], axis=0)
        nxt = jnp.concatenate([u[8:tm], last], axis=0)
        return prev * cw[0:1] + u[0:tm] * cw[1:2] + nxt * cw[2:3] + cb_ref[:, col:col + FF_CHUNK]

    for c in range(D_FF // FF_CHUNK):
        ca, cg = c * FF_CHUNK, D_FF + c * FF_CHUNK
        ua = jnp.dot(hext, wup_ref[:, ca:ca + FF_CHUNK], preferred_element_type=F32)
        ug = jnp.dot(hext, wup_ref[:, cg:cg + FF_CHUNK], preferred_element_type=F32)
        act_ref[:, ca:ca + FF_CHUNK] = (_silu(conv(ua, ca)) * conv(ug, cg)).astype(BF16)
    f = jnp.dot(act_ref[...], wdn_ref[...], preferred_element_type=F32)
    for cb in range(n_slab):
        slab_ref[cb] = f[:, cb * LANES:(cb + 1) * LANES]
    for cb in range(n_slab):
        for s in range(SUBLANES):
            o_ref[s * seg:(s + 1) * seg, cb * LANES:(cb + 1) * LANES] = (
                slab_ref[cb, pl.ds(s, seg, stride=SUBLANES), :])
    o_ref[...] = _post_norm(x, gate * o_ref[...], g_ref[...], b_ref[...])


def _ffn(x, mods, layer, seq_len, mod_row, w_up, conv_w, conv_b, w_down, g, b, tm, attn=None):
    rows = x.shape[0]
    halo = seq_len > tm
    kern = functools.partial(_ffn_kernel, tm=tm, seq_len=seq_len, attn_proj=attn is not None)

    def halo_specs(block_rows):
        per_tile = tm // block_rows
        return [
            pl.BlockSpec((block_rows, D_MODEL), lambda i: (jnp.maximum(i * per_tile - 1, 0), 0)),
            pl.BlockSpec((block_rows, D_MODEL),
                         lambda i: (jnp.minimum((i + 1) * per_tile, rows // block_rows - 1), 0)),
        ]

    in_specs = [pl.BlockSpec((tm, D_MODEL), lambda i: (i, 0))]
    args = [x]
    if halo:
        in_specs += halo_specs(SUBLANES)
        args += [x, x]
    if attn is not None:
        a, w_o, g1, b1 = attn
        in_specs += [pl.BlockSpec((tm, D_MODEL), lambda i: (i, 0))]
        args += [a]
        if halo:
            in_specs += halo_specs(HALO)
            args += [a, a]
        in_specs += [_layer_spec((D_MODEL, D_MODEL), layer // 2),
                     _layer_spec((1, D_MODEL), layer), _layer_spec((1, D_MODEL), layer)]
        args += [w_o, g1, b1]
    in_specs += [
        pl.BlockSpec((1, 6, D_MODEL), lambda i: (layer * MOD_ROWS + mod_row(i), 0, 0)),
        _layer_spec((D_MODEL, 2 * D_FF), layer),
        _layer_spec((3, 2 * D_FF), layer),
        _layer_spec((1, 2 * D_FF), layer),
        _layer_spec((D_FF, D_MODEL), layer),
        _layer_spec((1, D_MODEL), layer),
        _layer_spec((1, D_MODEL), layer),
    ]
    args += [mods, w_up, conv_w, conv_b, w_down, g, b]
    return pl.pallas_call(
        kern,
        out_shape=jax.ShapeDtypeStruct(x.shape, F32),
        grid=(rows // tm,),
        in_specs=in_specs,
        out_specs=pl.BlockSpec((tm, D_MODEL), lambda i: (i, 0)),
        scratch_shapes=[
            pltpu.VMEM((D_MODEL // LANES, tm, LANES), F32),
            pltpu.VMEM((tm + (HALO if halo else 0), D_MODEL), BF16),
            pltpu.VMEM((tm, D_FF), BF16),
        ],
        compiler_params=_params(("arbitrary",)),
        name="conv_ffn",
    )(*args)


def _fourier_kernel(x_ref, mod_ref, cs_ref, dc_ref, ds_ref, wf_ref, g_ref, b_ref, o_ref, y_ref,
                    *, n, tq, rows_per_step):
    j = pl.program_id(1)
    shift, scale, gate = mod_ref[0, 0:1, :], mod_ref[0, 1:2, :], mod_ref[0, 2:3, :]

    @pl.when(j == 0)
    def _():
        for r in range(n // rows_per_step):
            r0 = r * rows_per_step
            h = _modulate(x_ref[r0:r0 + rows_per_step, :], shift, scale).astype(BF16)
            for gi in range(N_FOURIER_GROUPS):
                c0 = gi * FOURIER_GROUP
                y = jnp.dot(h[:, c0:c0 + FOURIER_GROUP], cs_ref[...], preferred_element_type=F32)
                y_ref[r0:r0 + rows_per_step, c0:c0 + FOURIER_GROUP] = y[:, :FOURIER_GROUP].astype(BF16)
                y_ref[n + r0:n + r0 + rows_per_step, c0:c0 + FOURIER_GROUP] = y[:, FOURIER_GROUP:].astype(BF16)

    f = (jnp.dot(dc_ref[...], y_ref[0:n, :], preferred_element_type=F32)
         + jnp.dot(ds_ref[...], y_ref[n:, :], preferred_element_type=F32))
    m = jnp.dot(f.astype(BF16), wf_ref[...], preferred_element_type=F32)
    x = x_ref[pl.ds(pl.multiple_of(j * tq, tq), tq), :]
    o_ref[...] = _post_norm(x, gate * m, g_ref[...], b_ref[...])


def _fourier(x, mods, layer, seq_len, mod_row, cs, dft, w_f, g, b, tq):
    rows = x.shape[0]
    n_seq = rows // seq_len
    steps = seq_len // tq
    kern = functools.partial(_fourier_kernel, n=seq_len, tq=tq, rows_per_step=min(seq_len, 512))
    return pl.pallas_call(
        kern,
        out_shape=jax.ShapeDtypeStruct(x.shape, F32),
        grid=(n_seq, steps),
        in_specs=[
            pl.BlockSpec((seq_len, D_MODEL), lambda s, j: (s, 0)),
            pl.BlockSpec((1, 6, D_MODEL), lambda s, j: (layer * MOD_ROWS + mod_row(s), 0, 0)),
            _const_spec((FOURIER_GROUP, 2 * FOURIER_GROUP)),
            pl.BlockSpec((tq, seq_len), lambda s, j: (j, 0)),
            pl.BlockSpec((tq, seq_len), lambda s, j: (j, 0)),
            _layer_spec((D_MODEL, D_MODEL), layer // 2),
            _layer_spec((1, D_MODEL), layer),
            _layer_spec((1, D_MODEL), layer),
        ],
        out_specs=pl.BlockSpec((tq, D_MODEL), lambda s, j: (s * steps + j, 0)),
        scratch_shapes=[pltpu.VMEM((2 * seq_len, D_MODEL), BF16)],
        compiler_params=_params(("arbitrary", "arbitrary")),
        name="fourier_mix",
    )(x, mods, cs, dft[0], dft[1], w_f, g, b)


def _cos_sin(k, t, n):
    ang = ((k * t) % n).astype(F32) * (2.0 * math.pi / n)
    return jnp.cos(ang), jnp.sin(ang)


def _dft_table_kernel(ca_ref, sa_ref, cb_ref, sb_ref, c_ref, s_ref, *, n):
    cb, sb = cb_ref[...], sb_ref[...]
    norm = n ** -0.5
    for i in range(n // LANES):
        ca, sa = ca_ref[:, i:i + 1], sa_ref[:, i:i + 1]
        c_ref[:, i * LANES:(i + 1) * LANES] = ((ca * cb - sa * sb) * norm).astype(BF16)
        s_ref[:, i * LANES:(i + 1) * LANES] = ((sa * cb + ca * sb) * -norm).astype(BF16)


def _dft_table(n):
    n_hi = n // LANES
    k = lax.broadcasted_iota(jnp.int32, (n, n_hi), 0)
    t_hi = lax.broadcasted_iota(jnp.int32, (n, n_hi), 1) * LANES
    ca, sa = _cos_sin(k, t_hi, n)
    k = lax.broadcasted_iota(jnp.int32, (n, LANES), 0)
    t_lo = lax.broadcasted_iota(jnp.int32, (n, LANES), 1)
    cb, sb = _cos_sin(k, t_lo, n)
    tr = min(n, 256)
    hi_spec = pl.BlockSpec((tr, n_hi), lambda i: (i, 0))
    lo_spec = pl.BlockSpec((tr, LANES), lambda i: (i, 0))
    out_spec = pl.BlockSpec((tr, n), lambda i: (i, 0))
    table = jax.ShapeDtypeStruct((n, n), BF16)
    return pl.pallas_call(
        functools.partial(_dft_table_kernel, n=n),
        out_shape=[table, table],
        grid=(n // tr,),
        in_specs=[hi_spec, hi_spec, lo_spec, lo_spec],
        out_specs=[out_spec, out_spec],
        compiler_params=_params(("arbitrary",)),
        name="dft_table",
    )(ca, sa, cb, sb)


def _channel_dft_table():
    n = FOURIER_GROUP
    k = lax.broadcasted_iota(jnp.int32, (n, n), 0)
    t = lax.broadcasted_iota(jnp.int32, (n, n), 1)
    cos, sin = _cos_sin(k, t, n)
    return (jnp.concatenate([cos, sin], axis=1) * (n ** -0.5)).astype(BF16)


def _qkv_kernel(*refs, tm, rope, cache):
    x_ref, mod_ref, w_ref = refs[:3]
    refs = refs[3:]
    if rope:
        cos_ref, sin_ref = refs[:2]
        refs = refs[2:]
    if cache == "update":
        refs = refs[2:]
    q0_ref, q1_ref, k_ref, v_ref = refs[:4]
    if cache:
        kc_ref, vc_ref = refs[4:6]
    if cache == "create" and kc_ref.shape[1] > 1:
        kc_ref[:, 1:] = jnp.zeros((kc_ref.shape[0], kc_ref.shape[1] - 1) + kc_ref.shape[2:], F32)
        vc_ref[:, 1:] = jnp.zeros((vc_ref.shape[0], vc_ref.shape[1] - 1) + vc_ref.shape[2:], F32)
    shift, scale = mod_ref[0, 0:1, :], mod_ref[0, 1:2, :]
    h = _modulate(x_ref[...], shift, scale).astype(BF16)
    lane = lax.broadcasted_iota(jnp.int32, (tm, QKV_CHUNK), 1)
    first_half = (lane % V_DIM) < HEAD_DIM
    if rope:
        reps = QKV_CHUNK // V_DIM
        cos = jnp.concatenate([cos_ref[...]] * reps, axis=1)
        sin = jnp.concatenate([sin_ref[...]] * reps, axis=1)
        swap_up = (lane % (2 * N_FREQ)) < N_FREQ

    def rotary(y):
        up = jnp.concatenate([pltpu.roll(y[:, c:c + LANES], LANES - N_FREQ, 1)
                              for c in range(0, QKV_CHUNK, LANES)], axis=1)
        down = jnp.concatenate([pltpu.roll(y[:, c:c + LANES], N_FREQ, 1)
                                for c in range(0, QKV_CHUNK, LANES)], axis=1)
        return y * cos + jnp.where(swap_up, up, down) * sin

    def proj(col):
        return jnp.dot(h, w_ref[:, col:col + QKV_CHUNK], preferred_element_type=F32)

    for c0 in range(0, D_MODEL, QKV_CHUNK):
        cols = slice(c0, c0 + QKV_CHUNK)
        q, k, v = proj(c0), proj(D_MODEL + c0), proj(2 * D_MODEL + c0)
        if rope:
            q, k = rotary(q), rotary(k)
        q = q * Q_SCALE
        q0_ref[:, cols] = jnp.where(first_half, q, 0.0).astype(BF16)
        q1_ref[:, cols] = jnp.where(first_half, 0.0, q).astype(BF16)
        k_ref[:, cols] = k.astype(BF16)
        v_ref[:, cols] = v.astype(BF16)
        if cache:
            kc_ref[:, 0, :, cols] = k.reshape(kc_ref.shape[0], kc_ref.shape[2], QKV_CHUNK)
            vc_ref[:, 0, :, cols] = v.reshape(vc_ref.shape[0], vc_ref.shape[2], QKV_CHUNK)


def _qkv(x, mods, layer, seq_len, mod_row, w_qkv, tm, rope_tables=None, cache=None, slot=0, n_slots=1):
    rows = x.shape[0]
    rope = rope_tables is not None
    with_cache = cache is not None
    tiles_per_seq = max(seq_len // tm, 1)
    in_specs = [
        pl.BlockSpec((tm, D_MODEL), lambda i: (i, 0)),
        pl.BlockSpec((1, 6, D_MODEL), lambda i: (layer * MOD_ROWS + mod_row(i), 0, 0)),
        _layer_spec((D_MODEL, 3 * D_MODEL), layer // 2),
    ]
    args = [x, mods, w_qkv]
    if rope:
        in_specs += [pl.BlockSpec((tm, V_DIM), lambda i: (i % tiles_per_seq, 0))] * 2
        args += list(rope_tables)
    act = jax.ShapeDtypeStruct((rows, D_MODEL), BF16)
    out_shape = [act] * 4
    out_specs = [pl.BlockSpec((tm, D_MODEL), lambda i: (i, 0))] * 4
    aliases, mode = {}, None
    if with_cache:
        seqs = tm // seq_len
        if cache:
            mode = "update"
            in_specs += [pl.BlockSpec(memory_space=pl.ANY)] * 2
            aliases = {len(args): 4, len(args) + 1: 5}
            args += list(cache)
            full = jax.ShapeDtypeStruct(cache[0].shape, F32)
            out_specs += [pl.BlockSpec((seqs, 1, seq_len, D_MODEL), lambda i: (i, slot, 0, 0))] * 2
        else:
            mode = "create"
            full = jax.ShapeDtypeStruct((rows // seq_len, n_slots, seq_len, D_MODEL), F32)
            out_specs += [pl.BlockSpec((seqs, n_slots, seq_len, D_MODEL), lambda i: (i, 0, 0, 0))] * 2
        out_shape += [full, full]
    kern = functools.partial(_qkv_kernel, tm=tm, rope=rope, cache=mode)
    return pl.pallas_call(
        kern,
        out_shape=out_shape,
        grid=(rows // tm,),
        in_specs=in_specs,
        out_specs=out_specs,
        input_output_aliases=aliases,
        compiler_params=_params(("arbitrary",)),
        name="qkv_proj",
    )(*args)


def _attn_kernel(*refs, hb, tq, lam_init, cache, lockstep):
    lam_ref, g_ref, q0_ref, q1_ref, k_ref, v_ref = refs[:6]
    if cache:
        ck_ref, cv_ref = refs[6:8]
    o_ref = refs[-1]
    lv = lam_ref[...]
    lam = (jnp.exp(jnp.sum(lv[0:1] * lv[1:2], axis=-1, keepdims=True))
           - jnp.exp(jnp.sum(lv[2:3] * lv[3:4], axis=-1, keepdims=True)) + lam_init)
    nt = (((1,), (1,)), ((), ()))

    def scores(hh):
        cols = slice(hh * V_DIM, (hh + 1) * V_DIM)
        qq = jnp.concatenate([q0_ref[:, cols], q1_ref[:, cols]], axis=0)
        s = lax.dot_general(qq, k_ref[:, cols], nt, preferred_element_type=F32)
        sc = None
        if cache:
            sc = lax.dot_general(qq, ck_ref[:, cols].astype(BF16), nt, preferred_element_type=F32)
        return s, sc

    def row_max(s, sc):
        m = jnp.max(s, axis=-1, keepdims=True)
        return jnp.maximum(m, jnp.max(sc, axis=-1, keepdims=True)) if cache else m

    def exponentials(s, sc, m):
        return jnp.exp2(s - m), (jnp.exp2(sc - m) if cache else None)

    def row_sum(p, pc):
        l = jnp.sum(p, axis=-1, keepdims=True)
        return l + jnp.sum(pc, axis=-1, keepdims=True) if cache else l

    def weighted_values(hh, p, pc, l):
        cols = slice(hh * V_DIM, (hh + 1) * V_DIM)
        ratio = l[:tq] * lam / l[tq:]
        w = (p[:tq] - p[tq:] * ratio).astype(BF16)
        o = jnp.dot(w, v_ref[:, cols], preferred_element_type=F32)
        if cache:
            wc = (pc[:tq] - pc[tq:] * ratio).astype(BF16)
            o = o + jnp.dot(wc, cv_ref[:, cols].astype(BF16), preferred_element_type=F32)
        return o / l[:tq]

    ones_col = (lax.broadcasted_iota(jnp.int32, (1, V_DIM), 1) == 0).astype(BF16)

    def with_ones(v):
        return jnp.concatenate([v, jnp.broadcast_to(ones_col, v.shape)], axis=1)

    def weighted_values_unnormalised(hh, p, pc):
        cols = slice(hh * V_DIM, (hh + 1) * V_DIM)
        o = jnp.dot(p.astype(BF16), with_ones(v_ref[:, cols]), preferred_element_type=F32)
        if cache:
            o = o + jnp.dot(pc.astype(BF16), with_ones(cv_ref[:, cols].astype(BF16)),
                            preferred_element_type=F32)
        o = o[:, :V_DIM] / o[:, V_DIM:V_DIM + 1]
        return o[:tq] - lam * o[tq:]

    def sub_norm(o):
        return lax.rsqrt(jnp.mean(o * o, axis=-1, keepdims=True) + SUBLN_EPS)

    def emit(hh, o, r):
        o_ref[:, hh * V_DIM:(hh + 1) * V_DIM] = (o * r * g_ref[...] * (1.0 - lam_init)).astype(BF16)

    heads = range(hb)
    if lockstep:
        S = [scores(hh) for hh in heads]
        M = [row_max(*S[hh]) for hh in heads]
        P = [exponentials(*S[hh], M[hh]) for hh in heads]
        L = [row_sum(*P[hh]) for hh in heads]
        O = [weighted_values(hh, *P[hh], L[hh]) for hh in heads]
        R = [sub_norm(O[hh]) for hh in heads]
        for hh in heads:
            emit(hh, O[hh], R[hh])
    else:
        pending = [scores(hh) for hh in range(min(SCORE_LOOKAHEAD, hb))]
        for hh in heads:
            if hh + SCORE_LOOKAHEAD < hb:
                pending.append(scores(hh + SCORE_LOOKAHEAD))
            s, sc = pending.pop(0)
            o = weighted_values_unnormalised(hh, *exponentials(s, sc, row_max(s, sc)))
            emit(hh, o, sub_norm(o))


def _attention(q0, q1, k, v, lam_vecs, subln_g, lam_init, seq_len, tq, hb, cache=None):
    rows = q0.shape[0]
    n_seq = rows // seq_len
    q_steps = seq_len // tq
    wb = hb * V_DIM
    with_cache = cache is not None
    q_spec = pl.BlockSpec((tq, wb), lambda s, h, j: (s * q_steps + j, h))
    kv_spec = pl.BlockSpec((seq_len, wb), lambda s, h, j: (s, h))
    in_specs = [_const_spec((4, HEAD_DIM)), _const_spec((1, V_DIM)), q_spec, q_spec, kv_spec, kv_spec]
    args = [lam_vecs, subln_g, q0, q1, k, v]
    if with_cache:
        ck, cv, slot = cache
        past = ck.shape[2]
        c_spec = pl.BlockSpec((None, None, past, wb), lambda s, h, j: (s, slot, 0, h))
        in_specs += [c_spec, c_spec]
        args += [ck, cv]
    n_keys = seq_len + (cache[0].shape[2] if with_cache else 0)
    lockstep = hb * 2 * tq * n_keys * 4 <= LOCKSTEP_SCORE_BYTES
    kern = functools.partial(_attn_kernel, hb=hb, tq=tq, lam_init=lam_init, cache=with_cache,
                             lockstep=lockstep)
    return pl.pallas_call(
        kern,
        out_shape=jax.ShapeDtypeStruct((rows, D_MODEL), BF16),
        grid=(n_seq, N_HEADS // hb, q_steps),
        in_specs=in_specs,
        out_specs=q_spec,
        compiler_params=_params(("arbitrary", "arbitrary", "arbitrary")),
        name="diff_attention",
    )(*args)


def _rope_tables(n):
    token = lax.broadcasted_iota(jnp.int32, (n, N_FREQ), 0)
    row = (token // GRID_W).astype(F32)
    col = (token % GRID_W).astype(F32)
    inv = 1.0 / (ROPE_THETA ** (jnp.arange(N_FREQ, dtype=F32) / N_FREQ))
    ar = row * inv
    ac = col * inv
    ang = jnp.concatenate([ar, ar, ac, ac], axis=-1)
    sign = jnp.tile(jnp.concatenate([-jnp.ones((N_FREQ,), F32), jnp.ones((N_FREQ,), F32)]), 2)
    cos = jnp.cos(ang)
    sin = jnp.sin(ang) * sign
    return jnp.tile(cos, (1, 2)), jnp.tile(sin, (1, 2))


def _tiles(ctx_len, lat_len):
    return dict(
        ffn_ctx=2 * ctx_len, ffn_lat=512,
        row_ctx=512, row_lat=1024,
        fourier_ctx=ctx_len, fourier_lat=512,
        attn_q_ctx=ctx_len, attn_q_lat=256,
        attn_heads_ctx=N_HEADS, attn_heads_lat=N_HEADS,
    )


def kernel(x_prompt, x_sample, cache_k, cache_v, c, c_ctx, w_ada, b_ada, w_fourier, w_qkv,
           lambda_q1, lambda_k1, lambda_q2, lambda_k2, subln_g, w_o, w_up, conv_w, conv_b,
           w_down, ln1_g, ln1_b, ln2_g, ln2_b):
    n_ctx_seq, ctx_len, d = x_prompt.shape
    n_lat_seq, lat_len, _ = x_sample.shape
    past = cache_k.shape[2]
    assert d == D_MODEL and n_lat_seq + 1 <= MOD_ROWS
    t = _tiles(ctx_len, lat_len)

    cvec = jnp.concatenate(
        [c_ctx[None, :], c, jnp.zeros((MOD_ROWS - 1 - n_lat_seq, d), F32)], axis=0)
    mods = _ada_table(cvec, w_ada, b_ada)

    xc = x_prompt.reshape(n_ctx_seq * ctx_len, d)
    xl = x_sample.reshape(n_lat_seq * lat_len, d)
    ck_in = cache_k.reshape(n_lat_seq, DEPTH // 2, past, d)
    cv_in = cache_v.reshape(n_lat_seq, DEPTH // 2, past, d)

    ctx_row = lambda i: 0
    lat_row_of = lambda tile: (lambda i: 1 + i // (lat_len // tile))
    cs = _channel_dft_table()
    dft_ctx, dft_lat = _dft_table(ctx_len), _dft_table(lat_len)
    rope = _rope_tables(lat_len)
    caches = ()

    wf, wqkv, wo = w_fourier.astype(BF16), w_qkv.astype(BF16), w_o.astype(BF16)
    wup, wdn = w_up.astype(BF16), w_down.astype(BF16)
    cw, cb = conv_w, conv_b[:, None, :]
    g1, b1 = ln1_g[:, None, :], ln1_b[:, None, :]
    g2, b2 = ln2_g[:, None, :], ln2_b[:, None, :]

    for i in range(DEPTH):
        j = i // 2
        attn_c = attn_l = None
        if i % 2 == 0:
            xc = _fourier(xc, mods, i, ctx_len, ctx_row, cs, dft_ctx, wf, g1, b1, tq=t["fourier_ctx"])
            xl = _fourier(xl, mods, i, lat_len, lambda s: 1 + s, cs, dft_lat, wf, g1, b1,
                          tq=t["fourier_lat"])
        else:
            lam_init = 0.8 - 0.6 * math.exp(-0.3 * i)
            lam_vecs = jnp.stack([lambda_q1[j], lambda_k1[j], lambda_q2[j], lambda_k2[j]])
            sg = subln_g[j][None, :]
            q0, q1, k, v, new_k, new_v = _qkv(
                xc, mods, i, ctx_len, ctx_row, wqkv, tm=t["row_ctx"], cache=caches, slot=j,
                n_slots=DEPTH // 2)
            caches = (new_k, new_v)
            oc = _attention(q0, q1, k, v, lam_vecs, sg, lam_init, ctx_len,
                            tq=t["attn_q_ctx"], hb=t["attn_heads_ctx"])
            lat_row = lat_row_of(t["row_lat"])
            q0, q1, k, v = _qkv(xl, mods, i, lat_len, lat_row, wqkv, tm=t["row_lat"], rope_tables=rope)
            ol = _attention(q0, q1, k, v, lam_vecs, sg, lam_init, lat_len,
                            tq=t["attn_q_lat"], hb=t["attn_heads_lat"], cache=(ck_in, cv_in, j))
            attn_c, attn_l = (oc, wo, g1, b1), (ol, wo, g1, b1)
        xc = _ffn(xc, mods, i, ctx_len, ctx_row, wup, cw, cb, wdn, g2, b2, tm=t["ffn_ctx"], attn=attn_c)
        xl = _ffn(xl, mods, i, lat_len, lat_row_of(t["ffn_lat"]), wup, cw, cb, wdn, g2, b2,
                  tm=t["ffn_lat"], attn=attn_l)

    y_prompt = xc.reshape(x_prompt.shape)
    y_sample = xl.reshape(x_sample.shape)
    new_cache_k = new_k.reshape(n_ctx_seq, DEPTH // 2, ctx_len, N_HEADS, 2, HEAD_DIM)
    new_cache_v = new_v.reshape(n_ctx_seq, DEPTH // 2, ctx_len, N_HEADS, V_DIM)
    return (y_prompt, y_sample, new_cache_k, new_cache_v)
```

```python
import functools
import math

import jax
import jax.numpy as jnp
from jax import lax
from jax.experimental import pallas as pl
from jax.experimental.pallas import tpu as pltpu

F32 = jnp.float32
BF16 = jnp.bfloat16

D_MODEL = 1024
DEPTH = 4
GRID_W = 64
N_HEADS = 8
HEAD_DIM = 64
V_DIM = 2 * HEAD_DIM
N_FOURIER_GROUPS = 4
FOURIER_GROUP = D_MODEL // N_FOURIER_GROUPS
D_FF = 2816
ROPE_THETA = 10000.0
N_FREQ = HEAD_DIM // 4
DN_ALPHA = (2 * DEPTH) ** 0.25
LN_EPS = 1e-6
SUBLN_EPS = 1e-5

LANES = 128
SUBLANES = 8
MOD_ROWS = 8
FF_CHUNK = 256
HALO = 16
QKV_CHUNK = 256
VMEM_LIMIT = 56 * 1024 * 1024
LOCKSTEP_SCORE_BYTES = 8 * 1024 * 1024
SCORE_LOOKAHEAD = 2
Q_SCALE = HEAD_DIM ** -0.5 * math.log2(math.e)


def _params(sem, vmem=VMEM_LIMIT):
    return pltpu.CompilerParams(dimension_semantics=sem, vmem_limit_bytes=vmem)


def _ln(x):
    mu = jnp.mean(x, axis=-1, keepdims=True)
    xc = x - mu
    var = jnp.mean(xc * xc, axis=-1, keepdims=True)
    return xc * lax.rsqrt(var + LN_EPS)


def _modulate(x, shift, scale):
    return _ln(x) * (1.0 + scale) + shift


def _post_norm(x, update, g, b):
    return _ln(DN_ALPHA * x + update) * g + b


def _silu(x):
    return x / (1.0 + jnp.exp(-x))


def _const_spec(shape):
    return pl.BlockSpec(shape, lambda *_: (0,) * len(shape))


def _layer_spec(shape, layer):
    return pl.BlockSpec((None,) + tuple(shape), lambda *_: (layer,) + (0,) * len(shape))


def _ada_kernel(c_ref, w_ref, b_ref, o_ref):
    s = _silu(c_ref[...]).astype(BF16)
    a = jnp.dot(s, w_ref[0].astype(BF16), preferred_element_type=F32)
    o_ref[0] = a + b_ref[0]


def _ada_table(cvec, w_ada, b_ada):
    nc = 2 * D_MODEL
    n_col = w_ada.shape[-1] // nc
    out = pl.pallas_call(
        _ada_kernel,
        out_shape=jax.ShapeDtypeStruct((DEPTH, MOD_ROWS, w_ada.shape[-1]), F32),
        grid=(DEPTH, n_col),
        in_specs=[
            pl.BlockSpec((MOD_ROWS, D_MODEL), lambda l, j: (0, 0)),
            pl.BlockSpec((1, D_MODEL, nc), lambda l, j: (l, 0, j)),
            pl.BlockSpec((1, 1, nc), lambda l, j: (l, 0, j)),
        ],
        out_specs=pl.BlockSpec((1, MOD_ROWS, nc), lambda l, j: (l, 0, j)),
        compiler_params=_params(("arbitrary", "arbitrary")),
        name="ada_table",
    )(cvec, w_ada, b_ada.reshape(DEPTH, 1, -1))
    return out.reshape(DEPTH * MOD_ROWS, 6, D_MODEL)


def _ffn_kernel(*refs, tm, seq_len, attn_proj):
    halo = seq_len > tm
    x_ref = refs[0]
    if halo:
        xp_ref, xn_ref = refs[1:3]
        refs = refs[2:]
    if attn_proj:
        a_ref = refs[1]
        refs = refs[1:]
        if halo:
            ap_ref, an_ref = refs[1:3]
            refs = refs[2:]
        wo_ref, g1_ref, b1_ref = refs[1:4]
        refs = refs[3:]
    (mod_ref, wup_ref, cw_ref, cb_ref, wdn_ref, g_ref, b_ref,
     o_ref, slab_ref, hext_ref, act_ref) = refs[1:]
    seg = tm // SUBLANES
    n_slab = D_MODEL // LANES
    shift, scale, gate = mod_ref[0, 3:4, :], mod_ref[0, 4:5, :], mod_ref[0, 5:6, :]
    x = x_ref[...]
    if halo:
        xp, xn = xp_ref[...], xn_ref[...]
    if attn_proj:
        gate1 = mod_ref[0, 2:3, :]
        a = a_ref[...]
        if halo:
            a = jnp.concatenate([a, ap_ref[...], an_ref[...]], axis=0)
        m = jnp.dot(a, wo_ref[...], preferred_element_type=F32)
        x = _post_norm(x, gate1 * m[0:tm], g1_ref[...], b1_ref[...])
        if halo:
            xp = _post_norm(xp, gate1 * m[tm + 8:tm + 16], g1_ref[...], b1_ref[...])
            xn = _post_norm(xn, gate1 * m[tm + 16:tm + 24], g1_ref[...], b1_ref[...])
    h = _modulate(x, shift, scale)
    for cb in range(n_slab):
        for s in range(SUBLANES):
            slab_ref[cb, pl.ds(s, seg, stride=SUBLANES), :] = (
                h[s * seg:(s + 1) * seg, cb * LANES:(cb + 1) * LANES])
    for cb in range(n_slab):
        hext_ref[0:tm, cb * LANES:(cb + 1) * LANES] = slab_ref[cb].astype(BF16)
    row = lax.broadcasted_iota(jnp.int32, (SUBLANES, FF_CHUNK), 0)
    if halo:
        tiles_per_seq = seq_len // tm
        pos = pl.program_id(0) % tiles_per_seq
        hp = jnp.where(pos > 0, _modulate(xp, shift, scale), 0.0)
        hn = jnp.where(pos < tiles_per_seq - 1, _modulate(xn, shift, scale), 0.0)
        hext_ref[tm:, :] = jnp.concatenate([hp, hn], axis=0).astype(BF16)
    else:
        assert tm % seq_len == 0 and seq_len % seg == 0
        seq_start = functools.reduce(jnp.logical_or, [row == s for s in range(SUBLANES) if (s * seg) % seq_len == 0])
        seq_end = functools.reduce(jnp.logical_or, [row == s for s in range(SUBLANES) if ((s + 1) * seg) % seq_len == 0])
    hext = hext_ref[...]

    def conv(u, col):
        cw = cw_ref[:, col:col + FF_CHUNK]
        first, last = pltpu.roll(u[tm - 8:tm], 1, 0), pltpu.roll(u[0:8], 7, 0)
        if halo:
            first = jnp.where(row == 0, u[tm + 7:tm + 8], first)
            last = jnp.where(row == 7, u[tm + 8:tm + 9], last)
        else:
            first = jnp.where(seq_start, 0.0, first)
            last = jnp.where(seq_end, 0.0, last)
        prev = jnp.concatenate([first, u[0:tm - 8]], axis=0)
        nxt = jnp.concatenate([u[8:tm], last], axis=0)
        return prev * cw[0:1] + u[0:tm] * cw[1:2] + nxt * cw[2:3] + cb_ref[:, col:col + FF_CHUNK]

    for c in range(D_FF // FF_CHUNK):
        ca, cg = c * FF_CHUNK, D_FF + c * FF_CHUNK
        ua = jnp.dot(hext, wup_ref[:, ca:ca + FF_CHUNK], preferred_element_type=F32)
        ug = jnp.dot(hext, wup_ref[:, cg:cg + FF_CHUNK], preferred_element_type=F32)
        act_ref[:, ca:ca + FF_CHUNK] = (_silu(conv(ua, ca)) * conv(ug, cg)).astype(BF16)
    f = jnp.dot(act_ref[...], wdn_ref[...], preferred_element_type=F32)
    for cb in range(n_slab):
        slab_ref[cb] = f[:, cb * LANES:(cb + 1) * LANES]
    for cb in range(n_slab):
        for s in range(SUBLANES):
            o_ref[s * seg:(s + 1) * seg, cb * LANES:(cb + 1) * LANES] = (
                slab_ref[cb, pl.ds(s, seg, stride=SUBLANES), :])
    o_ref[...] = _post_norm(x, gate * o_ref[...], g_ref[...], b_ref[...])


def _ffn(x, mods, layer, seq_len, mod_row, w_up, conv_w, conv_b, w_down, g, b, tm, attn=None):
    rows = x.shape[0]
    halo = seq_len > tm
    kern = functools.partial(_ffn_kernel, tm=tm, seq_len=seq_len, attn_proj=attn is not None)

    def halo_specs(block_rows):
        per_tile = tm // block_rows
        return [
            pl.BlockSpec((block_rows, D_MODEL), lambda i: (jnp.maximum(i * per_tile - 1, 0), 0)),
            pl.BlockSpec((block_rows, D_MODEL),
                         lambda i: (jnp.minimum((i + 1) * per_tile, rows // block_rows - 1), 0)),
        ]

    in_specs = [pl.BlockSpec((tm, D_MODEL), lambda i: (i, 0))]
    args = [x]
    if halo:
        in_specs += halo_specs(SUBLANES)
        args += [x, x]
    if attn is not None:
        a, w_o, g1, b1 = attn
        in_specs += [pl.BlockSpec((tm, D_MODEL), lambda i: (i, 0))]
        args += [a]
        if halo:
            in_specs += halo_specs(HALO)
            args += [a, a]
        in_specs += [_layer_spec((D_MODEL, D_MODEL), layer // 2),
                     _layer_spec((1, D_MODEL), layer), _layer_spec((1, D_MODEL), layer)]
        args += [w_o, g1, b1]
    in_specs += [
        pl.BlockSpec((1, 6, D_MODEL), lambda i: (layer * MOD_ROWS + mod_row(i), 0, 0)),
        _layer_spec((D_MODEL, 2 * D_FF), layer),
        _layer_spec((3, 2 * D_FF), layer),
        _layer_spec((1, 2 * D_FF), layer),
        _layer_spec((D_FF, D_MODEL), layer),
        _layer_spec((1, D_MODEL), layer),
        _layer_spec((1, D_MODEL), layer),
    ]
    args += [mods, w_up, conv_w, conv_b, w_down, g, b]
    return pl.pallas_call(
        kern,
        out_shape=jax.ShapeDtypeStruct(x.shape, F32),
        grid=(rows // tm,),
        in_specs=in_specs,
        out_specs=pl.BlockSpec((tm, D_MODEL), lambda i: (i, 0)),
        scratch_shapes=[
            pltpu.VMEM((D_MODEL // LANES, tm, LANES), F32),
            pltpu.VMEM((tm + (HALO if halo else 0), D_MODEL), BF16),
            pltpu.VMEM((tm, D_FF), BF16),
        ],
        compiler_params=_params(("arbitrary",)),
        name="conv_ffn",
    )(*args)


def _fourier_kernel(*refs, n, tq, rows_per_step, mirror):
    x_ref, mod_ref, cs_ref, dc_ref, ds_ref, wf_ref = refs[:6]
    refs = refs[6:]
    if mirror:
        wfm_ref, rev_ref = refs[:2]
        refs = refs[2:]
    g_ref, b_ref, o_ref, y_ref = refs[:4]
    if mirror:
        f_ref = refs[4]
    j = pl.program_id(1)
    steps, half_steps = n // tq, (n // 2) // tq
    shift, scale, gate = mod_ref[0, 0:1, :], mod_ref[0, 1:2, :], mod_ref[0, 2:3, :]

    @pl.when(j == 0)
    def _():
        sign = (1 - 2 * (lax.broadcasted_iota(jnp.int32, (rows_per_step, 1), 0) % 2)).astype(F32)
        nyquist = [jnp.zeros((1, FOURIER_GROUP), F32)] * N_FOURIER_GROUPS
        for r in range(n // rows_per_step):
            r0 = r * rows_per_step
            h = _modulate(x_ref[r0:r0 + rows_per_step, :], shift, scale).astype(BF16)
            for gi in range(N_FOURIER_GROUPS):
                c0 = gi * FOURIER_GROUP
                y = jnp.dot(h[:, c0:c0 + FOURIER_GROUP], cs_ref[...], preferred_element_type=F32)
                y_ref[r0:r0 + rows_per_step, c0:c0 + FOURIER_GROUP] = y[:, :FOURIER_GROUP].astype(BF16)
                y_ref[n + r0:n + r0 + rows_per_step, c0:c0 + FOURIER_GROUP] = y[:, FOURIER_GROUP:].astype(BF16)
                if mirror:
                    nyquist[gi] = nyquist[gi] + jnp.sum(y[:, :FOURIER_GROUP] * sign, axis=0, keepdims=True)
        if mirror:
            row = lax.broadcasted_iota(jnp.int32, (SUBLANES, D_MODEL), 0)
            f_ref[n // 2:, :] = jnp.where(row == 0, jnp.concatenate(nyquist, axis=1) * n ** -0.5, 0.0)

    def finish(m):
        x = x_ref[pl.ds(pl.multiple_of(j * tq, tq), tq), :]
        o_ref[...] = _post_norm(x, gate * m, g_ref[...], b_ref[...])

    def direct():
        return (jnp.dot(dc_ref[...], y_ref[0:n, :], preferred_element_type=F32)
                + jnp.dot(ds_ref[...], y_ref[n:, :], preferred_element_type=F32))

    if not mirror:
        finish(jnp.dot(direct().astype(BF16), wf_ref[...], preferred_element_type=F32))
        return

    @pl.when(j < half_steps)
    def _():
        f = direct()
        f_ref[pl.ds(pl.multiple_of(j * tq, tq), tq), :] = f
        finish(jnp.dot(f.astype(BF16), wf_ref[...], preferred_element_type=F32))

    for jj in range(half_steps, steps):
        @pl.when(j == jj)
        def _(jj=jj):
            top = n - jj * tq
            window = f_ref[top - tq + 1:top + 1, :].astype(BF16)
            f = jnp.dot(rev_ref[...], window, preferred_element_type=F32)
            finish(jnp.dot(f.astype(BF16), wfm_ref[...], preferred_element_type=F32))


def _fourier(x, mods, layer, seq_len, mod_row, cs, dft, w_f, g, b, tq, mirror=None):
    rows = x.shape[0]
    n_seq = rows // seq_len
    steps = seq_len // tq
    half_steps = max((seq_len // 2) // tq, 1)
    use_mirror = mirror is not None
    kern = functools.partial(_fourier_kernel, n=seq_len, tq=tq, rows_per_step=min(seq_len, 512),
                             mirror=use_mirror)
    table_spec = pl.BlockSpec((tq, seq_len), lambda s, j: (jnp.minimum(j, half_steps - 1) if use_mirror else j, 0))
    in_specs = [
        pl.BlockSpec((seq_len, D_MODEL), lambda s, j: (s, 0)),
        pl.BlockSpec((1, 6, D_MODEL), lambda s, j: (layer * MOD_ROWS + mod_row(s), 0, 0)),
        _const_spec((FOURIER_GROUP, 2 * FOURIER_GROUP)),
        table_spec, table_spec,
        _layer_spec((D_MODEL, D_MODEL), layer // 2),
    ]
    args = [x, mods, cs, dft[0], dft[1], w_f]
    scratch = [pltpu.VMEM((2 * seq_len, D_MODEL), BF16)]
    if use_mirror:
        in_specs += [_layer_spec((D_MODEL, D_MODEL), layer // 2), _const_spec((tq, tq))]
        args += list(mirror)
        scratch += [pltpu.VMEM((seq_len // 2 + SUBLANES, D_MODEL), F32)]
    in_specs += [_layer_spec((1, D_MODEL), layer), _layer_spec((1, D_MODEL), layer)]
    args += [g, b]
    return pl.pallas_call(
        kern,
        out_shape=jax.ShapeDtypeStruct(x.shape, F32),
        grid=(n_seq, steps),
        in_specs=in_specs,
        out_specs=pl.BlockSpec((tq, D_MODEL), lambda s, j: (s * steps + j, 0)),
        scratch_shapes=scratch,
        compiler_params=_params(("arbitrary", "arbitrary")),
        name="fourier_mix",
    )(*args)


def _cos_sin(k, t, n):
    ang = ((k * t) % n).astype(F32) * (2.0 * math.pi / n)
    return jnp.cos(ang), jnp.sin(ang)


def _dft_table_kernel(ca_ref, sa_ref, cb_ref, sb_ref, c_ref, s_ref, *, n):
    cb, sb = cb_ref[...], sb_ref[...]
    norm = n ** -0.5
    for i in range(n // LANES):
        ca, sa = ca_ref[:, i:i + 1], sa_ref[:, i:i + 1]
        c_ref[:, i * LANES:(i + 1) * LANES] = ((ca * cb - sa * sb) * norm).astype(BF16)
        s_ref[:, i * LANES:(i + 1) * LANES] = ((sa * cb + ca * sb) * -norm).astype(BF16)


def _dft_table(n):
    n_hi = n // LANES
    k = lax.broadcasted_iota(jnp.int32, (n, n_hi), 0)
    t_hi = lax.broadcasted_iota(jnp.int32, (n, n_hi), 1) * LANES
    ca, sa = _cos_sin(k, t_hi, n)
    k = lax.broadcasted_iota(jnp.int32, (n, LANES), 0)
    t_lo = lax.broadcasted_iota(jnp.int32, (n, LANES), 1)
    cb, sb = _cos_sin(k, t_lo, n)
    tr = min(n, 256)
    hi_spec = pl.BlockSpec((tr, n_hi), lambda i: (i, 0))
    lo_spec = pl.BlockSpec((tr, LANES), lambda i: (i, 0))
    out_spec = pl.BlockSpec((tr, n), lambda i: (i, 0))
    table = jax.ShapeDtypeStruct((n, n), BF16)
    return pl.pallas_call(
        functools.partial(_dft_table_kernel, n=n),
        out_shape=[table, table],
        grid=(n // tr,),
        in_specs=[hi_spec, hi_spec, lo_spec, lo_spec],
        out_specs=[out_spec, out_spec],
        compiler_params=_params(("arbitrary",)),
        name="dft_table",
    )(ca, sa, cb, sb)


def _channel_dft_table():
    n = FOURIER_GROUP
    k = lax.broadcasted_iota(jnp.int32, (n, n), 0)
    t = lax.broadcasted_iota(jnp.int32, (n, n), 1)
    cos, sin = _cos_sin(k, t, n)
    return (jnp.concatenate([cos, sin], axis=1) * (n ** -0.5)).astype(BF16)


def _qkv_kernel(*refs, tm, rope, cache):
    x_ref, mod_ref, w_ref = refs[:3]
    refs = refs[3:]
    if rope:
        cos_ref, sin_ref = refs[:2]
        refs = refs[2:]
    if cache == "update":
        refs = refs[2:]
    q0_ref, q1_ref, k_ref, v_ref = refs[:4]
    if cache:
        kc_ref, vc_ref = refs[4:6]
    if cache == "create" and kc_ref.shape[1] > 1:
        kc_ref[:, 1:] = jnp.zeros((kc_ref.shape[0], kc_ref.shape[1] - 1) + kc_ref.shape[2:], F32)
        vc_ref[:, 1:] = jnp.zeros((vc_ref.shape[0], vc_ref.shape[1] - 1) + vc_ref.shape[2:], F32)
    shift, scale = mod_ref[0, 0:1, :], mod_ref[0, 1:2, :]
    h = _modulate(x_ref[...], shift, scale).astype(BF16)
    lane = lax.broadcasted_iota(jnp.int32, (tm, QKV_CHUNK), 1)
    first_half = (lane % V_DIM) < HEAD_DIM
    if rope:
        reps = QKV_CHUNK // V_DIM
        cos = jnp.concatenate([cos_ref[...]] * reps, axis=1)
        sin = jnp.concatenate([sin_ref[...]] * reps, axis=1)
        swap_up = (lane % (2 * N_FREQ)) < N_FREQ

    def rotary(y):
        up = jnp.concatenate([pltpu.roll(y[:, c:c + LANES], LANES - N_FREQ, 1)
                              for c in range(0, QKV_CHUNK, LANES)], axis=1)
        down = jnp.concatenate([pltpu.roll(y[:, c:c + LANES], N_FREQ, 1)
                                for c in range(0, QKV_CHUNK, LANES)], axis=1)
        return y * cos + jnp.where(swap_up, up, down) * sin

    def proj(col):
        return jnp.dot(h, w_ref[:, col:col + QKV_CHUNK], preferred_element_type=F32)

    for c0 in range(0, D_MODEL, QKV_CHUNK):
        cols = slice(c0, c0 + QKV_CHUNK)
        q, k, v = proj(c0), proj(D_MODEL + c0), proj(2 * D_MODEL + c0)
        if rope:
            q, k = rotary(q), rotary(k)
        q = q * Q_SCALE
        q0_ref[:, cols] = jnp.where(first_half, q, 0.0).astype(BF16)
        q1_ref[:, cols] = jnp.where(first_half, 0.0, q).astype(BF16)
        k_ref[:, cols] = k.astype(BF16)
        v_ref[:, cols] = v.astype(BF16)
        if cache:
            kc_ref[:, 0, :, cols] = k.reshape(kc_ref.shape[0], kc_ref.shape[2], QKV_CHUNK)
            vc_ref[:, 0, :, cols] = v.reshape(vc_ref.shape[0], vc_ref.shape[2], QKV_CHUNK)


def _qkv(x, mods, layer, seq_len, mod_row, w_qkv, tm, rope_tables=None, cache=None, slot=0, n_slots=1):
    rows = x.shape[0]
    rope = rope_tables is not None
    with_cache = cache is not None
    tiles_per_seq = max(seq_len // tm, 1)
    in_specs = [
        pl.BlockSpec((tm, D_MODEL), lambda i: (i, 0)),
        pl.BlockSpec((1, 6, D_MODEL), lambda i: (layer * MOD_ROWS + mod_row(i), 0, 0)),
        _layer_spec((D_MODEL, 3 * D_MODEL), layer // 2),
    ]
    args = [x, mods, w_qkv]
    if rope:
        in_specs += [pl.BlockSpec((tm, V_DIM), lambda i: (i % tiles_per_seq, 0))] * 2
        args += list(rope_tables)
    act = jax.ShapeDtypeStruct((rows, D_MODEL), BF16)
    out_shape = [act] * 4
    out_specs = [pl.BlockSpec((tm, D_MODEL), lambda i: (i, 0))] * 4
    aliases, mode = {}, None
    if with_cache:
        seqs = tm // seq_len
        if cache:
            mode = "update"
            in_specs += [pl.BlockSpec(memory_space=pl.ANY)] * 2
            aliases = {len(args): 4, len(args) + 1: 5}
            args += list(cache)
            full = jax.ShapeDtypeStruct(cache[0].shape, F32)
            out_specs += [pl.BlockSpec((seqs, 1, seq_len, D_MODEL), lambda i: (i, slot, 0, 0))] * 2
        else:
            mode = "create"
            full = jax.ShapeDtypeStruct((rows // seq_len, n_slots, seq_len, D_MODEL), F32)
            out_specs += [pl.BlockSpec((seqs, n_slots, seq_len, D_MODEL), lambda i: (i, 0, 0, 0))] * 2
        out_shape += [full, full]
    kern = functools.partial(_qkv_kernel, tm=tm, rope=rope, cache=mode)
    return pl.pallas_call(
        kern,
        out_shape=out_shape,
        grid=(rows // tm,),
        in_specs=in_specs,
        out_specs=out_specs,
        input_output_aliases=aliases,
        compiler_params=_params(("arbitrary",)),
        name="qkv_proj",
    )(*args)


def _attn_kernel(*refs, hb, tq, lam_init, cache, lockstep):
    lam_ref, g_ref, q0_ref, q1_ref, k_ref, v_ref = refs[:6]
    if cache:
        ck_ref, cv_ref = refs[6:8]
    o_ref = refs[-1]
    lv = lam_ref[...]
    lam = (jnp.exp(jnp.sum(lv[0:1] * lv[1:2], axis=-1, keepdims=True))
           - jnp.exp(jnp.sum(lv[2:3] * lv[3:4], axis=-1, keepdims=True)) + lam_init)
    nt = (((1,), (1,)), ((), ()))

    def scores(hh):
        cols = slice(hh * V_DIM, (hh + 1) * V_DIM)
        qq = jnp.concatenate([q0_ref[:, cols], q1_ref[:, cols]], axis=0)
        s = lax.dot_general(qq, k_ref[:, cols], nt, preferred_element_type=F32)
        sc = None
        if cache:
            sc = lax.dot_general(qq, ck_ref[:, cols].astype(BF16), nt, preferred_element_type=F32)
        return s, sc

    def row_max(s, sc):
        m = jnp.max(s, axis=-1, keepdims=True)
        return jnp.maximum(m, jnp.max(sc, axis=-1, keepdims=True)) if cache else m

    def exponentials(s, sc, m):
        return jnp.exp2(s - m), (jnp.exp2(sc - m) if cache else None)

    def row_sum(p, pc):
        l = jnp.sum(p, axis=-1, keepdims=True)
        return l + jnp.sum(pc, axis=-1, keepdims=True) if cache else l

    def weighted_values(hh, p, pc, l):
        cols = slice(hh * V_DIM, (hh + 1) * V_DIM)
        ratio = l[:tq] * lam / l[tq:]
        w = (p[:tq] - p[tq:] * ratio).astype(BF16)
        o = jnp.dot(w, v_ref[:, cols], preferred_element_type=F32)
        if cache:
            wc = (pc[:tq] - pc[tq:] * ratio).astype(BF16)
            o = o + jnp.dot(wc, cv_ref[:, cols].astype(BF16), preferred_element_type=F32)
        return o / l[:tq]

    ones_col = (lax.broadcasted_iota(jnp.int32, (1, V_DIM), 1) == 0).astype(BF16)

    def with_ones(v):
        return jnp.concatenate([v, jnp.broadcast_to(ones_col, v.shape)], axis=1)

    def weighted_values_unnormalised(hh, p, pc):
        cols = slice(hh * V_DIM, (hh + 1) * V_DIM)
        o = jnp.dot(p.astype(BF16), with_ones(v_ref[:, cols]), preferred_element_type=F32)
        if cache:
            o = o + jnp.dot(pc.astype(BF16), with_ones(cv_ref[:, cols].astype(BF16)),
                            preferred_element_type=F32)
        o = o[:, :V_DIM] / o[:, V_DIM:V_DIM + 1]
        return o[:tq] - lam * o[tq:]

    def sub_norm(o):
        return lax.rsqrt(jnp.mean(o * o, axis=-1, keepdims=True) + SUBLN_EPS)

    def emit(hh, o, r):
        o_ref[:, hh * V_DIM:(hh + 1) * V_DIM] = (o * r * g_ref[...] * (1.0 - lam_init)).astype(BF16)

    heads = range(hb)
    if lockstep:
        S = [scores(hh) for hh in heads]
        M = [row_max(*S[hh]) for hh in heads]
        P = [exponentials(*S[hh], M[hh]) for hh in heads]
        L = [row_sum(*P[hh]) for hh in heads]
        O = [weighted_values(hh, *P[hh], L[hh]) for hh in heads]
        R = [sub_norm(O[hh]) for hh in heads]
        for hh in heads:
            emit(hh, O[hh], R[hh])
    else:
        pending = [scores(hh) for hh in range(min(SCORE_LOOKAHEAD, hb))]
        for hh in heads:
            if hh + SCORE_LOOKAHEAD < hb:
                pending.append(scores(hh + SCORE_LOOKAHEAD))
            s, sc = pending.pop(0)
            o = weighted_values_unnormalised(hh, *exponentials(s, sc, row_max(s, sc)))
            emit(hh, o, sub_norm(o))


def _attention(q0, q1, k, v, lam_vecs, subln_g, lam_init, seq_len, tq, hb, cache=None):
    rows = q0.shape[0]
    n_seq = rows // seq_len
    q_steps = seq_len // tq
    wb = hb * V_DIM
    with_cache = cache is not None
    q_spec = pl.BlockSpec((tq, wb), lambda s, h, j: (s * q_steps + j, h))
    kv_spec = pl.BlockSpec((seq_len, wb), lambda s, h, j: (s, h))
    in_specs = [_const_spec((4, HEAD_DIM)), _const_spec((1, V_DIM)), q_spec, q_spec, kv_spec, kv_spec]
    args = [lam_vecs, subln_g, q0, q1, k, v]
    if with_cache:
        ck, cv, slot = cache
        past = ck.shape[2]
        c_spec = pl.BlockSpec((None, None, past, wb), lambda s, h, j: (s, slot, 0, h))
        in_specs += [c_spec, c_spec]
        args += [ck, cv]
    n_keys = seq_len + (cache[0].shape[2] if with_cache else 0)
    lockstep = hb * 2 * tq * n_keys * 4 <= LOCKSTEP_SCORE_BYTES
    kern = functools.partial(_attn_kernel, hb=hb, tq=tq, lam_init=lam_init, cache=with_cache,
                             lockstep=lockstep)
    return pl.pallas_call(
        kern,
        out_shape=jax.ShapeDtypeStruct((rows, D_MODEL), BF16),
        grid=(n_seq, N_HEADS // hb, q_steps),
        in_specs=in_specs,
        out_specs=q_spec,
        compiler_params=_params(("arbitrary", "arbitrary", "arbitrary")),
        name="diff_attention",
    )(*args)


def _rope_tables(n):
    token = lax.broadcasted_iota(jnp.int32, (n, N_FREQ), 0)
    row = (token // GRID_W).astype(F32)
    col = (token % GRID_W).astype(F32)
    inv = 1.0 / (ROPE_THETA ** (jnp.arange(N_FREQ, dtype=F32) / N_FREQ))
    ar = row * inv
    ac = col * inv
    ang = jnp.concatenate([ar, ar, ac, ac], axis=-1)
    sign = jnp.tile(jnp.concatenate([-jnp.ones((N_FREQ,), F32), jnp.ones((N_FREQ,), F32)]), 2)
    cos = jnp.cos(ang)
    sin = jnp.sin(ang) * sign
    return jnp.tile(cos, (1, 2)), jnp.tile(sin, (1, 2))


def _tiles(ctx_len, lat_len):
    return dict(
        ffn_ctx=2 * ctx_len, ffn_lat=512,
        row_ctx=512, row_lat=1024,
        fourier_ctx=ctx_len, fourier_lat=512,
        attn_q_ctx=ctx_len, attn_q_lat=256,
        attn_heads_ctx=N_HEADS, attn_heads_lat=N_HEADS,
    )


def kernel(x_prompt, x_sample, cache_k, cache_v, c, c_ctx, w_ada, b_ada, w_fourier, w_qkv,
           lambda_q1, lambda_k1, lambda_q2, lambda_k2, subln_g, w_o, w_up, conv_w, conv_b,
           w_down, ln1_g, ln1_b, ln2_g, ln2_b):
    n_ctx_seq, ctx_len, d = x_prompt.shape
    n_lat_seq, lat_len, _ = x_sample.shape
    past = cache_k.shape[2]
    assert d == D_MODEL and n_lat_seq + 1 <= MOD_ROWS
    t = _tiles(ctx_len, lat_len)

    cvec = jnp.concatenate(
        [c_ctx[None, :], c, jnp.zeros((MOD_ROWS - 1 - n_lat_seq, d), F32)], axis=0)
    mods = _ada_table(cvec, w_ada, b_ada)

    xc = x_prompt.reshape(n_ctx_seq * ctx_len, d)
    xl = x_sample.reshape(n_lat_seq * lat_len, d)
    ck_in = cache_k.reshape(n_lat_seq, DEPTH // 2, past, d)
    cv_in = cache_v.reshape(n_lat_seq, DEPTH // 2, past, d)

    ctx_row = lambda i: 0
    lat_row_of = lambda tile: (lambda i: 1 + i // (lat_len // tile))
    cs = _channel_dft_table()
    dft_ctx, dft_lat = _dft_table(ctx_len), _dft_table(lat_len)
    rope = _rope_tables(lat_len)
    caches = ()

    wf, wqkv, wo = w_fourier.astype(BF16), w_qkv.astype(BF16), w_o.astype(BF16)
    chan = jnp.arange(D_MODEL)
    flip = (chan // FOURIER_GROUP) * FOURIER_GROUP + (FOURIER_GROUP - chan % FOURIER_GROUP) % FOURIER_GROUP
    wf_flipped = jnp.take(wf, flip, axis=1)
    tq_lat = t["fourier_lat"]
    reverse_rows = (lax.broadcasted_iota(jnp.int32, (tq_lat, tq_lat), 0)
                    + lax.broadcasted_iota(jnp.int32, (tq_lat, tq_lat), 1) == tq_lat - 1).astype(BF16)
    wup, wdn = w_up.astype(BF16), w_down.astype(BF16)
    cw, cb = conv_w, conv_b[:, None, :]
    g1, b1 = ln1_g[:, None, :], ln1_b[:, None, :]
    g2, b2 = ln2_g[:, None, :], ln2_b[:, None, :]

    for i in range(DEPTH):
        j = i // 2
        attn_c = attn_l = None
        if i % 2 == 0:
            xc = _fourier(xc, mods, i, ctx_len, ctx_row, cs, dft_ctx, wf, g1, b1, tq=t["fourier_ctx"])
            xl = _fourier(xl, mods, i, lat_len, lambda s: 1 + s, cs, dft_lat, wf, g1, b1,
                          tq=t["fourier_lat"], mirror=(wf_flipped, reverse_rows))
        else:
            lam_init = 0.8 - 0.6 * math.exp(-0.3 * i)
            lam_vecs = jnp.stack([lambda_q1[j], lambda_k1[j], lambda_q2[j], lambda_k2[j]])
            sg = subln_g[j][None, :]
            q0, q1, k, v, new_k, new_v = _qkv(
                xc, mods, i, ctx_len, ctx_row, wqkv, tm=t["row_ctx"], cache=caches, slot=j,
                n_slots=DEPTH // 2)
            caches = (new_k, new_v)
            oc = _attention(q0, q1, k, v, lam_vecs, sg, lam_init, ctx_len,
                            tq=t["attn_q_ctx"], hb=t["attn_heads_ctx"])
            lat_row = lat_row_of(t["row_lat"])
            q0, q1, k, v = _qkv(xl, mods, i, lat_len, lat_row, wqkv, tm=t["row_lat"], rope_tables=rope)
            ol = _attention(q0, q1, k, v, lam_vecs, sg, lam_init, lat_len,
                            tq=t["attn_q_lat"], hb=t["attn_heads_lat"], cache=(ck_in, cv_in, j))
            attn_c, attn_l = (oc, wo, g1, b1), (ol, wo, g1, b1)
        xc = _ffn(xc, mods, i, ctx_len, ctx_row, wup, cw, cb, wdn, g2, b2, tm=t["ffn_ctx"], attn=attn_c)
        xl = _ffn(xl, mods, i, lat_len, lat_row_of(t["ffn_lat"]), wup, cw, cb, wdn, g2, b2,
                  tm=t["ffn_lat"], attn=attn_l)

    y_prompt = xc.reshape(x_prompt.shape)
    y_sample = xl.reshape(x_sample.shape)
    new_cache_k = new_k.reshape(n_ctx_seq, DEPTH // 2, ctx_len, N_HEADS, 2, HEAD_DIM)
    new_cache_v = new_v.reshape(n_ctx_seq, DEPTH // 2, ctx_len, N_HEADS, V_DIM)
    return (y_prompt, y_sample, new_cache_k, new_cache_v)
```

```python
import functools
import math

import jax
import jax.numpy as jnp
from jax import lax
from jax.experimental import pallas as pl
from jax.experimental.pallas import tpu as pltpu

F32 = jnp.float32
BF16 = jnp.bfloat16

D_MODEL = 1024
DEPTH = 4
GRID_W = 64
N_HEADS = 8
HEAD_DIM = 64
V_DIM = 2 * HEAD_DIM
N_FOURIER_GROUPS = 4
FOURIER_GROUP = D_MODEL // N_FOURIER_GROUPS
D_FF = 2816
ROPE_THETA = 10000.0
N_FREQ = HEAD_DIM // 4
DN_ALPHA = (2 * DEPTH) ** 0.25
LN_EPS = 1e-6
SUBLN_EPS = 1e-5

LANES = 128
SUBLANES = 8
MOD_ROWS = 8
FF_CHUNK = 256
HALO = 16
QKV_CHUNK = 256
VMEM_LIMIT = 56 * 1024 * 1024
LOCKSTEP_SCORE_BYTES = 8 * 1024 * 1024
SCORE_LOOKAHEAD = 2
Q_SCALE = HEAD_DIM ** -0.5 * math.log2(math.e)


def _params(sem, vmem=VMEM_LIMIT):
    return pltpu.CompilerParams(dimension_semantics=sem, vmem_limit_bytes=vmem)


def _ln(x):
    mu = jnp.mean(x, axis=-1, keepdims=True)
    xc = x - mu
    var = jnp.mean(xc * xc, axis=-1, keepdims=True)
    return xc * lax.rsqrt(var + LN_EPS)


def _modulate(x, shift, scale):
    return _ln(x) * (1.0 + scale) + shift


def _post_norm(x, update, g, b):
    return _ln(DN_ALPHA * x + update) * g + b


def _silu(x):
    return x / (1.0 + jnp.exp(-x))


def _const_spec(shape):
    return pl.BlockSpec(shape, lambda *_: (0,) * len(shape))


def _layer_spec(shape, layer):
    return pl.BlockSpec((None,) + tuple(shape), lambda *_: (layer,) + (0,) * len(shape))


def _ada_kernel(c_ref, w_ref, b_ref, o_ref):
    s = _silu(c_ref[...]).astype(BF16)
    a = jnp.dot(s, w_ref[0].astype(BF16), preferred_element_type=F32)
    o_ref[0] = a + b_ref[0]


def _ada_table(cvec, w_ada, b_ada):
    nc = 2 * D_MODEL
    n_col = w_ada.shape[-1] // nc
    out = pl.pallas_call(
        _ada_kernel,
        out_shape=jax.ShapeDtypeStruct((DEPTH, MOD_ROWS, w_ada.shape[-1]), F32),
        grid=(DEPTH, n_col),
        in_specs=[
            pl.BlockSpec((MOD_ROWS, D_MODEL), lambda l, j: (0, 0)),
            pl.BlockSpec((1, D_MODEL, nc), lambda l, j: (l, 0, j)),
            pl.BlockSpec((1, 1, nc), lambda l, j: (l, 0, j)),
        ],
        out_specs=pl.BlockSpec((1, MOD_ROWS, nc), lambda l, j: (l, 0, j)),
        compiler_params=_params(("arbitrary", "arbitrary")),
        name="ada_table",
    )(cvec, w_ada, b_ada.reshape(DEPTH, 1, -1))
    return out.reshape(DEPTH * MOD_ROWS, 6, D_MODEL)


def _ffn_body(*refs, tm, seq_len, attn_proj, layer, first_step):
    halo = seq_len > tm
    x_ref = refs[0]
    if halo:
        xp_ref, xn_ref = refs[1:3]
        refs = refs[2:]
    if attn_proj:
        a_ref = refs[1]
        refs = refs[1:]
        if halo:
            ap_ref, an_ref = refs[1:3]
            refs = refs[2:]
        wo_ref, g1_ref, b1_ref = refs[1:4]
        refs = refs[3:]
    (mod_ref, wup_hbm, cw_ref, cb_ref, wdn_ref, g_ref, b_ref,
     o_ref, slab_ref, hext_ref, act_ref, wup_ref, wup_sem) = refs[1:]

    def chunk_copies(c):
        return [pltpu.make_async_copy(wup_hbm.at[layer, :, pl.ds(col, FF_CHUNK)],
                                      wup_ref.at[:, pl.ds(col, FF_CHUNK)], wup_sem.at[2 * c + k])
                for k, col in enumerate((c * FF_CHUNK, D_FF + c * FF_CHUNK))]

    if first_step:
        for c in range(D_FF // FF_CHUNK):
            for copy in chunk_copies(c):
                copy.start()
    seg = tm // SUBLANES
    n_slab = D_MODEL // LANES
    shift, scale, gate = mod_ref[0, 3:4, :], mod_ref[0, 4:5, :], mod_ref[0, 5:6, :]
    x = x_ref[...]
    if halo:
        xp, xn = xp_ref[...], xn_ref[...]
    if attn_proj:
        gate1 = mod_ref[0, 2:3, :]
        a = a_ref[...]
        if halo:
            a = jnp.concatenate([a, ap_ref[...], an_ref[...]], axis=0)
        m = jnp.dot(a, wo_ref[...], preferred_element_type=F32)
        x = _post_norm(x, gate1 * m[0:tm], g1_ref[...], b1_ref[...])
        if halo:
            xp = _post_norm(xp, gate1 * m[tm + 8:tm + 16], g1_ref[...], b1_ref[...])
            xn = _post_norm(xn, gate1 * m[tm + 16:tm + 24], g1_ref[...], b1_ref[...])
    h = _modulate(x, shift, scale)
    for cb in range(n_slab):
        for s in range(SUBLANES):
            slab_ref[cb, pl.ds(s, seg, stride=SUBLANES), :] = (
                h[s * seg:(s + 1) * seg, cb * LANES:(cb + 1) * LANES])
    for cb in range(n_slab):
        hext_ref[0:tm, cb * LANES:(cb + 1) * LANES] = slab_ref[cb].astype(BF16)
    row = lax.broadcasted_iota(jnp.int32, (SUBLANES, FF_CHUNK), 0)
    if halo:
        tiles_per_seq = seq_len // tm
        pos = pl.program_id(0) % tiles_per_seq
        hp = jnp.where(pos > 0, _modulate(xp, shift, scale), 0.0)
        hn = jnp.where(pos < tiles_per_seq - 1, _modulate(xn, shift, scale), 0.0)
        hext_ref[tm:, :] = jnp.concatenate([hp, hn], axis=0).astype(BF16)
    else:
        assert tm % seq_len == 0 and seq_len % seg == 0
        seq_start = functools.reduce(jnp.logical_or, [row == s for s in range(SUBLANES) if (s * seg) % seq_len == 0])
        seq_end = functools.reduce(jnp.logical_or, [row == s for s in range(SUBLANES) if ((s + 1) * seg) % seq_len == 0])
    hext = hext_ref[...]

    def conv(u, col):
        cw = cw_ref[:, col:col + FF_CHUNK]
        first, last = pltpu.roll(u[tm - 8:tm], 1, 0), pltpu.roll(u[0:8], 7, 0)
        if halo:
            first = jnp.where(row == 0, u[tm + 7:tm + 8], first)
            last = jnp.where(row == 7, u[tm + 8:tm + 9], last)
        else:
            first = jnp.where(seq_start, 0.0, first)
            last = jnp.where(seq_end, 0.0, last)
        prev = jnp.concatenate([first, u[0:tm - 8]], axis=0)
        nxt = jnp.concatenate([u[8:tm], last], axis=0)
        return prev * cw[0:1] + u[0:tm] * cw[1:2] + nxt * cw[2:3] + cb_ref[:, col:col + FF_CHUNK]

    for c in range(D_FF // FF_CHUNK):
        ca, cg = c * FF_CHUNK, D_FF + c * FF_CHUNK
        if first_step:
            for copy in chunk_copies(c):
                copy.wait()
        ua = jnp.dot(hext, wup_ref[:, ca:ca + FF_CHUNK], preferred_element_type=F32)
        ug = jnp.dot(hext, wup_ref[:, cg:cg + FF_CHUNK], preferred_element_type=F32)
        act_ref[:, ca:ca + FF_CHUNK] = (_silu(conv(ua, ca)) * conv(ug, cg)).astype(BF16)
    f = jnp.dot(act_ref[...], wdn_ref[...], preferred_element_type=F32)
    for cb in range(n_slab):
        slab_ref[cb] = f[:, cb * LANES:(cb + 1) * LANES]
    for cb in range(n_slab):
        for s in range(SUBLANES):
            o_ref[s * seg:(s + 1) * seg, cb * LANES:(cb + 1) * LANES] = (
                slab_ref[cb, pl.ds(s, seg, stride=SUBLANES), :])
    o_ref[...] = _post_norm(x, gate * o_ref[...], g_ref[...], b_ref[...])


def _ffn_kernel(*refs, **static):
    first = pl.program_id(0) == 0

    @pl.when(first)
    def _():
        _ffn_body(*refs, first_step=True, **static)

    @pl.when(jnp.logical_not(first))
    def _():
        _ffn_body(*refs, first_step=False, **static)


def _ffn(x, mods, layer, seq_len, mod_row, w_up, conv_w, conv_b, w_down, g, b, tm, attn=None):
    rows = x.shape[0]
    halo = seq_len > tm
    kern = functools.partial(_ffn_kernel, tm=tm, seq_len=seq_len, attn_proj=attn is not None, layer=layer)

    def halo_specs(block_rows):
        per_tile = tm // block_rows
        return [
            pl.BlockSpec((block_rows, D_MODEL), lambda i: (jnp.maximum(i * per_tile - 1, 0), 0)),
            pl.BlockSpec((block_rows, D_MODEL),
                         lambda i: (jnp.minimum((i + 1) * per_tile, rows // block_rows - 1), 0)),
        ]

    in_specs = [pl.BlockSpec((tm, D_MODEL), lambda i: (i, 0))]
    args = [x]
    if halo:
        in_specs += halo_specs(SUBLANES)
        args += [x, x]
    if attn is not None:
        a, w_o, g1, b1 = attn
        in_specs += [pl.BlockSpec((tm, D_MODEL), lambda i: (i, 0))]
        args += [a]
        if halo:
            in_specs += halo_specs(HALO)
            args += [a, a]
        in_specs += [_layer_spec((D_MODEL, D_MODEL), layer // 2),
                     _layer_spec((1, D_MODEL), layer), _layer_spec((1, D_MODEL), layer)]
        args += [w_o, g1, b1]
    in_specs += [
        pl.BlockSpec((1, 6, D_MODEL), lambda i: (layer * MOD_ROWS + mod_row(i), 0, 0)),
        pl.BlockSpec(memory_space=pl.ANY),
        _layer_spec((3, 2 * D_FF), layer),
        _layer_spec((1, 2 * D_FF), layer),
        _layer_spec((D_FF, D_MODEL), layer),
        _layer_spec((1, D_MODEL), layer),
        _layer_spec((1, D_MODEL), layer),
    ]
    args += [mods, w_up, conv_w, conv_b, w_down, g, b]
    return pl.pallas_call(
        kern,
        out_shape=jax.ShapeDtypeStruct(x.shape, F32),
        grid=(rows // tm,),
        in_specs=in_specs,
        out_specs=pl.BlockSpec((tm, D_MODEL), lambda i: (i, 0)),
        scratch_shapes=[
            pltpu.VMEM((D_MODEL // LANES, tm, LANES), F32),
            pltpu.VMEM((tm + (HALO if halo else 0), D_MODEL), BF16),
            pltpu.VMEM((tm, D_FF), BF16),
            pltpu.VMEM((D_MODEL, 2 * D_FF), BF16),
            pltpu.SemaphoreType.DMA((2 * (D_FF // FF_CHUNK),)),
        ],
        compiler_params=_params(("arbitrary",)),
        name="conv_ffn",
    )(*args)


def _fourier_kernel(*refs, n, tq, rows_per_step, mirror):
    x_ref, mod_ref, cs_ref, dc_ref, ds_ref, wf_ref = refs[:6]
    refs = refs[6:]
    if mirror:
        wfm_ref, rev_ref = refs[:2]
        refs = refs[2:]
    g_ref, b_ref, o_ref, y_ref = refs[:4]
    if mirror:
        f_ref = refs[4]
    j = pl.program_id(1)
    steps, half_steps = n // tq, (n // 2) // tq
    shift, scale, gate = mod_ref[0, 0:1, :], mod_ref[0, 1:2, :], mod_ref[0, 2:3, :]

    @pl.when(j == 0)
    def _():
        sign = (1 - 2 * (lax.broadcasted_iota(jnp.int32, (rows_per_step, 1), 0) % 2)).astype(F32)
        nyquist = [jnp.zeros((1, FOURIER_GROUP), F32)] * N_FOURIER_GROUPS
        for r in range(n // rows_per_step):
            r0 = r * rows_per_step
            h = _modulate(x_ref[r0:r0 + rows_per_step, :], shift, scale).astype(BF16)
            for gi in range(N_FOURIER_GROUPS):
                c0 = gi * FOURIER_GROUP
                y = jnp.dot(h[:, c0:c0 + FOURIER_GROUP], cs_ref[...], preferred_element_type=F32)
                y_ref[r0:r0 + rows_per_step, c0:c0 + FOURIER_GROUP] = y[:, :FOURIER_GROUP].astype(BF16)
                y_ref[n + r0:n + r0 + rows_per_step, c0:c0 + FOURIER_GROUP] = y[:, FOURIER_GROUP:].astype(BF16)
                if mirror:
                    nyquist[gi] = nyquist[gi] + jnp.sum(y[:, :FOURIER_GROUP] * sign, axis=0, keepdims=True)
        if mirror:
            row = lax.broadcasted_iota(jnp.int32, (SUBLANES, D_MODEL), 0)
            f_ref[n // 2:, :] = jnp.where(row == 0, jnp.concatenate(nyquist, axis=1) * n ** -0.5, 0.0)

    def finish(m):
        x = x_ref[pl.ds(pl.multiple_of(j * tq, tq), tq), :]
        o_ref[...] = _post_norm(x, gate * m, g_ref[...], b_ref[...])

    def direct():
        return (jnp.dot(dc_ref[...], y_ref[0:n, :], preferred_element_type=F32)
                + jnp.dot(ds_ref[...], y_ref[n:, :], preferred_element_type=F32))

    if not mirror:
        finish(jnp.dot(direct().astype(BF16), wf_ref[...], preferred_element_type=F32))
        return

    @pl.when(j < half_steps)
    def _():
        f = direct()
        f_ref[pl.ds(pl.multiple_of(j * tq, tq), tq), :] = f
        finish(jnp.dot(f.astype(BF16), wf_ref[...], preferred_element_type=F32))

    for jj in range(half_steps, steps):
        @pl.when(j == jj)
        def _(jj=jj):
            top = n - jj * tq
            window = f_ref[top - tq + 1:top + 1, :].astype(BF16)
            f = jnp.dot(rev_ref[...], window, preferred_element_type=F32)
            finish(jnp.dot(f.astype(BF16), wfm_ref[...], preferred_element_type=F32))


def _fourier(x, mods, layer, seq_len, mod_row, cs, dft, w_f, g, b, tq, mirror=None):
    rows = x.shape[0]
    n_seq = rows // seq_len
    steps = seq_len // tq
    half_steps = max((seq_len // 2) // tq, 1)
    use_mirror = mirror is not None
    kern = functools.partial(_fourier_kernel, n=seq_len, tq=tq, rows_per_step=min(seq_len, 512),
                             mirror=use_mirror)
    table_spec = pl.BlockSpec((tq, seq_len), lambda s, j: (jnp.minimum(j, half_steps - 1) if use_mirror else j, 0))
    in_specs = [
        pl.BlockSpec((seq_len, D_MODEL), lambda s, j: (s, 0)),
        pl.BlockSpec((1, 6, D_MODEL), lambda s, j: (layer * MOD_ROWS + mod_row(s), 0, 0)),
        _const_spec((FOURIER_GROUP, 2 * FOURIER_GROUP)),
        table_spec, table_spec,
        _layer_spec((D_MODEL, D_MODEL), layer // 2),
    ]
    args = [x, mods, cs, dft[0], dft[1], w_f]
    scratch = [pltpu.VMEM((2 * seq_len, D_MODEL), BF16)]
    if use_mirror:
        in_specs += [_layer_spec((D_MODEL, D_MODEL), layer // 2), _const_spec((tq, tq))]
        args += list(mirror)
        scratch += [pltpu.VMEM((seq_len // 2 + SUBLANES, D_MODEL), F32)]
    in_specs += [_layer_spec((1, D_MODEL), layer), _layer_spec((1, D_MODEL), layer)]
    args += [g, b]
    return pl.pallas_call(
        kern,
        out_shape=jax.ShapeDtypeStruct(x.shape, F32),
        grid=(n_seq, steps),
        in_specs=in_specs,
        out_specs=pl.BlockSpec((tq, D_MODEL), lambda s, j: (s * steps + j, 0)),
        scratch_shapes=scratch,
        compiler_params=_params(("arbitrary", "arbitrary")),
        name="fourier_mix",
    )(*args)


def _cos_sin(k, t, n):
    ang = ((k * t) % n).astype(F32) * (2.0 * math.pi / n)
    return jnp.cos(ang), jnp.sin(ang)


def _dft_table_kernel(ca_ref, sa_ref, cb_ref, sb_ref, c_ref, s_ref, *, n):
    cb, sb = cb_ref[...], sb_ref[...]
    norm = n ** -0.5
    for i in range(n // LANES):
        ca, sa = ca_ref[:, i:i + 1], sa_ref[:, i:i + 1]
        c_ref[:, i * LANES:(i + 1) * LANES] = ((ca * cb - sa * sb) * norm).astype(BF16)
        s_ref[:, i * LANES:(i + 1) * LANES] = ((sa * cb + ca * sb) * -norm).astype(BF16)


def _dft_table(n):
    n_hi = n // LANES
    k = lax.broadcasted_iota(jnp.int32, (n, n_hi), 0)
    t_hi = lax.broadcasted_iota(jnp.int32, (n, n_hi), 1) * LANES
    ca, sa = _cos_sin(k, t_hi, n)
    k = lax.broadcasted_iota(jnp.int32, (n, LANES), 0)
    t_lo = lax.broadcasted_iota(jnp.int32, (n, LANES), 1)
    cb, sb = _cos_sin(k, t_lo, n)
    tr = min(n, 256)
    hi_spec = pl.BlockSpec((tr, n_hi), lambda i: (i, 0))
    lo_spec = pl.BlockSpec((tr, LANES), lambda i: (i, 0))
    out_spec = pl.BlockSpec((tr, n), lambda i: (i, 0))
    table = jax.ShapeDtypeStruct((n, n), BF16)
    return pl.pallas_call(
        functools.partial(_dft_table_kernel, n=n),
        out_shape=[table, table],
        grid=(n // tr,),
        in_specs=[hi_spec, hi_spec, lo_spec, lo_spec],
        out_specs=[out_spec, out_spec],
        compiler_params=_params(("arbitrary",)),
        name="dft_table",
    )(ca, sa, cb, sb)


def _channel_dft_table():
    n = FOURIER_GROUP
    k = lax.broadcasted_iota(jnp.int32, (n, n), 0)
    t = lax.broadcasted_iota(jnp.int32, (n, n), 1)
    cos, sin = _cos_sin(k, t, n)
    return (jnp.concatenate([cos, sin], axis=1) * (n ** -0.5)).astype(BF16)


def _qkv_kernel(*refs, tm, rope, cache):
    x_ref, mod_ref, w_ref = refs[:3]
    refs = refs[3:]
    if rope:
        cos_ref, sin_ref = refs[:2]
        refs = refs[2:]
    if cache == "update":
        refs = refs[2:]
    q0_ref, q1_ref, k_ref, v_ref = refs[:4]
    if cache:
        kc_ref, vc_ref = refs[4:6]
    if cache == "create" and kc_ref.shape[1] > 1:
        kc_ref[:, 1:] = jnp.zeros((kc_ref.shape[0], kc_ref.shape[1] - 1) + kc_ref.shape[2:], F32)
        vc_ref[:, 1:] = jnp.zeros((vc_ref.shape[0], vc_ref.shape[1] - 1) + vc_ref.shape[2:], F32)
    shift, scale = mod_ref[0, 0:1, :], mod_ref[0, 1:2, :]
    h = _modulate(x_ref[...], shift, scale).astype(BF16)
    lane = lax.broadcasted_iota(jnp.int32, (tm, QKV_CHUNK), 1)
    first_half = (lane % V_DIM) < HEAD_DIM
    if rope:
        reps = QKV_CHUNK // V_DIM
        cos = jnp.concatenate([cos_ref[...]] * reps, axis=1)
        sin = jnp.concatenate([sin_ref[...]] * reps, axis=1)
        swap_up = (lane % (2 * N_FREQ)) < N_FREQ

    def rotary(y):
        up = jnp.concatenate([pltpu.roll(y[:, c:c + LANES], LANES - N_FREQ, 1)
                              for c in range(0, QKV_CHUNK, LANES)], axis=1)
        down = jnp.concatenate([pltpu.roll(y[:, c:c + LANES], N_FREQ, 1)
                                for c in range(0, QKV_CHUNK, LANES)], axis=1)
        return y * cos + jnp.where(swap_up, up, down) * sin

    def proj(col):
        return jnp.dot(h, w_ref[:, col:col + QKV_CHUNK], preferred_element_type=F32)

    for c0 in range(0, D_MODEL, QKV_CHUNK):
        cols = slice(c0, c0 + QKV_CHUNK)
        q, k, v = proj(c0), proj(D_MODEL + c0), proj(2 * D_MODEL + c0)
        if rope:
            q, k = rotary(q), rotary(k)
        q = q * Q_SCALE
        q0_ref[:, cols] = jnp.where(first_half, q, 0.0).astype(BF16)
        q1_ref[:, cols] = jnp.where(first_half, 0.0, q).astype(BF16)
        k_ref[:, cols] = k.astype(BF16)
        v_ref[:, cols] = v.astype(BF16)
        if cache:
            kc_ref[:, 0, :, cols] = k.reshape(kc_ref.shape[0], kc_ref.shape[2], QKV_CHUNK)
            vc_ref[:, 0, :, cols] = v.reshape(vc_ref.shape[0], vc_ref.shape[2], QKV_CHUNK)


def _qkv(x, mods, layer, seq_len, mod_row, w_qkv, tm, rope_tables=None, cache=None, slot=0, n_slots=1):
    rows = x.shape[0]
    rope = rope_tables is not None
    with_cache = cache is not None
    tiles_per_seq = max(seq_len // tm, 1)
    in_specs = [
        pl.BlockSpec((tm, D_MODEL), lambda i: (i, 0)),
        pl.BlockSpec((1, 6, D_MODEL), lambda i: (layer * MOD_ROWS + mod_row(i), 0, 0)),
        _layer_spec((D_MODEL, 3 * D_MODEL), layer // 2),
    ]
    args = [x, mods, w_qkv]
    if rope:
        in_specs += [pl.BlockSpec((tm, V_DIM), lambda i: (i % tiles_per_seq, 0))] * 2
        args += list(rope_tables)
    act = jax.ShapeDtypeStruct((rows, D_MODEL), BF16)
    out_shape = [act] * 4
    out_specs = [pl.BlockSpec((tm, D_MODEL), lambda i: (i, 0))] * 4
    aliases, mode = {}, None
    if with_cache:
        seqs = tm // seq_len
        if cache:
            mode = "update"
            in_specs += [pl.BlockSpec(memory_space=pl.ANY)] * 2
            aliases = {len(args): 4, len(args) + 1: 5}
            args += list(cache)
            full = jax.ShapeDtypeStruct(cache[0].shape, F32)
            out_specs += [pl.BlockSpec((seqs, 1, seq_len, D_MODEL), lambda i: (i, slot, 0, 0))] * 2
        else:
            mode = "create"
            full = jax.ShapeDtypeStruct((rows // seq_len, n_slots, seq_len, D_MODEL), F32)
            out_specs += [pl.BlockSpec((seqs, n_slots, seq_len, D_MODEL), lambda i: (i, 0, 0, 0))] * 2
        out_shape += [full, full]
    kern = functools.partial(_qkv_kernel, tm=tm, rope=rope, cache=mode)
    return pl.pallas_call(
        kern,
        out_shape=out_shape,
        grid=(rows // tm,),
        in_specs=in_specs,
        out_specs=out_specs,
        input_output_aliases=aliases,
        compiler_params=_params(("arbitrary",)),
        name="qkv_proj",
    )(*args)


def _attn_kernel(*refs, hb, tq, lam_init, cache, lockstep):
    lam_ref, g_ref, q0_ref, q1_ref, k_ref, v_ref = refs[:6]
    if cache:
        ck_ref, cv_ref = refs[6:8]
    o_ref = refs[-1]
    lv = lam_ref[...]
    lam = (jnp.exp(jnp.sum(lv[0:1] * lv[1:2], axis=-1, keepdims=True))
           - jnp.exp(jnp.sum(lv[2:3] * lv[3:4], axis=-1, keepdims=True)) + lam_init)
    nt = (((1,), (1,)), ((), ()))

    def scores(hh):
        cols = slice(hh * V_DIM, (hh + 1) * V_DIM)
        qq = jnp.concatenate([q0_ref[:, cols], q1_ref[:, cols]], axis=0)
        s = lax.dot_general(qq, k_ref[:, cols], nt, preferred_element_type=F32)
        sc = None
        if cache:
            sc = lax.dot_general(qq, ck_ref[:, cols].astype(BF16), nt, preferred_element_type=F32)
        return s, sc

    def row_max(s, sc):
        m = jnp.max(s, axis=-1, keepdims=True)
        return jnp.maximum(m, jnp.max(sc, axis=-1, keepdims=True)) if cache else m

    def exponentials(s, sc, m):
        return jnp.exp2(s - m), (jnp.exp2(sc - m) if cache else None)

    def row_sum(p, pc):
        l = jnp.sum(p, axis=-1, keepdims=True)
        return l + jnp.sum(pc, axis=-1, keepdims=True) if cache else l

    def weighted_values(hh, p, pc, l):
        cols = slice(hh * V_DIM, (hh + 1) * V_DIM)
        ratio = l[:tq] * lam / l[tq:]
        w = (p[:tq] - p[tq:] * ratio).astype(BF16)
        o = jnp.dot(w, v_ref[:, cols], preferred_element_type=F32)
        if cache:
            wc = (pc[:tq] - pc[tq:] * ratio).astype(BF16)
            o = o + jnp.dot(wc, cv_ref[:, cols].astype(BF16), preferred_element_type=F32)
        return o / l[:tq]

    ones_col = (lax.broadcasted_iota(jnp.int32, (1, V_DIM), 1) == 0).astype(BF16)

    def with_ones(v):
        return jnp.concatenate([v, jnp.broadcast_to(ones_col, v.shape)], axis=1)

    def weighted_values_unnormalised(hh, p, pc):
        cols = slice(hh * V_DIM, (hh + 1) * V_DIM)
        o = jnp.dot(p.astype(BF16), with_ones(v_ref[:, cols]), preferred_element_type=F32)
        if cache:
            o = o + jnp.dot(pc.astype(BF16), with_ones(cv_ref[:, cols].astype(BF16)),
                            preferred_element_type=F32)
        o = o[:, :V_DIM] / o[:, V_DIM:V_DIM + 1]
        return o[:tq] - lam * o[tq:]

    def sub_norm(o):
        return lax.rsqrt(jnp.mean(o * o, axis=-1, keepdims=True) + SUBLN_EPS)

    def emit(hh, o, r):
        o_ref[:, hh * V_DIM:(hh + 1) * V_DIM] = (o * r * g_ref[...] * (1.0 - lam_init)).astype(BF16)

    heads = range(hb)
    if lockstep:
        S = [scores(hh) for hh in heads]
        M = [row_max(*S[hh]) for hh in heads]
        P = [exponentials(*S[hh], M[hh]) for hh in heads]
        L = [row_sum(*P[hh]) for hh in heads]
        O = [weighted_values(hh, *P[hh], L[hh]) for hh in heads]
        R = [sub_norm(O[hh]) for hh in heads]
        for hh in heads:
            emit(hh, O[hh], R[hh])
    else:
        pending = [scores(hh) for hh in range(min(SCORE_LOOKAHEAD, hb))]
        for hh in heads:
            if hh + SCORE_LOOKAHEAD < hb:
                pending.append(scores(hh + SCORE_LOOKAHEAD))
            s, sc = pending.pop(0)
            o = weighted_values_unnormalised(hh, *exponentials(s, sc, row_max(s, sc)))
            emit(hh, o, sub_norm(o))


def _attention(q0, q1, k, v, lam_vecs, subln_g, lam_init, seq_len, tq, hb, cache=None):
    rows = q0.shape[0]
    n_seq = rows // seq_len
    q_steps = seq_len // tq
    wb = hb * V_DIM
    with_cache = cache is not None
    q_spec = pl.BlockSpec((tq, wb), lambda s, h, j: (s * q_steps + j, h))
    kv_spec = pl.BlockSpec((seq_len, wb), lambda s, h, j: (s, h))
    in_specs = [_const_spec((4, HEAD_DIM)), _const_spec((1, V_DIM)), q_spec, q_spec, kv_spec, kv_spec]
    args = [lam_vecs, subln_g, q0, q1, k, v]
    if with_cache:
        ck, cv, slot = cache
        past = ck.shape[2]
        c_spec = pl.BlockSpec((None, None, past, wb), lambda s, h, j: (s, slot, 0, h))
        in_specs += [c_spec, c_spec]
        args += [ck, cv]
    n_keys = seq_len + (cache[0].shape[2] if with_cache else 0)
    lockstep = hb * 2 * tq * n_keys * 4 <= LOCKSTEP_SCORE_BYTES
    kern = functools.partial(_attn_kernel, hb=hb, tq=tq, lam_init=lam_init, cache=with_cache,
                             lockstep=lockstep)
    return pl.pallas_call(
        kern,
        out_shape=jax.ShapeDtypeStruct((rows, D_MODEL), BF16),
        grid=(n_seq, N_HEADS // hb, q_steps),
        in_specs=in_specs,
        out_specs=q_spec,
        compiler_params=_params(("arbitrary", "arbitrary", "arbitrary")),
        name="diff_attention",
    )(*args)


def _rope_tables(n):
    token = lax.broadcasted_iota(jnp.int32, (n, N_FREQ), 0)
    row = (token // GRID_W).astype(F32)
    col = (token % GRID_W).astype(F32)
    inv = 1.0 / (ROPE_THETA ** (jnp.arange(N_FREQ, dtype=F32) / N_FREQ))
    ar = row * inv
    ac = col * inv
    ang = jnp.concatenate([ar, ar, ac, ac], axis=-1)
    sign = jnp.tile(jnp.concatenate([-jnp.ones((N_FREQ,), F32), jnp.ones((N_FREQ,), F32)]), 2)
    cos = jnp.cos(ang)
    sin = jnp.sin(ang) * sign
    return jnp.tile(cos, (1, 2)), jnp.tile(sin, (1, 2))


def _tiles(ctx_len, lat_len):
    return dict(
        ffn_ctx=2 * ctx_len, ffn_lat=512,
        row_ctx=512, row_lat=1024,
        fourier_ctx=ctx_len, fourier_lat=512,
        attn_q_ctx=ctx_len, attn_q_lat=256,
        attn_heads_ctx=N_HEADS, attn_heads_lat=N_HEADS,
    )


def kernel(x_prompt, x_sample, cache_k, cache_v, c, c_ctx, w_ada, b_ada, w_fourier, w_qkv,
           lambda_q1, lambda_k1, lambda_q2, lambda_k2, subln_g, w_o, w_up, conv_w, conv_b,
           w_down, ln1_g, ln1_b, ln2_g, ln2_b):
    n_ctx_seq, ctx_len, d = x_prompt.shape
    n_lat_seq, lat_len, _ = x_sample.shape
    past = cache_k.shape[2]
    assert d == D_MODEL and n_lat_seq + 1 <= MOD_ROWS
    t = _tiles(ctx_len, lat_len)

    cvec = jnp.concatenate(
        [c_ctx[None, :], c, jnp.zeros((MOD_ROWS - 1 - n_lat_seq, d), F32)], axis=0)
    mods = _ada_table(cvec, w_ada, b_ada)

    xc = x_prompt.reshape(n_ctx_seq * ctx_len, d)
    xl = x_sample.reshape(n_lat_seq * lat_len, d)
    ck_in = cache_k.reshape(n_lat_seq, DEPTH // 2, past, d)
    cv_in = cache_v.reshape(n_lat_seq, DEPTH // 2, past, d)

    ctx_row = lambda i: 0
    lat_row_of = lambda tile: (lambda i: 1 + i // (lat_len // tile))
    cs = _channel_dft_table()
    dft_ctx, dft_lat = _dft_table(ctx_len), _dft_table(lat_len)
    rope = _rope_tables(lat_len)
    caches = ()

    wf, wqkv, wo = w_fourier.astype(BF16), w_qkv.astype(BF16), w_o.astype(BF16)
    chan = jnp.arange(D_MODEL)
    flip = (chan // FOURIER_GROUP) * FOURIER_GROUP + (FOURIER_GROUP - chan % FOURIER_GROUP) % FOURIER_GROUP
    wf_flipped = jnp.take(wf, flip, axis=1)
    tq_lat = t["fourier_lat"]
    reverse_rows = (lax.broadcasted_iota(jnp.int32, (tq_lat, tq_lat), 0)
                    + lax.broadcasted_iota(jnp.int32, (tq_lat, tq_lat), 1) == tq_lat - 1).astype(BF16)
    wup, wdn = w_up.astype(BF16), w_down.astype(BF16)
    cw, cb = conv_w, conv_b[:, None, :]
    g1, b1 = ln1_g[:, None, :], ln1_b[:, None, :]
    g2, b2 = ln2_g[:, None, :], ln2_b[:, None, :]

    for i in range(DEPTH):
        j = i // 2
        attn_c = attn_l = None
        if i % 2 == 0:
            xc = _fourier(xc, mods, i, ctx_len, ctx_row, cs, dft_ctx, wf, g1, b1, tq=t["fourier_ctx"])
            xl = _fourier(xl, mods, i, lat_len, lambda s: 1 + s, cs, dft_lat, wf, g1, b1,
                          tq=t["fourier_lat"], mirror=(wf_flipped, reverse_rows))
        else:
            lam_init = 0.8 - 0.6 * math.exp(-0.3 * i)
            lam_vecs = jnp.stack([lambda_q1[j], lambda_k1[j], lambda_q2[j], lambda_k2[j]])
            sg = subln_g[j][None, :]
            q0, q1, k, v, new_k, new_v = _qkv(
                xc, mods, i, ctx_len, ctx_row, wqkv, tm=t["row_ctx"], cache=caches, slot=j,
                n_slots=DEPTH // 2)
            caches = (new_k, new_v)
            oc = _attention(q0, q1, k, v, lam_vecs, sg, lam_init, ctx_len,
                            tq=t["attn_q_ctx"], hb=t["attn_heads_ctx"])
            lat_row = lat_row_of(t["row_lat"])
            q0, q1, k, v = _qkv(xl, mods, i, lat_len, lat_row, wqkv, tm=t["row_lat"], rope_tables=rope)
            ol = _attention(q0, q1, k, v, lam_vecs, sg, lam_init, lat_len,
                            tq=t["attn_q_lat"], hb=t["attn_heads_lat"], cache=(ck_in, cv_in, j))
            attn_c, attn_l = (oc, wo, g1, b1), (ol, wo, g1, b1)
        xc = _ffn(xc, mods, i, ctx_len, ctx_row, wup, cw, cb, wdn, g2, b2, tm=t["ffn_ctx"], attn=attn_c)
        xl = _ffn(xl, mods, i, lat_len, lat_row_of(t["ffn_lat"]), wup, cw, cb, wdn, g2, b2,
                  tm=t["ffn_lat"], attn=attn_l)

    y_prompt = xc.reshape(x_prompt.shape)
    y_sample = xl.reshape(x_sample.shape)
    new_cache_k = new_k.reshape(n_ctx_seq, DEPTH // 2, ctx_len, N_HEADS, 2, HEAD_DIM)
    new_cache_v = new_v.reshape(n_ctx_seq, DEPTH // 2, ctx_len, N_HEADS, V_DIM)
    return (y_prompt, y_sample, new_cache_k, new_cache_v)
```

```python
import functools
import math

import jax
import jax.numpy as jnp
from jax import lax
from jax.experimental import pallas as pl
from jax.experimental.pallas import tpu as pltpu

F32 = jnp.float32
BF16 = jnp.bfloat16

D_MODEL = 1024
DEPTH = 4
GRID_W = 64
N_HEADS = 8
HEAD_DIM = 64
V_DIM = 2 * HEAD_DIM
N_FOURIER_GROUPS = 4
FOURIER_GROUP = D_MODEL // N_FOURIER_GROUPS
D_FF = 2816
ROPE_THETA = 10000.0
N_FREQ = HEAD_DIM // 4
DN_ALPHA = (2 * DEPTH) ** 0.25
LN_EPS = 1e-6
SUBLN_EPS = 1e-5

LANES = 128
SUBLANES = 8
MOD_ROWS = 8
FF_CHUNK = 256
HALO = 16
QKV_CHUNK = 256
VMEM_LIMIT = 56 * 1024 * 1024
LOCKSTEP_SCORE_BYTES = 8 * 1024 * 1024
SCORE_LOOKAHEAD = 2
Q_SCALE = HEAD_DIM ** -0.5 * math.log2(math.e)


def _params(sem, vmem=VMEM_LIMIT):
    return pltpu.CompilerParams(dimension_semantics=sem, vmem_limit_bytes=vmem)


def _ln(x):
    mu = jnp.mean(x, axis=-1, keepdims=True)
    xc = x - mu
    var = jnp.mean(xc * xc, axis=-1, keepdims=True)
    return xc * lax.rsqrt(var + LN_EPS)


def _modulate(x, shift, scale):
    return _ln(x) * (1.0 + scale) + shift


def _post_norm(x, update, g, b):
    return _ln(DN_ALPHA * x + update) * g + b


def _silu(x):
    return x / (1.0 + jnp.exp(-x))


def _const_spec(shape):
    return pl.BlockSpec(shape, lambda *_: (0,) * len(shape))


def _layer_spec(shape, layer):
    return pl.BlockSpec((None,) + tuple(shape), lambda *_: (layer,) + (0,) * len(shape),
                        pipeline_mode=pl.Buffered(1))


def _ada_kernel(c_ref, w_ref, b_ref, o_ref):
    s = _silu(c_ref[...]).astype(BF16)
    a = jnp.dot(s, w_ref[0].astype(BF16), preferred_element_type=F32)
    o_ref[0] = a + b_ref[0]


def _ada_table(cvec, w_ada, b_ada):
    nc = 2 * D_MODEL
    n_col = w_ada.shape[-1] // nc
    out = pl.pallas_call(
        _ada_kernel,
        out_shape=jax.ShapeDtypeStruct((DEPTH, MOD_ROWS, w_ada.shape[-1]), F32),
        grid=(DEPTH, n_col),
        in_specs=[
            pl.BlockSpec((MOD_ROWS, D_MODEL), lambda l, j: (0, 0)),
            pl.BlockSpec((1, D_MODEL, nc), lambda l, j: (l, 0, j)),
            pl.BlockSpec((1, 1, nc), lambda l, j: (l, 0, j)),
        ],
        out_specs=pl.BlockSpec((1, MOD_ROWS, nc), lambda l, j: (l, 0, j)),
        compiler_params=_params(("arbitrary", "arbitrary")),
        name="ada_table",
    )(cvec, w_ada, b_ada.reshape(DEPTH, 1, -1))
    return out.reshape(DEPTH * MOD_ROWS, 6, D_MODEL)


def _ffn_kernel(*refs, tm, seq_len, attn_proj):
    halo = seq_len > tm
    x_ref = refs[0]
    if halo:
        xp_ref, xn_ref = refs[1:3]
        refs = refs[2:]
    if attn_proj:
        a_ref = refs[1]
        refs = refs[1:]
        if halo:
            ap_ref, an_ref = refs[1:3]
            refs = refs[2:]
        wo_ref, g1_ref, b1_ref = refs[1:4]
        refs = refs[3:]
    (mod_ref, wup_ref, cw_ref, cb_ref, wdn_ref, g_ref, b_ref,
     o_ref, slab_ref, hext_ref, act_ref) = refs[1:]
    seg = tm // SUBLANES
    n_slab = D_MODEL // LANES
    shift, scale, gate = mod_ref[0, 3:4, :], mod_ref[0, 4:5, :], mod_ref[0, 5:6, :]
    x = x_ref[...]
    if halo:
        xp, xn = xp_ref[...], xn_ref[...]
    if attn_proj:
        gate1 = mod_ref[0, 2:3, :]
        a = a_ref[...]
        if halo:
            a = jnp.concatenate([a, ap_ref[...], an_ref[...]], axis=0)
        m = jnp.dot(a, wo_ref[...], preferred_element_type=F32)
        x = _post_norm(x, gate1 * m[0:tm], g1_ref[...], b1_ref[...])
        if halo:
            xp = _post_norm(xp, gate1 * m[tm + 8:tm + 16], g1_ref[...], b1_ref[...])
            xn = _post_norm(xn, gate1 * m[tm + 16:tm + 24], g1_ref[...], b1_ref[...])
    h = _modulate(x, shift, scale)
    for cb in range(n_slab):
        for s in range(SUBLANES):
            slab_ref[cb, pl.ds(s, seg, stride=SUBLANES), :] = (
                h[s * seg:(s + 1) * seg, cb * LANES:(cb + 1) * LANES])
    for cb in range(n_slab):
        hext_ref[0:tm, cb * LANES:(cb + 1) * LANES] = slab_ref[cb].astype(BF16)
    row = lax.broadcasted_iota(jnp.int32, (SUBLANES, FF_CHUNK), 0)
    if halo:
        tiles_per_seq = seq_len // tm
        pos = pl.program_id(0) % tiles_per_seq
        hp = jnp.where(pos > 0, _modulate(xp, shift, scale), 0.0)
        hn = jnp.where(pos < tiles_per_seq - 1, _modulate(xn, shift, scale), 0.0)
        hext_ref[tm:, :] = jnp.concatenate([hp, hn], axis=0).astype(BF16)
    else:
        assert tm % seq_len == 0 and seq_len % seg == 0
        seq_start = functools.reduce(jnp.logical_or, [row == s for s in range(SUBLANES) if (s * seg) % seq_len == 0])
        seq_end = functools.reduce(jnp.logical_or, [row == s for s in range(SUBLANES) if ((s + 1) * seg) % seq_len == 0])
    hext = hext_ref[...]

    def conv(u, col):
        cw = cw_ref[:, col:col + FF_CHUNK]
        first, last = pltpu.roll(u[tm - 8:tm], 1, 0), pltpu.roll(u[0:8], 7, 0)
        if halo:
            first = jnp.where(row == 0, u[tm + 7:tm + 8], first)
            last = jnp.where(row == 7, u[tm + 8:tm + 9], last)
        else:
            first = jnp.where(seq_start, 0.0, first)
            last = jnp.where(seq_end, 0.0, last)
        prev = jnp.concatenate([first, u[0:tm - 8]], axis=0)
        nxt = jnp.concatenate([u[8:tm], last], axis=0)
        return prev * cw[0:1] + u[0:tm] * cw[1:2] + nxt * cw[2:3] + cb_ref[:, col:col + FF_CHUNK]

    for c in range(D_FF // FF_CHUNK):
        ca, cg = c * FF_CHUNK, D_FF + c * FF_CHUNK
        ua = jnp.dot(hext, wup_ref[:, ca:ca + FF_CHUNK], preferred_element_type=F32)
        ug = jnp.dot(hext, wup_ref[:, cg:cg + FF_CHUNK], preferred_element_type=F32)
        act_ref[:, ca:ca + FF_CHUNK] = (_silu(conv(ua, ca)) * conv(ug, cg)).astype(BF16)
    f = jnp.dot(act_ref[...], wdn_ref[...], preferred_element_type=F32)
    for cb in range(n_slab):
        slab_ref[cb] = f[:, cb * LANES:(cb + 1) * LANES]
    for cb in range(n_slab):
        for s in range(SUBLANES):
            o_ref[s * seg:(s + 1) * seg, cb * LANES:(cb + 1) * LANES] = (
                slab_ref[cb, pl.ds(s, seg, stride=SUBLANES), :])
    o_ref[...] = _post_norm(x, gate * o_ref[...], g_ref[...], b_ref[...])


def _ffn(x, mods, layer, seq_len, mod_row, w_up, conv_w, conv_b, w_down, g, b, tm, attn=None):
    rows = x.shape[0]
    halo = seq_len > tm
    kern = functools.partial(_ffn_kernel, tm=tm, seq_len=seq_len, attn_proj=attn is not None)

    def halo_specs(block_rows):
        per_tile = tm // block_rows
        return [
            pl.BlockSpec((block_rows, D_MODEL), lambda i: (jnp.maximum(i * per_tile - 1, 0), 0)),
            pl.BlockSpec((block_rows, D_MODEL),
                         lambda i: (jnp.minimum((i + 1) * per_tile, rows // block_rows - 1), 0)),
        ]

    in_specs = [pl.BlockSpec((tm, D_MODEL), lambda i: (i, 0))]
    args = [x]
    if halo:
        in_specs += halo_specs(SUBLANES)
        args += [x, x]
    if attn is not None:
        a, w_o, g1, b1 = attn
        in_specs += [pl.BlockSpec((tm, D_MODEL), lambda i: (i, 0))]
        args += [a]
        if halo:
            in_specs += halo_specs(HALO)
            args += [a, a]
        in_specs += [_layer_spec((D_MODEL, D_MODEL), layer // 2),
                     _layer_spec((1, D_MODEL), layer), _layer_spec((1, D_MODEL), layer)]
        args += [w_o, g1, b1]
    in_specs += [
        pl.BlockSpec((1, 6, D_MODEL), lambda i: (layer * MOD_ROWS + mod_row(i), 0, 0)),
        _layer_spec((D_MODEL, 2 * D_FF), layer),
        _layer_spec((3, 2 * D_FF), layer),
        _layer_spec((1, 2 * D_FF), layer),
        _layer_spec((D_FF, D_MODEL), layer),
        _layer_spec((1, D_MODEL), layer),
        _layer_spec((1, D_MODEL), layer),
    ]
    args += [mods, w_up, conv_w, conv_b, w_down, g, b]
    return pl.pallas_call(
        kern,
        out_shape=jax.ShapeDtypeStruct(x.shape, F32),
        grid=(rows // tm,),
        in_specs=in_specs,
        out_specs=pl.BlockSpec((tm, D_MODEL), lambda i: (i, 0)),
        scratch_shapes=[
            pltpu.VMEM((D_MODEL // LANES, tm, LANES), F32),
            pltpu.VMEM((tm + (HALO if halo else 0), D_MODEL), BF16),
            pltpu.VMEM((tm, D_FF), BF16),
        ],
        compiler_params=_params(("arbitrary",)),
        name="conv_ffn",
    )(*args)


def _fourier_kernel(*refs, n, tq, rows_per_step, mirror):
    x_ref, mod_ref, cs_ref, dc_ref, ds_ref, wf_ref = refs[:6]
    refs = refs[6:]
    if mirror:
        wfm_ref, rev_ref = refs[:2]
        refs = refs[2:]
    g_ref, b_ref, o_ref, y_ref = refs[:4]
    if mirror:
        f_ref = refs[4]
    j = pl.program_id(1)
    steps, half_steps = n // tq, (n // 2) // tq
    shift, scale, gate = mod_ref[0, 0:1, :], mod_ref[0, 1:2, :], mod_ref[0, 2:3, :]

    @pl.when(j == 0)
    def _():
        sign = (1 - 2 * (lax.broadcasted_iota(jnp.int32, (rows_per_step, 1), 0) % 2)).astype(F32)
        nyquist = [jnp.zeros((1, FOURIER_GROUP), F32)] * N_FOURIER_GROUPS
        for r in range(n // rows_per_step):
            r0 = r * rows_per_step
            h = _modulate(x_ref[r0:r0 + rows_per_step, :], shift, scale).astype(BF16)
            for gi in range(N_FOURIER_GROUPS):
                c0 = gi * FOURIER_GROUP
                y = jnp.dot(h[:, c0:c0 + FOURIER_GROUP], cs_ref[...], preferred_element_type=F32)
                y_ref[r0:r0 + rows_per_step, c0:c0 + FOURIER_GROUP] = y[:, :FOURIER_GROUP].astype(BF16)
                y_ref[n + r0:n + r0 + rows_per_step, c0:c0 + FOURIER_GROUP] = y[:, FOURIER_GROUP:].astype(BF16)
                if mirror:
                    nyquist[gi] = nyquist[gi] + jnp.sum(y[:, :FOURIER_GROUP] * sign, axis=0, keepdims=True)
        if mirror:
            row = lax.broadcasted_iota(jnp.int32, (SUBLANES, D_MODEL), 0)
            f_ref[n // 2:, :] = jnp.where(row == 0, jnp.concatenate(nyquist, axis=1) * n ** -0.5, 0.0)

    def finish(m):
        x = x_ref[pl.ds(pl.multiple_of(j * tq, tq), tq), :]
        o_ref[...] = _post_norm(x, gate * m, g_ref[...], b_ref[...])

    def direct():
        return (jnp.dot(dc_ref[...], y_ref[0:n, :], preferred_element_type=F32)
                + jnp.dot(ds_ref[...], y_ref[n:, :], preferred_element_type=F32))

    if not mirror:
        finish(jnp.dot(direct().astype(BF16), wf_ref[...], preferred_element_type=F32))
        return

    @pl.when(j < half_steps)
    def _():
        f = direct()
        f_ref[pl.ds(pl.multiple_of(j * tq, tq), tq), :] = f
        finish(jnp.dot(f.astype(BF16), wf_ref[...], preferred_element_type=F32))

    for jj in range(half_steps, steps):
        @pl.when(j == jj)
        def _(jj=jj):
            top = n - jj * tq
            window = f_ref[top - tq + 1:top + 1, :].astype(BF16)
            f = jnp.dot(rev_ref[...], window, preferred_element_type=F32)
            finish(jnp.dot(f.astype(BF16), wfm_ref[...], preferred_element_type=F32))


def _fourier(x, mods, layer, seq_len, mod_row, cs, dft, w_f, g, b, tq, mirror=None):
    rows = x.shape[0]
    n_seq = rows // seq_len
    steps = seq_len // tq
    half_steps = max((seq_len // 2) // tq, 1)
    use_mirror = mirror is not None
    kern = functools.partial(_fourier_kernel, n=seq_len, tq=tq, rows_per_step=min(seq_len, 512),
                             mirror=use_mirror)
    table_spec = pl.BlockSpec((tq, seq_len), lambda s, j: (jnp.minimum(j, half_steps - 1) if use_mirror else j, 0))
    in_specs = [
        pl.BlockSpec((seq_len, D_MODEL), lambda s, j: (s, 0)),
        pl.BlockSpec((1, 6, D_MODEL), lambda s, j: (layer * MOD_ROWS + mod_row(s), 0, 0)),
        _const_spec((FOURIER_GROUP, 2 * FOURIER_GROUP)),
        table_spec, table_spec,
        _layer_spec((D_MODEL, D_MODEL), layer // 2),
    ]
    args = [x, mods, cs, dft[0], dft[1], w_f]
    scratch = [pltpu.VMEM((2 * seq_len, D_MODEL), BF16)]
    if use_mirror:
        in_specs += [_layer_spec((D_MODEL, D_MODEL), layer // 2), _const_spec((tq, tq))]
        args += list(mirror)
        scratch += [pltpu.VMEM((seq_len // 2 + SUBLANES, D_MODEL), F32)]
    in_specs += [_layer_spec((1, D_MODEL), layer), _layer_spec((1, D_MODEL), layer)]
    args += [g, b]
    return pl.pallas_call(
        kern,
        out_shape=jax.ShapeDtypeStruct(x.shape, F32),
        grid=(n_seq, steps),
        in_specs=in_specs,
        out_specs=pl.BlockSpec((tq, D_MODEL), lambda s, j: (s * steps + j, 0)),
        scratch_shapes=scratch,
        compiler_params=_params(("arbitrary", "arbitrary")),
        name="fourier_mix",
    )(*args)


def _cos_sin(k, t, n):
    ang = ((k * t) % n).astype(F32) * (2.0 * math.pi / n)
    return jnp.cos(ang), jnp.sin(ang)


def _dft_table_kernel(ca_ref, sa_ref, cb_ref, sb_ref, c_ref, s_ref, *, n):
    cb, sb = cb_ref[...], sb_ref[...]
    norm = n ** -0.5
    for i in range(n // LANES):
        ca, sa = ca_ref[:, i:i + 1], sa_ref[:, i:i + 1]
        c_ref[:, i * LANES:(i + 1) * LANES] = ((ca * cb - sa * sb) * norm).astype(BF16)
        s_ref[:, i * LANES:(i + 1) * LANES] = ((sa * cb + ca * sb) * -norm).astype(BF16)


def _dft_table(n):
    n_hi = n // LANES
    k = lax.broadcasted_iota(jnp.int32, (n, n_hi), 0)
    t_hi = lax.broadcasted_iota(jnp.int32, (n, n_hi), 1) * LANES
    ca, sa = _cos_sin(k, t_hi, n)
    k = lax.broadcasted_iota(jnp.int32, (n, LANES), 0)
    t_lo = lax.broadcasted_iota(jnp.int32, (n, LANES), 1)
    cb, sb = _cos_sin(k, t_lo, n)
    tr = min(n, 256)
    hi_spec = pl.BlockSpec((tr, n_hi), lambda i: (i, 0))
    lo_spec = pl.BlockSpec((tr, LANES), lambda i: (i, 0))
    out_spec = pl.BlockSpec((tr, n), lambda i: (i, 0))
    table = jax.ShapeDtypeStruct((n, n), BF16)
    return pl.pallas_call(
        functools.partial(_dft_table_kernel, n=n),
        out_shape=[table, table],
        grid=(n // tr,),
        in_specs=[hi_spec, hi_spec, lo_spec, lo_spec],
        out_specs=[out_spec, out_spec],
        compiler_params=_params(("arbitrary",)),
        name="dft_table",
    )(ca, sa, cb, sb)


def _channel_dft_table():
    n = FOURIER_GROUP
    k = lax.broadcasted_iota(jnp.int32, (n, n), 0)
    t = lax.broadcasted_iota(jnp.int32, (n, n), 1)
    cos, sin = _cos_sin(k, t, n)
    return (jnp.concatenate([cos, sin], axis=1) * (n ** -0.5)).astype(BF16)


def _qkv_kernel(*refs, tm, rope, cache):
    x_ref, mod_ref, w_ref = refs[:3]
    refs = refs[3:]
    if rope:
        cos_ref, sin_ref = refs[:2]
        refs = refs[2:]
    if cache == "update":
        refs = refs[2:]
    q0_ref, q1_ref, k_ref, v_ref = refs[:4]
    if cache:
        kc_ref, vc_ref = refs[4:6]
    if cache == "create" and kc_ref.shape[1] > 1:
        kc_ref[:, 1:] = jnp.zeros((kc_ref.shape[0], kc_ref.shape[1] - 1) + kc_ref.shape[2:], F32)
        vc_ref[:, 1:] = jnp.zeros((vc_ref.shape[0], vc_ref.shape[1] - 1) + vc_ref.shape[2:], F32)
    shift, scale = mod_ref[0, 0:1, :], mod_ref[0, 1:2, :]
    h = _modulate(x_ref[...], shift, scale).astype(BF16)
    lane = lax.broadcasted_iota(jnp.int32, (tm, QKV_CHUNK), 1)
    first_half = (lane % V_DIM) < HEAD_DIM
    if rope:
        reps = QKV_CHUNK // V_DIM
        cos = jnp.concatenate([cos_ref[...]] * reps, axis=1)
        sin = jnp.concatenate([sin_ref[...]] * reps, axis=1)
        swap_up = (lane % (2 * N_FREQ)) < N_FREQ

    def rotary(y):
        up = jnp.concatenate([pltpu.roll(y[:, c:c + LANES], LANES - N_FREQ, 1)
                              for c in range(0, QKV_CHUNK, LANES)], axis=1)
        down = jnp.concatenate([pltpu.roll(y[:, c:c + LANES], N_FREQ, 1)
                                for c in range(0, QKV_CHUNK, LANES)], axis=1)
        return y * cos + jnp.where(swap_up, up, down) * sin

    def proj(col):
        return jnp.dot(h, w_ref[:, col:col + QKV_CHUNK], preferred_element_type=F32)

    for c0 in range(0, D_MODEL, QKV_CHUNK):
        cols = slice(c0, c0 + QKV_CHUNK)
        q, k, v = proj(c0), proj(D_MODEL + c0), proj(2 * D_MODEL + c0)
        if rope:
            q, k = rotary(q), rotary(k)
        q = q * Q_SCALE
        q0_ref[:, cols] = jnp.where(first_half, q, 0.0).astype(BF16)
        q1_ref[:, cols] = jnp.where(first_half, 0.0, q).astype(BF16)
        k_ref[:, cols] = k.astype(BF16)
        v_ref[:, cols] = v.astype(BF16)
        if cache:
            kc_ref[:, 0, :, cols] = k.reshape(kc_ref.shape[0], kc_ref.shape[2], QKV_CHUNK)
            vc_ref[:, 0, :, cols] = v.reshape(vc_ref.shape[0], vc_ref.shape[2], QKV_CHUNK)


def _qkv(x, mods, layer, seq_len, mod_row, w_qkv, tm, rope_tables=None, cache=None, slot=0, n_slots=1):
    rows = x.shape[0]
    rope = rope_tables is not None
    with_cache = cache is not None
    tiles_per_seq = max(seq_len // tm, 1)
    in_specs = [
        pl.BlockSpec((tm, D_MODEL), lambda i: (i, 0)),
        pl.BlockSpec((1, 6, D_MODEL), lambda i: (layer * MOD_ROWS + mod_row(i), 0, 0)),
        _layer_spec((D_MODEL, 3 * D_MODEL), layer // 2),
    ]
    args = [x, mods, w_qkv]
    if rope:
        in_specs += [pl.BlockSpec((tm, V_DIM), lambda i: (i % tiles_per_seq, 0))] * 2
        args += list(rope_tables)
    act = jax.ShapeDtypeStruct((rows, D_MODEL), BF16)
    out_shape = [act] * 4
    out_specs = [pl.BlockSpec((tm, D_MODEL), lambda i: (i, 0))] * 4
    aliases, mode = {}, None
    if with_cache:
        seqs = tm // seq_len
        if cache:
            mode = "update"
            in_specs += [pl.BlockSpec(memory_space=pl.ANY)] * 2
            aliases = {len(args): 4, len(args) + 1: 5}
            args += list(cache)
            full = jax.ShapeDtypeStruct(cache[0].shape, F32)
            out_specs += [pl.BlockSpec((seqs, 1, seq_len, D_MODEL), lambda i: (i, slot, 0, 0))] * 2
        else:
            mode = "create"
            full = jax.ShapeDtypeStruct((rows // seq_len, n_slots, seq_len, D_MODEL), F32)
            out_specs += [pl.BlockSpec((seqs, n_slots, seq_len, D_MODEL), lambda i: (i, 0, 0, 0))] * 2
        out_shape += [full, full]
    kern = functools.partial(_qkv_kernel, tm=tm, rope=rope, cache=mode)
    return pl.pallas_call(
        kern,
        out_shape=out_shape,
        grid=(rows // tm,),
        in_specs=in_specs,
        out_specs=out_specs,
        input_output_aliases=aliases,
        compiler_params=_params(("arbitrary",)),
        name="qkv_proj",
    )(*args)


def _attn_kernel(*refs, hb, tq, lam_init, cache, lockstep):
    lam_ref, g_ref, q0_ref, q1_ref, k_ref, v_ref = refs[:6]
    if cache:
        ck_ref, cv_ref = refs[6:8]
    o_ref = refs[-1]
    lv = lam_ref[...]
    lam = (jnp.exp(jnp.sum(lv[0:1] * lv[1:2], axis=-1, keepdims=True))
           - jnp.exp(jnp.sum(lv[2:3] * lv[3:4], axis=-1, keepdims=True)) + lam_init)
    nt = (((1,), (1,)), ((), ()))

    def scores(hh):
        cols = slice(hh * V_DIM, (hh + 1) * V_DIM)
        qq = jnp.concatenate([q0_ref[:, cols], q1_ref[:, cols]], axis=0)
        s = lax.dot_general(qq, k_ref[:, cols], nt, preferred_element_type=F32)
        sc = None
        if cache:
            sc = lax.dot_general(qq, ck_ref[:, cols].astype(BF16), nt, preferred_element_type=F32)
        return s, sc

    def row_max(s, sc):
        m = jnp.max(s, axis=-1, keepdims=True)
        return jnp.maximum(m, jnp.max(sc, axis=-1, keepdims=True)) if cache else m

    def exponentials(s, sc, m):
        return jnp.exp2(s - m), (jnp.exp2(sc - m) if cache else None)

    def row_sum(p, pc):
        l = jnp.sum(p, axis=-1, keepdims=True)
        return l + jnp.sum(pc, axis=-1, keepdims=True) if cache else l

    def weighted_values(hh, p, pc, l):
        cols = slice(hh * V_DIM, (hh + 1) * V_DIM)
        ratio = l[:tq] * lam / l[tq:]
        w = (p[:tq] - p[tq:] * ratio).astype(BF16)
        o = jnp.dot(w, v_ref[:, cols], preferred_element_type=F32)
        if cache:
            wc = (pc[:tq] - pc[tq:] * ratio).astype(BF16)
            o = o + jnp.dot(wc, cv_ref[:, cols].astype(BF16), preferred_element_type=F32)
        return o / l[:tq]

    ones_col = (lax.broadcasted_iota(jnp.int32, (1, V_DIM), 1) == 0).astype(BF16)

    def with_ones(v):
        return jnp.concatenate([v, jnp.broadcast_to(ones_col, v.shape)], axis=1)

    def weighted_values_unnormalised(hh, p, pc):
        cols = slice(hh * V_DIM, (hh + 1) * V_DIM)
        o = jnp.dot(p.astype(BF16), with_ones(v_ref[:, cols]), preferred_element_type=F32)
        if cache:
            o = o + jnp.dot(pc.astype(BF16), with_ones(cv_ref[:, cols].astype(BF16)),
                            preferred_element_type=F32)
        o = o[:, :V_DIM] / o[:, V_DIM:V_DIM + 1]
        return o[:tq] - lam * o[tq:]

    def sub_norm(o):
        return lax.rsqrt(jnp.mean(o * o, axis=-1, keepdims=True) + SUBLN_EPS)

    def emit(hh, o, r):
        o_ref[:, hh * V_DIM:(hh + 1) * V_DIM] = (o * r * g_ref[...] * (1.0 - lam_init)).astype(BF16)

    heads = range(hb)
    if lockstep:
        S = [scores(hh) for hh in heads]
        M = [row_max(*S[hh]) for hh in heads]
        P = [exponentials(*S[hh], M[hh]) for hh in heads]
        L = [row_sum(*P[hh]) for hh in heads]
        O = [weighted_values(hh, *P[hh], L[hh]) for hh in heads]
        R = [sub_norm(O[hh]) for hh in heads]
        for hh in heads:
            emit(hh, O[hh], R[hh])
    else:
        pending = [scores(hh) for hh in range(min(SCORE_LOOKAHEAD, hb))]
        for hh in heads:
            if hh + SCORE_LOOKAHEAD < hb:
                pending.append(scores(hh + SCORE_LOOKAHEAD))
            s, sc = pending.pop(0)
            o = weighted_values_unnormalised(hh, *exponentials(s, sc, row_max(s, sc)))
            emit(hh, o, sub_norm(o))


def _attention(q0, q1, k, v, lam_vecs, subln_g, lam_init, seq_len, tq, hb, cache=None):
    rows = q0.shape[0]
    n_seq = rows // seq_len
    q_steps = seq_len // tq
    wb = hb * V_DIM
    with_cache = cache is not None
    q_spec = pl.BlockSpec((tq, wb), lambda s, h, j: (s * q_steps + j, h))
    kv_spec = pl.BlockSpec((seq_len, wb), lambda s, h, j: (s, h))
    in_specs = [_const_spec((4, HEAD_DIM)), _const_spec((1, V_DIM)), q_spec, q_spec, kv_spec, kv_spec]
    args = [lam_vecs, subln_g, q0, q1, k, v]
    if with_cache:
        ck, cv, slot = cache
        past = ck.shape[2]
        c_spec = pl.BlockSpec((None, None, past, wb), lambda s, h, j: (s, slot, 0, h))
        in_specs += [c_spec, c_spec]
        args += [ck, cv]
    n_keys = seq_len + (cache[0].shape[2] if with_cache else 0)
    lockstep = hb * 2 * tq * n_keys * 4 <= LOCKSTEP_SCORE_BYTES
    kern = functools.partial(_attn_kernel, hb=hb, tq=tq, lam_init=lam_init, cache=with_cache,
                             lockstep=lockstep)
    return pl.pallas_call(
        kern,
        out_shape=jax.ShapeDtypeStruct((rows, D_MODEL), BF16),
        grid=(n_seq, N_HEADS // hb, q_steps),
        in_specs=in_specs,
        out_specs=q_spec,
        compiler_params=_params(("arbitrary", "arbitrary", "arbitrary")),
        name="diff_attention",
    )(*args)


def _rope_tables(n):
    token = lax.broadcasted_iota(jnp.int32, (n, N_FREQ), 0)
    row = (token // GRID_W).astype(F32)
    col = (token % GRID_W).astype(F32)
    inv = 1.0 / (ROPE_THETA ** (jnp.arange(N_FREQ, dtype=F32) / N_FREQ))
    ar = row * inv
    ac = col * inv
    ang = jnp.concatenate([ar, ar, ac, ac], axis=-1)
    sign = jnp.tile(jnp.concatenate([-jnp.ones((N_FREQ,), F32), jnp.ones((N_FREQ,), F32)]), 2)
    cos = jnp.cos(ang)
    sin = jnp.sin(ang) * sign
    return jnp.tile(cos, (1, 2)), jnp.tile(sin, (1, 2))


def _tiles(ctx_len, lat_len):
    return dict(
        ffn_ctx=2 * ctx_len, ffn_lat=512,
        row_ctx=512, row_lat=1024,
        fourier_ctx=ctx_len, fourier_lat=512,
        attn_q_ctx=ctx_len, attn_q_lat=256,
        attn_heads_ctx=N_HEADS, attn_heads_lat=N_HEADS,
    )


def kernel(x_prompt, x_sample, cache_k, cache_v, c, c_ctx, w_ada, b_ada, w_fourier, w_qkv,
           lambda_q1, lambda_k1, lambda_q2, lambda_k2, subln_g, w_o, w_up, conv_w, conv_b,
           w_down, ln1_g, ln1_b, ln2_g, ln2_b):
    n_ctx_seq, ctx_len, d = x_prompt.shape
    n_lat_seq, lat_len, _ = x_sample.shape
    past = cache_k.shape[2]
    assert d == D_MODEL and n_lat_seq + 1 <= MOD_ROWS
    t = _tiles(ctx_len, lat_len)

    cvec = jnp.concatenate(
        [c_ctx[None, :], c, jnp.zeros((MOD_ROWS - 1 - n_lat_seq, d), F32)], axis=0)
    mods = _ada_table(cvec, w_ada, b_ada)

    xc = x_prompt.reshape(n_ctx_seq * ctx_len, d)
    xl = x_sample.reshape(n_lat_seq * lat_len, d)
    ck_in = cache_k.reshape(n_lat_seq, DEPTH // 2, past, d)
    cv_in = cache_v.reshape(n_lat_seq, DEPTH // 2, past, d)

    ctx_row = lambda i: 0
    lat_row_of = lambda tile: (lambda i: 1 + i // (lat_len // tile))
    cs = _channel_dft_table()
    dft_ctx, dft_lat = _dft_table(ctx_len), _dft_table(lat_len)
    rope = _rope_tables(lat_len)
    caches = ()

    wf, wqkv, wo = w_fourier.astype(BF16), w_qkv.astype(BF16), w_o.astype(BF16)
    chan = jnp.arange(D_MODEL)
    flip = (chan // FOURIER_GROUP) * FOURIER_GROUP + (FOURIER_GROUP - chan % FOURIER_GROUP) % FOURIER_GROUP
    wf_flipped = jnp.take(wf, flip, axis=1)
    tq_lat = t["fourier_lat"]
    reverse_rows = (lax.broadcasted_iota(jnp.int32, (tq_lat, tq_lat), 0)
                    + lax.broadcasted_iota(jnp.int32, (tq_lat, tq_lat), 1) == tq_lat - 1).astype(BF16)
    wup, wdn = w_up.astype(BF16), w_down.astype(BF16)
    cw, cb = conv_w, conv_b[:, None, :]
    g1, b1 = ln1_g[:, None, :], ln1_b[:, None, :]
    g2, b2 = ln2_g[:, None, :], ln2_b[:, None, :]

    for i in range(DEPTH):
        j = i // 2
        attn_c = attn_l = None
        if i % 2 == 0:
            xc = _fourier(xc, mods, i, ctx_len, ctx_row, cs, dft_ctx, wf, g1, b1, tq=t["fourier_ctx"])
            xl = _fourier(xl, mods, i, lat_len, lambda s: 1 + s, cs, dft_lat, wf, g1, b1,
                          tq=t["fourier_lat"], mirror=(wf_flipped, reverse_rows))
        else:
            lam_init = 0.8 - 0.6 * math.exp(-0.3 * i)
            lam_vecs = jnp.stack([lambda_q1[j], lambda_k1[j], lambda_q2[j], lambda_k2[j]])
            sg = subln_g[j][None, :]
            q0, q1, k, v, new_k, new_v = _qkv(
                xc, mods, i, ctx_len, ctx_row, wqkv, tm=t["row_ctx"], cache=caches, slot=j,
                n_slots=DEPTH // 2)
            caches = (new_k, new_v)
            oc = _attention(q0, q1, k, v, lam_vecs, sg, lam_init, ctx_len,
                            tq=t["attn_q_ctx"], hb=t["attn_heads_ctx"])
            lat_row = lat_row_of(t["row_lat"])
            q0, q1, k, v = _qkv(xl, mods, i, lat_len, lat_row, wqkv, tm=t["row_lat"], rope_tables=rope)
            ol = _attention(q0, q1, k, v, lam_vecs, sg, lam_init, lat_len,
                            tq=t["attn_q_lat"], hb=t["attn_heads_lat"], cache=(ck_in, cv_in, j))
            attn_c, attn_l = (oc, wo, g1, b1), (ol, wo, g1, b1)
        xc = _ffn(xc, mods, i, ctx_len, ctx_row, wup, cw, cb, wdn, g2, b2, tm=t["ffn_ctx"], attn=attn_c)
        xl = _ffn(xl, mods, i, lat_len, lat_row_of(t["ffn_lat"]), wup, cw, cb, wdn, g2, b2,
                  tm=t["ffn_lat"], attn=attn_l)

    y_prompt = xc.reshape(x_prompt.shape)
    y_sample = xl.reshape(x_sample.shape)
    new_cache_k = new_k.reshape(n_ctx_seq, DEPTH // 2, ctx_len, N_HEADS, 2, HEAD_DIM)
    new_cache_v = new_v.reshape(n_ctx_seq, DEPTH // 2, ctx_len, N_HEADS, V_DIM)
    return (y_prompt, y_sample, new_cache_k, new_cache_v)
```
